```python
import jax, jax.numpy as jnp
from jax import lax
import numpy as np

D_MODEL = 1024
BATCH = 8
SEQ = 8192
DEPTH = 1
DEC_BATCH = 8
DEC_SEQ = 16
PAST_LEN = 2048

CHUNK = 64
QBLK = 128
H_A = 8
DH_A = 64
FOX_W = H_A * DH_A
H_B = 4
DK_B = 128
DV_B = 256
GLA_K = H_B * DK_B
GLA_V = H_B * DV_B
GLA_RANK = 16
GLA_TAU = 16
D_FF = 2816
CONV_W = 3
PLE_DIM = 256
LN_EPS = 1e-5
ALPHA = (2 * DEPTH) ** 0.25
BETA = (8 * DEPTH) ** -0.25
IN_SIZES = (FOX_W, FOX_W, FOX_W, H_A, GLA_K, GLA_K, GLA_V, GLA_V, GLA_RANK, D_MODEL, D_MODEL)
D_IN = FOX_W * 3 + H_A + GLA_K * 2 + GLA_V * 2 + GLA_RANK + D_MODEL * 2

kernel_name = "fox_gla_convffn_deepnorm_stream_step"


def _split_points():
    pts, acc = [], 0
    for s in IN_SIZES[:-1]:
        acc += s
        pts.append(acc)
    return pts


def layer_norm(x, g, b):
    xf = x.astype(jnp.float32)
    mu = jnp.mean(xf, axis=-1, keepdims=True)
    var = jnp.mean(jnp.square(xf - mu), axis=-1, keepdims=True)
    y = (xf - mu) * lax.rsqrt(var + LN_EPS) * g.astype(jnp.float32) + b.astype(jnp.float32)
    return y.astype(x.dtype)


def fox_attend(q, k, v, cq, ck, q_pos, k_pos):
    s = jnp.einsum('bqhd,bkhd->bhqk', q, k).astype(jnp.float32) * (DH_A ** -0.5)
    s = s + jnp.transpose(cq, (0, 2, 1))[..., :, None] - jnp.transpose(ck, (0, 2, 1))[..., None, :]
    mask = k_pos[None, :] <= q_pos[:, None]
    s = jnp.where(mask, s, -jnp.inf)
    p = jax.nn.softmax(s, axis=-1)
    return jnp.einsum('bhqk,bkhd->bqhd', p.astype(v.dtype), v)


def fox_mixer(q, k, v, logf, past):
    T = q.shape[1]
    if past is None:
        c = jnp.cumsum(logf, axis=1)
        outs = []
        for i in range(T // QBLK):
            lo, hi = i * QBLK, (i + 1) * QBLK
            outs.append(fox_attend(q[:, lo:hi], k[:, :hi], v[:, :hi], c[:, lo:hi], c[:, :hi],
                                   jnp.arange(lo, hi), jnp.arange(hi)))
        return jnp.concatenate(outs, axis=1)
    past_k, past_v, past_logf = past
    P = past_k.shape[1]
    k_all = jnp.concatenate([past_k.astype(k.dtype), k], axis=1)
    v_all = jnp.concatenate([past_v.astype(v.dtype), v], axis=1)
    c = jnp.cumsum(jnp.concatenate([past_logf.astype(jnp.float32), logf], axis=1), axis=1)
    return fox_attend(q, k_all, v_all, c[:, P:], c, P + jnp.arange(T), jnp.arange(P + T))


def gla_block(S, inp):
    q, k, v, la = inp
    L = q.shape[2]
    b = jnp.cumsum(la, axis=2)
    tri = jnp.tril(jnp.ones((L, L), dtype=bool))
    diff = b[:, :, :, None, :] - b[:, :, None, :, :]
    decay = jnp.exp(jnp.where(tri[:, :, None], diff, -jnp.inf))
    a = jnp.einsum('bhtc,bhtsc,bhsc->bhts', q, decay, k)
    o = jnp.einsum('bhts,bhsv->bhtv', a, v) + jnp.einsum('bhtc,bhcv->bhtv', q * jnp.exp(b), S)
    b_last = b[:, :, -1:, :]
    S_new = jnp.exp(b_last[:, :, 0, :])[..., None] * S + jnp.einsum('bhsc,bhsv->bhcv', k * jnp.exp(b_last - b), v)
    return S_new, o


def gla_mixer(q, k, v, la, S0):
    B, T, H, _ = q.shape
    L = min(CHUNK, T)
    n = T // L

    def to_blocks(a):
        return a.astype(jnp.float32).reshape(B, n, L, H, a.shape[-1]).transpose(1, 0, 3, 2, 4)

    S_fin, o = lax.scan(gla_block, S0.astype(jnp.float32), (to_blocks(q), to_blocks(k), to_blocks(v), to_blocks(la)))
    o = o.transpose(1, 0, 3, 2, 4).reshape(B, T, H, v.shape[-1])
    return o, S_fin


def trunk_layer(x, p, fox_past, gla_S0, conv_prev,
                w_in, b_fgate, w_a2, b_a2, g_gla, w_a_out, w_b_out, w_o, ln1_g, ln1_b,
                w_up, conv_w, conv_b, w_down, ln2_g, ln2_b, w_pl, w_plg, ln3_g, ln3_b):
    B, T, _ = x.shape
    proj = x @ w_in
    qa, ka, va, fa, qb, kb, vb, rb, a1, ga, gb = jnp.split(proj, _split_points(), axis=-1)

    qa = qa.reshape(B, T, H_A, DH_A)
    ka = ka.reshape(B, T, H_A, DH_A)
    va = va.reshape(B, T, H_A, DH_A)
    logf = jax.nn.log_sigmoid((fa + b_fgate).astype(jnp.float32))
    oa = fox_mixer(qa, ka, va, logf, fox_past)
    ya = oa.reshape(B, T, FOX_W) @ w_a_out

    la = jax.nn.log_sigmoid((a1 @ w_a2 + b_a2).astype(jnp.float32)) / GLA_TAU
    ob, S_fin = gla_mixer(qb.reshape(B, T, H_B, DK_B) * (DK_B ** -0.5), kb.reshape(B, T, H_B, DK_B),
                          vb.reshape(B, T, H_B, DV_B), la.reshape(B, T, H_B, DK_B), gla_S0)
    ob = ob * lax.rsqrt(jnp.mean(jnp.square(ob), axis=-1, keepdims=True) + LN_EPS) \
        * g_gla.astype(jnp.float32).reshape(H_B, DV_B)
    yb = (ob.astype(x.dtype) * jax.nn.silu(rb.reshape(B, T, H_B, DV_B))).reshape(B, T, GLA_V) @ w_b_out

    merged = jax.nn.sigmoid(ga) * ya + jax.nn.sigmoid(gb) * yb
    x1 = layer_norm(ALPHA * x + merged @ w_o, ln1_g, ln1_b)

    up = x1 @ w_up
    ext = jnp.concatenate([conv_prev.astype(up.dtype), up], axis=1)
    conv = conv_b + sum(conv_w[j] * ext[:, j:j + T] for j in range(CONV_W))
    conv_new = ext[:, T:]
    val, gate = jnp.split(conv, 2, axis=-1)
    ffn = (val * jax.nn.gelu(gate)) @ w_down
    x2 = layer_norm(ALPHA * x1 + ffn, ln2_g, ln2_b)

    e = (p @ w_pl) * jax.nn.sigmoid(x2 @ w_plg)
    y = layer_norm(ALPHA * x2 + e, ln3_g, ln3_b)
    return y, ka, va, logf.astype(x.dtype), S_fin.astype(x.dtype), conv_new


def setup_inputs(seed: int = 0) -> dict:
    key = jax.random.key(seed)
    ks = jax.random.split(key, 32)

    def nrm(k, shape, scale):
        return jax.random.normal(k, shape, jnp.float32) * scale

    D = D_MODEL
    return {
        "x_prompt": nrm(ks[0], (BATCH, SEQ, D), 1.0),
        "x_sample": nrm(ks[1], (DEC_BATCH, DEC_SEQ, D), 1.0),
        "cache_fox_k": nrm(ks[2], (DEPTH, DEC_BATCH, PAST_LEN, H_A, DH_A), 1.0),
        "cache_fox_v": nrm(ks[3], (DEPTH, DEC_BATCH, PAST_LEN, H_A, DH_A), 1.0),
        "cache_fox_logf": jax.nn.log_sigmoid(nrm(ks[4], (DEPTH, DEC_BATCH, PAST_LEN, H_A), 1.0) + 3.0),
        "state_gla": nrm(ks[5], (DEPTH, DEC_BATCH, H_B, DK_B, DV_B), 0.1),
        "cache_ffn_conv": nrm(ks[6], (DEPTH, DEC_BATCH, CONV_W - 1, 2 * D_FF), 1.0),
        "p_prompt": nrm(ks[7], (DEPTH, BATCH, SEQ, PLE_DIM), 1.0),
        "p_sample": nrm(ks[8], (DEPTH, DEC_BATCH, DEC_SEQ, PLE_DIM), 1.0),
        "w_in": nrm(ks[9], (DEPTH, D, D_IN), D ** -0.5),
        "b_fgate": jax.random.uniform(ks[10], (DEPTH, H_A), jnp.float32, 1.0, 4.0),
        "w_a2": nrm(ks[11], (DEPTH, GLA_RANK, GLA_K), GLA_RANK ** -0.5),
        "b_a2": nrm(ks[12], (DEPTH, GLA_K), 0.1),
        "g_gla": 1.0 + nrm(ks[13], (DEPTH, GLA_V), 0.05),
        "w_a_out": nrm(ks[14], (DEPTH, FOX_W, D), FOX_W ** -0.5),
        "w_b_out": nrm(ks[15], (DEPTH, GLA_V, D), GLA_V ** -0.5),
        "w_o": nrm(ks[16], (DEPTH, D, D), D ** -0.5 * BETA),
        "ln1_g": 1.0 + nrm(ks[17], (DEPTH, D), 0.05),
        "ln1_b": nrm(ks[18], (DEPTH, D), 0.02),
        "w_up": nrm(ks[19], (DEPTH, D, 2 * D_FF), D ** -0.5),
        "conv_w": nrm(ks[20], (DEPTH, CONV_W, 2 * D_FF), CONV_W ** -0.5),
        "conv_b": nrm(ks[21], (DEPTH, 2 * D_FF), 0.02),
        "w_down": nrm(ks[22], (DEPTH, D_FF, D), D_FF ** -0.5 * BETA),
        "ln2_g": 1.0 + nrm(ks[23], (DEPTH, D), 0.05),
        "ln2_b": nrm(ks[24], (DEPTH, D), 0.02),
        "w_pl": nrm(ks[25], (DEPTH, PLE_DIM, D), PLE_DIM ** -0.5 * BETA),
        "w_plg": nrm(ks[26], (DEPTH, D, D), D ** -0.5),
        "ln3_g": 1.0 + nrm(ks[27], (DEPTH, D), 0.05),
        "ln3_b": nrm(ks[28], (DEPTH, D), 0.02),
    }


def reference(x_prompt, x_sample, cache_fox_k, cache_fox_v, cache_fox_logf, state_gla, cache_ffn_conv,
              p_prompt, p_sample, w_in, b_fgate, w_a2, b_a2, g_gla, w_a_out, w_b_out, w_o, ln1_g, ln1_b,
              w_up, conv_w, conv_b, w_down, ln2_g, ln2_b, w_pl, w_plg, ln3_g, ln3_b):
    hp, hs = x_prompt, x_sample
    kp_l, vp_l, fp_l, sp_l, cp_l = [], [], [], [], []
    ks_l, vs_l, fs_l, ss_l, cs_l = [], [], [], [], []
    for i in range(DEPTH):
        lp = (w_in[i], b_fgate[i], w_a2[i], b_a2[i], g_gla[i], w_a_out[i], w_b_out[i], w_o[i],
              ln1_g[i], ln1_b[i], w_up[i], conv_w[i], conv_b[i], w_down[i], ln2_g[i], ln2_b[i],
              w_pl[i], w_plg[i], ln3_g[i], ln3_b[i])
        S0_p = jnp.zeros((hp.shape[0], H_B, DK_B, DV_B), jnp.float32)
        conv0_p = jnp.zeros((hp.shape[0], CONV_W - 1, 2 * D_FF), hp.dtype)
        hp, kp, vp, fp, sp, cp = trunk_layer(hp, p_prompt[i], None, S0_p, conv0_p, *lp)
        hs, k_s, v_s, f_s, s_s, c_s = trunk_layer(
            hs, p_sample[i], (cache_fox_k[i], cache_fox_v[i], cache_fox_logf[i]),
            state_gla[i], cache_ffn_conv[i], *lp)
        kp_l.append(kp); vp_l.append(vp); fp_l.append(fp); sp_l.append(sp); cp_l.append(cp)
        ks_l.append(k_s); vs_l.append(v_s); fs_l.append(f_s); ss_l.append(s_s); cs_l.append(c_s)
    return (hp, hs,
            jnp.stack(kp_l), jnp.stack(vp_l), jnp.stack(fp_l), jnp.stack(sp_l), jnp.stack(cp_l),
            jnp.stack(ks_l), jnp.stack(vs_l), jnp.stack(fs_l), jnp.stack(ss_l), jnp.stack(cs_l))
```

```python
import functools
import math

import jax
import jax.numpy as jnp
from jax import lax
from jax.experimental import pallas as pl
from jax.experimental.pallas import tpu as pltpu

F32 = jnp.float32
BF16 = jnp.bfloat16

D_MODEL = 1024
H_A, DH_A = 8, 64
FOX_W = H_A * DH_A
H_B, DK_B, DV_B = 4, 128, 256
GLA_K = H_B * DK_B
GLA_V = H_B * DV_B
GLA_RANK = 16
GLA_TAU = 16
D_FF = 2816
CONV_W = 3
PLE_DIM = 256
LN_EPS = 1e-5
DEPTH = 1
ALPHA = (2 * DEPTH) ** 0.25
GLA_CHUNK = 64
GLA_SUB = 16

LANES = 128
V7X_VMEM_LIMIT_BYTES = 56 * 1024 * 1024

_SEG = {}
_off = 0
for _name, _n in (("qa", FOX_W), ("ka", FOX_W), ("va", FOX_W), ("qb", GLA_K), ("kb", GLA_K),
                  ("vb", GLA_V), ("rb", GLA_V), ("ga", D_MODEL), ("gb", D_MODEL),
                  ("fa", LANES), ("a1", LANES)):
    _SEG[_name] = (_off, _n)
    _off += _n
W_CAT = _off


def _params(*sem):
    return pltpu.CompilerParams(dimension_semantics=sem, vmem_limit_bytes=V7X_VMEM_LIMIT_BYTES)


def _const_spec(shape):
    nd = len(shape)
    return pl.BlockSpec(shape, lambda *_: (0,) * nd, pipeline_mode=pl.Buffered(1))


def _log_sigmoid(z):
    return jnp.minimum(z, 0.0) - jnp.log(1.0 + jnp.exp(-jnp.abs(z)))


def _sigmoid(z):
    return 1.0 / (1.0 + jnp.exp(-z))


def _split3(a):
    hi = a.astype(BF16)
    r = a - hi.astype(F32)
    mid = r.astype(BF16)
    lo = (r - mid.astype(F32)).astype(BF16)
    return hi, mid, lo


def _layer_norm(z, g, b):
    mu = jnp.mean(z, axis=-1, keepdims=True)
    zc = z - mu
    var = jnp.mean(zc * zc, axis=-1, keepdims=True)
    return zc * lax.rsqrt(var + LN_EPS) * g + b


def _dot(a, b):
    return jnp.dot(a, b, preferred_element_type=F32)


def _dot_nt(a, b):
    return lax.dot_general(a, b, (((1,), (1,)), ((), ())), preferred_element_type=F32)


def _dot_tn(a, b):
    return lax.dot_general(a, b, (((0,), (0,)), ((), ())), preferred_element_type=F32)


def _proj_kernel(x_ref, w_ref, bf_ref, wa2h_ref, wa2l_ref, ba2_ref,
                 qa_ref, kaf_ref, kab_ref, vaf_ref, vab_ref, logf_ref,
                 qb_ref, kb_ref, vb_ref, rb_ref, la_ref, ga_ref, gb_ref):
    xb = x_ref[...].astype(BF16)

    def seg(name):
        lo, n = _SEG[name]
        return _dot(xb, w_ref[:, lo:lo + n])

    qa_ref[...] = (seg("qa") * (DH_A ** -0.5)).astype(BF16)
    ka = seg("ka")
    kaf_ref[...] = ka
    kab_ref[...] = ka.astype(BF16)
    va = seg("va")
    vaf_ref[...] = va
    vab_ref[...] = va.astype(BF16)
    qb_ref[...] = seg("qb") * (DK_B ** -0.5)
    kb_ref[...] = seg("kb")
    vb_ref[...] = seg("vb").astype(BF16)
    rb_ref[...] = seg("rb")
    ga_ref[...] = seg("ga").astype(BF16)
    gb_ref[...] = seg("gb").astype(BF16)
    logf = _log_sigmoid(seg("fa") + bf_ref[...])
    logf_ref[...] = logf[:, :H_A]
    a1 = seg("a1")
    a1h = a1.astype(BF16)
    a1l = (a1 - a1h.astype(F32)).astype(BF16)
    z = _dot(a1h, wa2h_ref[...]) + _dot(a1l, wa2h_ref[...]) + _dot(a1h, wa2l_ref[...]) + ba2_ref[...]
    la_ref[...] = _log_sigmoid(z) * (1.0 / GLA_TAU)


def _proj(x, wcat, bf, wa2h, wa2l, ba2, tm):
    B, T, D = x.shape
    grid = (B, T // tm)

    def row(n, dt):
        return pl.BlockSpec((None, tm, n), lambda b, t: (b, t, 0)), jax.ShapeDtypeStruct((B, T, n), dt)

    outs = [row(FOX_W, BF16), row(FOX_W, F32), row(FOX_W, BF16), row(FOX_W, F32), row(FOX_W, BF16),
            row(H_A, F32), row(GLA_K, F32), row(GLA_K, F32), row(GLA_V, BF16), row(GLA_V, F32),
            row(GLA_K, F32), row(D_MODEL, BF16), row(D_MODEL, BF16)]
    return pl.pallas_call(
        _proj_kernel,
        grid=grid,
        in_specs=[pl.BlockSpec((None, tm, D), lambda b, t: (b, t, 0)),
                  _const_spec(wcat.shape), _const_spec(bf.shape), _const_spec(wa2h.shape),
                  _const_spec(wa2l.shape), _const_spec(ba2.shape)],
        out_specs=[o[0] for o in outs],
        out_shape=[o[1] for o in outs],
        compiler_params=_params("parallel", "parallel"),
        name="proj",
    )(x, wcat, bf, wa2h, wa2l, ba2)


def _cumsum_kernel(lf_ref, nc_ref, carry_ref):
    @pl.when(pl.program_id(1) == 0)
    def _():
        carry_ref[...] = jnp.zeros_like(carry_ref)

    x = lf_ref[...]
    tl = x.shape[1]
    r = lax.broadcasted_iota(jnp.int32, (tl, tl), 0)
    c = lax.broadcasted_iota(jnp.int32, (tl, tl), 1)
    tri = jnp.where(r <= c, 1.0, 0.0).astype(BF16)
    hi, mid, lo = _split3(x)
    cs = _dot(hi, tri) + _dot(mid, tri) + _dot(lo, tri) + carry_ref[...]
    nc_ref[...] = -cs
    carry_ref[...] = carry_ref[...] + jnp.sum(x, axis=1, keepdims=True)


def _neg_cumsum(lf_t, tl):
    B, H, L = lf_t.shape
    return pl.pallas_call(
        _cumsum_kernel,
        grid=(B, L // tl),
        in_specs=[pl.BlockSpec((None, H, tl), lambda b, t: (b, 0, t))],
        out_specs=pl.BlockSpec((None, H, tl), lambda b, t: (b, 0, t)),
        out_shape=jax.ShapeDtypeStruct((B, H, L), F32),
        scratch_shapes=[pltpu.VMEM((H, 1), F32)],
        compiler_params=_params("parallel", "arbitrary"),
        name="cumsum",
    )(lf_t)


def _head_masks(dtype):
    lane = lax.broadcasted_iota(jnp.int32, (1, LANES), 1)
    return lane < DH_A


def _fox_prompt_kernel(q_ref, k_ref, v_ref, nc_ref, o_ref, m_ref, l_ref, acc_ref, *, tq):
    i = pl.program_id(2)
    first = _head_masks(BF16)
    q = q_ref[...]
    zero = jnp.zeros_like(q)
    qh = (jnp.where(first, q, zero), jnp.where(first, zero, q))
    m_ref[...] = jnp.full_like(m_ref, -jnp.inf)
    l_ref[...] = jnp.zeros_like(l_ref)
    acc_ref[...] = jnp.zeros_like(acc_ref)

    def step(j, masked):
        start = pl.multiple_of(j * tq, tq)
        kb = k_ref[pl.ds(start, tq), :]
        vb = v_ref[pl.ds(start, tq), :]
        nc = nc_ref[j]
        for h in range(2):
            s = _dot_nt(qh[h], kb) + nc[h:h + 1, :]
            if masked:
                r = lax.broadcasted_iota(jnp.int32, (tq, tq), 0)
                c = lax.broadcasted_iota(jnp.int32, (tq, tq), 1)
                s = jnp.where(c <= r, s, -jnp.inf)
            m_old = m_ref[h]
            m_new = jnp.maximum(m_old, jnp.max(s, axis=1, keepdims=True))
            p = jnp.exp(s - m_new)
            a = jnp.exp(m_old - m_new)
            l_ref[h] = a * l_ref[h] + jnp.sum(p, axis=1, keepdims=True)
            acc_ref[h] = a * acc_ref[h] + _dot(p.astype(BF16), vb)
            m_ref[h] = m_new

    def body(j, carry):
        step(j, False)
        return carry

    lax.fori_loop(0, i, body, 0)
    step(i, True)
    o0 = acc_ref[0] / l_ref[0]
    o1 = acc_ref[1] / l_ref[1]
    o_ref[...] = jnp.where(first, o0, o1).astype(o_ref.dtype)


def _fox_prompt(qa, ka, va, nc, tq):
    B, T, _ = qa.shape
    nq = T // tq
    return pl.pallas_call(
        functools.partial(_fox_prompt_kernel, tq=tq),
        grid=(B, H_A // 2, nq),
        in_specs=[pl.BlockSpec((None, tq, LANES), lambda b, p, i: (b, i, p)),
                  pl.BlockSpec((None, T, LANES), lambda b, p, i: (b, 0, p)),
                  pl.BlockSpec((None, T, LANES), lambda b, p, i: (b, 0, p)),
                  pl.BlockSpec((None, None, nq, 2, tq), lambda b, p, i: (b, p, 0, 0, 0))],
        out_specs=pl.BlockSpec((None, tq, LANES), lambda b, p, i: (b, i, p)),
        out_shape=jax.ShapeDtypeStruct((B, T, FOX_W), BF16),
        scratch_shapes=[pltpu.VMEM((2, tq, 1), F32), pltpu.VMEM((2, tq, 1), F32),
                        pltpu.VMEM((2, tq, LANES), F32)],
        compiler_params=_params("parallel", "parallel", "arbitrary"),
        name="fox_prompt",
    )(qa, ka, va, nc)


def _fox_sample_kernel(q_ref, k_ref, v_ref, nc_ref, o_ref, *, past):
    first = _head_masks(BF16)
    q = q_ref[...]
    zero = jnp.zeros_like(q)
    tq = q.shape[0]
    kb = k_ref[...]
    vb = v_ref[...]
    lk = kb.shape[0]
    r = lax.broadcasted_iota(jnp.int32, (tq, lk), 0)
    c = lax.broadcasted_iota(jnp.int32, (tq, lk), 1)
    visible = c <= r + past
    outs = []
    for h in range(2):
        qh = jnp.where(first, q, zero) if h == 0 else jnp.where(first, zero, q)
        s = _dot_nt(qh, kb) + nc_ref[h:h + 1, :]
        s = jnp.where(visible, s, -jnp.inf)
        m = jnp.max(s, axis=1, keepdims=True)
        p = jnp.exp(s - m)
        l = jnp.sum(p, axis=1, keepdims=True)
        outs.append(_dot(p.astype(BF16), vb) / l)
    o_ref[...] = jnp.where(first, outs[0], outs[1]).astype(o_ref.dtype)


def _fox_sample(qa, k_all, v_all, nc, past):
    B, Tq, _ = qa.shape
    Lk = k_all.shape[1]
    return pl.pallas_call(
        functools.partial(_fox_sample_kernel, past=past),
        grid=(B, H_A // 2),
        in_specs=[pl.BlockSpec((None, Tq, LANES), lambda b, p: (b, 0, p)),
                  pl.BlockSpec((None, Lk, LANES), lambda b, p: (b, 0, p)),
                  pl.BlockSpec((None, Lk, LANES), lambda b, p: (b, 0, p)),
                  pl.BlockSpec((None, None, 2, Lk), lambda b, p: (b, p, 0, 0))],
        out_specs=pl.BlockSpec((None, Tq, LANES), lambda b, p: (b, 0, p)),
        out_shape=jax.ShapeDtypeStruct((B, Tq, FOX_W), BF16),
        compiler_params=_params("parallel", "parallel"),
        name="fox_sample",
    )(qa, k_all, v_all, nc)


def _gla_kernel(q_ref, k_ref, v_ref, la_ref, rb_ref, g_ref, s0_ref, hb_ref, sfin_ref, st_ref,
                *, chunk, sub):
    t = pl.program_id(1)

    @pl.when(t == 0)
    def _():
        st_ref[...] = s0_ref[...]

    tc = q_ref.shape[0]
    nsub = chunk // sub
    r = lax.broadcasted_iota(jnp.int32, (tc, tc), 0)
    c = lax.broadcasted_iota(jnp.int32, (tc, tc), 1)
    log_chunk = chunk.bit_length() - 1
    same_chunk = jnp.right_shift(r, log_chunk) == jnp.right_shift(c, log_chunk)
    tri = jnp.where(jnp.logical_and(c <= r, same_chunk), 1.0, 0.0).astype(BF16)
    hi, mid, lo = _split3(la_ref[...])
    b_all = _dot(tri, hi) + _dot(tri, mid) + _dot(tri, lo)

    rr = lax.broadcasted_iota(jnp.int32, (chunk, chunk), 0)
    cc = lax.broadcasted_iota(jnp.int32, (chunk, chunk), 1)
    causal = cc <= rr
    row_sub = jnp.right_shift(lax.broadcasted_iota(jnp.int32, (chunk, 1), 0), sub.bit_length() - 1)

    for ci in range(tc // chunk):
        r0 = ci * chunk
        for h in range(H_B):
            kc = slice(h * DK_B, (h + 1) * DK_B)
            vc = slice(h * DV_B, (h + 1) * DV_B)
            b = b_all[r0:r0 + chunk, kc]
            q = q_ref[r0:r0 + chunk, kc]
            k = k_ref[r0:r0 + chunk, kc]
            v = v_ref[r0:r0 + chunk, vc]
            b_last = b[chunk - 1:chunk, :]
            st = st_ref[h]
            o = _dot_nt((q * jnp.exp(b)).astype(BF16), st.astype(BF16))
            refs = [b[i * sub:i * sub + 1, :] for i in range(nsub)]
            ref_rows = jnp.concatenate([jnp.broadcast_to(rf, (sub, DK_B)) for rf in refs], axis=0)
            qt = q * jnp.exp(b - ref_rows)
            qx, kx = [], []
            for i in range(nsub):
                hi_row = (i + 1) * sub
                qx.append(jnp.where(row_sub == i, qt, 0.0))
                kt = k[:hi_row] * jnp.exp(refs[i] - b[:hi_row])
                if hi_row < chunk:
                    kt = jnp.concatenate([kt, jnp.zeros((chunk - hi_row, DK_B), F32)], axis=0)
                kx.append(kt)
            qx = jnp.concatenate(qx, axis=1).astype(BF16)
            kx = jnp.concatenate(kx, axis=1).astype(BF16)
            a = jnp.where(causal, _dot_nt(qx, kx), 0.0)
            o = o + _dot(a.astype(BF16), v)
            kd = (k * jnp.exp(b_last - b)).astype(BF16)
            st_ref[h] = st * jnp.exp(b_last) + _dot_tn(v, kd)
            ms = jnp.mean(o * o, axis=-1, keepdims=True)
            on = o * lax.rsqrt(ms + LN_EPS) * g_ref[:, vc]
            rbv = rb_ref[r0:r0 + chunk, vc]
            hb_ref[r0:r0 + chunk, vc] = (on * (rbv * _sigmoid(rbv))).astype(hb_ref.dtype)

    @pl.when(t == pl.num_programs(1) - 1)
    def _():
        sfin_ref[...] = st_ref[...]


def _gla(qb, kb, vb, la, rb, g, s0_t, tc, chunk):
    B, T, _ = qb.shape
    sub = min(GLA_SUB, chunk)

    def row(n):
        return pl.BlockSpec((None, tc, n), lambda b, t: (b, t, 0))

    st_spec = pl.BlockSpec((None, H_B, DV_B, DK_B), lambda b, t: (b, 0, 0, 0))
    return pl.pallas_call(
        functools.partial(_gla_kernel, chunk=chunk, sub=sub),
        grid=(B, T // tc),
        in_specs=[row(GLA_K), row(GLA_K), row(GLA_V), row(GLA_K), row(GLA_V),
                  _const_spec(g.shape), st_spec],
        out_specs=[row(GLA_V), st_spec],
        out_shape=[jax.ShapeDtypeStruct((B, T, GLA_V), BF16),
                   jax.ShapeDtypeStruct((B, H_B, DV_B, DK_B), F32)],
        scratch_shapes=[pltpu.VMEM((H_B, DV_B, DK_B), F32)],
        compiler_params=_params("parallel", "arbitrary"),
        name="gla",
    )(qb, kb, vb, la, rb, g, s0_t)


def _merge_kernel(oa_ref, hb_ref, ga_ref, gb_ref, x_ref, wa_ref, wb_ref, wo_ref, g_ref, b_ref, x1_ref):
    ya = _dot(oa_ref[...], wa_ref[...])
    yb = _dot(hb_ref[...], wb_ref[...])
    merged = _sigmoid(ga_ref[...].astype(F32)) * ya + _sigmoid(gb_ref[...].astype(F32)) * yb
    z = ALPHA * x_ref[...] + _dot(merged.astype(BF16), wo_ref[...])
    x1_ref[...] = _layer_norm(z, g_ref[...], b_ref[...])


def _merge(oa, hb, ga, gb, x, wa, wb, wo, g, b, tm):
    B, T, D = x.shape

    def row(n):
        return pl.BlockSpec((None, tm, n), lambda bb, t: (bb, t, 0))

    return pl.pallas_call(
        _merge_kernel,
        grid=(B, T // tm),
        in_specs=[row(FOX_W), row(GLA_V), row(D), row(D), row(D),
                  _const_spec(wa.shape), _const_spec(wb.shape), _const_spec(wo.shape),
                  _const_spec(g.shape), _const_spec(b.shape)],
        out_specs=row(D),
        out_shape=jax.ShapeDtypeStruct((B, T, D), F32),
        compiler_params=_params("parallel", "parallel"),
        name="merge",
    )(oa, hb, ga, gb, x, wa, wb, wo, g, b)


def _gelu_tanh(x):
    return 0.5 * x * (1.0 + jnp.tanh(math.sqrt(2.0 / math.pi) * (x + 0.044715 * (x * x * x))))


def _ffn_kernel(x1_ref, p_ref, cprev_ref, wup_ref, cw_ref, cb_ref, wdn_ref, g2_ref, b2_ref,
                wpl_ref, wplg_ref, g3_ref, b3_ref, y_ref, cnew_ref, carry_ref):
    @pl.when(pl.program_id(1) == 0)
    def _():
        carry_ref[...] = cprev_ref[...]

    x1 = x1_ref[...]
    tm = x1.shape[0]
    up = _dot(x1.astype(BF16), wup_ref[...])
    prev = carry_ref[...]
    row = lax.broadcasted_iota(jnp.int32, (tm, 1), 0)
    up_m1 = jnp.where(row == 0, prev[1:2, :], pltpu.roll(up, 1, 0))
    up_m2 = jnp.where(row == 0, prev[0:1, :], jnp.where(row == 1, prev[1:2, :], pltpu.roll(up, 2, 0)))
    conv = cb_ref[...] + cw_ref[0:1, :] * up_m2 + cw_ref[1:2, :] * up_m1 + cw_ref[2:3, :] * up
    last2 = up[tm - 2:tm, :]
    carry_ref[...] = last2
    cnew_ref[...] = last2
    hgl = (conv[:, :D_FF] * _gelu_tanh(conv[:, D_FF:])).astype(BF16)
    z2 = ALPHA * x1 + _dot(hgl, wdn_ref[...])
    x2 = _layer_norm(z2, g2_ref[...], b2_ref[...])
    e = _dot(p_ref[...].astype(BF16), wpl_ref[...]) * _sigmoid(_dot(x2.astype(BF16), wplg_ref[...]))
    y_ref[...] = _layer_norm(ALPHA * x2 + e, g3_ref[...], b3_ref[...])


def _ffn(x1, p, cprev, wup, cw, cb, wdn, g2, b2, wpl, wplg, g3, b3, tm):
    B, T, D = x1.shape

    def row(n):
        return pl.BlockSpec((None, tm, n), lambda bb, t: (bb, t, 0))

    cspec = pl.BlockSpec((None, CONV_W - 1, 2 * D_FF), lambda bb, t: (bb, 0, 0))
    consts = (wup, cw, cb, wdn, g2, b2, wpl, wplg, g3, b3)
    return pl.pallas_call(
        _ffn_kernel,
        grid=(B, T // tm),
        in_specs=[row(D), row(PLE_DIM), cspec] + [_const_spec(c.shape) for c in consts],
        out_specs=[row(D), cspec],
        out_shape=[jax.ShapeDtypeStruct((B, T, D), F32),
                   jax.ShapeDtypeStruct((B, CONV_W - 1, 2 * D_FF), F32)],
        scratch_shapes=[pltpu.VMEM((CONV_W - 1, 2 * D_FF), F32)],
        compiler_params=_params("parallel", "arbitrary"),
        name="ffn",
    )(x1, p, cprev, *consts)


def _pick(T, want):
    t = min(T, want)
    while T % t:
        t //= 2
    return t


def _pack_w_in(w_in):
    pts, acc = {}, 0
    for name, n in (("qa", FOX_W), ("ka", FOX_W), ("va", FOX_W), ("fa", H_A), ("qb", GLA_K), ("kb", GLA_K),
                    ("vb", GLA_V), ("rb", GLA_V), ("a1", GLA_RANK), ("ga", D_MODEL), ("gb", D_MODEL)):
        pts[name] = (acc, n)
        acc += n
    cols = []
    for name in _SEG:
        lo, n = pts[name]
        piece = w_in[:, lo:lo + n]
        if n < _SEG[name][1]:
            piece = jnp.pad(piece, ((0, 0), (0, _SEG[name][1] - n)))
        cols.append(piece)
    return jnp.concatenate(cols, axis=1).astype(BF16)


def _layer(x, p, fox_past, s0, conv_prev, wts):
    (w_in, b_fgate, w_a2, b_a2, g_gla, w_a_out, w_b_out, w_o, ln1_g, ln1_b, w_up, conv_w, conv_b,
     w_down, ln2_g, ln2_b, w_pl, w_plg, ln3_g, ln3_b) = wts
    B, T, D = x.shape
    row2 = lambda a: a.reshape(1, -1).astype(F32)

    wcat = _pack_w_in(w_in)
    bf = jnp.pad(row2(b_fgate), ((0, 0), (0, LANES - H_A)))
    wa2 = jnp.pad(w_a2.astype(F32), ((0, LANES - GLA_RANK), (0, 0)))
    wa2h = wa2.astype(BF16)
    wa2l = (wa2 - wa2h.astype(F32)).astype(BF16)

    tm = _pick(T, 256)
    (qa, ka_f, ka_b, va_f, va_b, logf, qb, kb, vb, rb, la, ga, gb) = _proj(
        x, wcat, bf, wa2h, wa2l, row2(b_a2), tm)

    if fox_past is None:
        tq = _pick(T, 512)
        nc = _neg_cumsum(jnp.transpose(logf, (0, 2, 1)), _pick(T, 512))
        nc = nc.reshape(B, H_A // 2, 2, T // tq, tq).transpose(0, 1, 3, 2, 4)
        oa = _fox_prompt(qa, ka_b, va_b, nc, tq)
    else:
        past_k, past_v, past_logf = fox_past
        P = past_k.shape[1]
        lk = -(-(P + T) // LANES) * LANES
        padk = lambda a: jnp.pad(a, ((0, 0), (0, lk - P - T), (0, 0)))
        k_all = padk(jnp.concatenate([past_k.reshape(B, P, FOX_W).astype(BF16), ka_b], axis=1))
        v_all = padk(jnp.concatenate([past_v.reshape(B, P, FOX_W).astype(BF16), va_b], axis=1))
        lf_all = padk(jnp.concatenate([past_logf.astype(F32), logf], axis=1))
        nc = _neg_cumsum(jnp.transpose(lf_all, (0, 2, 1)), _pick(lk, 512))
        oa = _fox_sample(qa, k_all, v_all, nc.reshape(B, H_A // 2, 2, lk), P)

    chunk = min(GLA_CHUNK, T)
    tc = _pick(T, 4 * chunk)
    hb, s_t = _gla(qb, kb, vb, la, rb, row2(g_gla), jnp.swapaxes(s0.astype(F32), 2, 3), tc, chunk)
    s_fin = jnp.swapaxes(s_t, 2, 3)

    x1 = _merge(oa, hb, ga, gb, x, w_a_out.astype(BF16), w_b_out.astype(BF16), w_o.astype(BF16),
                row2(ln1_g), row2(ln1_b), tm)
    y, conv_new = _ffn(x1, p, conv_prev.astype(F32), w_up.astype(BF16), conv_w.astype(F32), row2(conv_b),
                       w_down.astype(BF16), row2(ln2_g), row2(ln2_b), w_pl.astype(BF16),
                       w_plg.astype(BF16), row2(ln3_g), row2(ln3_b), tm)
    k_out = ka_f.reshape(B, T, H_A, DH_A)
    v_out = va_f.reshape(B, T, H_A, DH_A)
    return y, k_out, v_out, logf, s_fin, conv_new


def kernel(x_prompt, x_sample, cache_fox_k, cache_fox_v, cache_fox_logf, state_gla, cache_ffn_conv, p_prompt, p_sample, w_in, b_fgate, w_a2, b_a2, g_gla, w_a_out, w_b_out, w_o, ln1_g, ln1_b, w_up, conv_w, conv_b, w_down, ln2_g, ln2_b, w_pl, w_plg, ln3_g, ln3_b):
    hp, hs = x_prompt, x_sample
    outs_p = [[] for _ in range(5)]
    outs_s = [[] for _ in range(5)]
    for i in range(DEPTH):
        wts = (w_in[i], b_fgate[i], w_a2[i], b_a2[i], g_gla[i], w_a_out[i], w_b_out[i], w_o[i],
               ln1_g[i], ln1_b[i], w_up[i], conv_w[i], conv_b[i], w_down[i], ln2_g[i], ln2_b[i],
               w_pl[i], w_plg[i], ln3_g[i], ln3_b[i])
        Bp = hp.shape[0]
        s0_p = jnp.zeros((Bp, H_B, DK_B, DV_B), F32)
        c0_p = jnp.zeros((Bp, CONV_W - 1, 2 * D_FF), F32)
        hp, *rest_p = _layer(hp, p_prompt[i], None, s0_p, c0_p, wts)
        hs, *rest_s = _layer(hs, p_sample[i], (cache_fox_k[i], cache_fox_v[i], cache_fox_logf[i]),
                             state_gla[i], cache_ffn_conv[i], wts)
        for dst, src in ((outs_p, rest_p), (outs_s, rest_s)):
            for lst, val in zip(dst, src):
                lst.append(val)
    return (hp, hs, *[jnp.stack(l) for l in outs_p], *[jnp.stack(l) for l in outs_s])
```

```python
import functools
import math

import jax
import jax.numpy as jnp
from jax import lax
from jax.experimental import pallas as pl
from jax.experimental.pallas import tpu as pltpu

F32 = jnp.float32
BF16 = jnp.bfloat16

D_MODEL = 1024
H_A, DH_A = 8, 64
FOX_W = H_A * DH_A
H_B, DK_B, DV_B = 4, 128, 256
GLA_K = H_B * DK_B
GLA_V = H_B * DV_B
GLA_RANK = 16
GLA_TAU = 16
D_FF = 2816
CONV_W = 3
PLE_DIM = 256
LN_EPS = 1e-5
DEPTH = 1
ALPHA = (2 * DEPTH) ** 0.25
GLA_CHUNK = 64
GLA_SUB = 16

LANES = 128
LOG2E = math.log2(math.e)
V7X_VMEM_LIMIT_BYTES = 56 * 1024 * 1024

_SEG = {}
_off = 0
for _name, _n in (("qa", FOX_W), ("ka", FOX_W), ("va", FOX_W), ("qb", GLA_K), ("kb", GLA_K),
                  ("vb", GLA_V), ("rb", GLA_V), ("ga", D_MODEL), ("gb", D_MODEL),
                  ("fa", LANES), ("a1", LANES)):
    _SEG[_name] = (_off, _n)
    _off += _n
W_CAT = _off


def _params(*sem):
    return pltpu.CompilerParams(dimension_semantics=sem, vmem_limit_bytes=V7X_VMEM_LIMIT_BYTES)


def _const_spec(shape):
    nd = len(shape)
    return pl.BlockSpec(shape, lambda *_: (0,) * nd, pipeline_mode=pl.Buffered(1))


def _log_sigmoid(z):
    return jnp.minimum(z, 0.0) - jnp.log(1.0 + jnp.exp(-jnp.abs(z)))


def _sigmoid(z):
    return 1.0 / (1.0 + jnp.exp(-z))


def _split3(a):
    hi = a.astype(BF16)
    r = a - hi.astype(F32)
    mid = r.astype(BF16)
    lo = (r - mid.astype(F32)).astype(BF16)
    return hi, mid, lo


def _layer_norm(z, g, b):
    mu = jnp.mean(z, axis=-1, keepdims=True)
    zc = z - mu
    var = jnp.mean(zc * zc, axis=-1, keepdims=True)
    return zc * lax.rsqrt(var + LN_EPS) * g + b


def _dot(a, b):
    return jnp.dot(a, b, preferred_element_type=F32)


def _dot_nt(a, b):
    return lax.dot_general(a, b, (((1,), (1,)), ((), ())), preferred_element_type=F32)


def _dot_tn(a, b):
    return lax.dot_general(a, b, (((0,), (0,)), ((), ())), preferred_element_type=F32)


def _proj_kernel(x_ref, w_ref, bf_ref, wa2h_ref, wa2l_ref, ba2_ref,
                 qa_ref, kaf_ref, kab_ref, vaf_ref, vab_ref, logf_ref, logfp_ref,
                 qb_ref, kb_ref, vb_ref, rb_ref, la_ref, ga_ref, gb_ref):
    xb = x_ref[...].astype(BF16)

    def seg(name):
        lo, n = _SEG[name]
        return _dot(xb, w_ref[:, lo:lo + n])

    qa_ref[...] = (seg("qa") * (DH_A ** -0.5 * LOG2E)).astype(BF16)
    ka = seg("ka")
    kaf_ref[...] = ka
    kab_ref[...] = ka.astype(BF16)
    va = seg("va")
    vaf_ref[...] = va
    vab_ref[...] = va.astype(BF16)
    qb_ref[...] = seg("qb") * (DK_B ** -0.5)
    kb_ref[...] = seg("kb")
    vb_ref[...] = seg("vb").astype(BF16)
    rb_ref[...] = seg("rb")
    ga_ref[...] = seg("ga").astype(BF16)
    gb_ref[...] = seg("gb").astype(BF16)
    logf = _log_sigmoid(seg("fa") + bf_ref[...])
    logf_ref[...] = logf[:, :H_A]
    logfp_ref[...] = logf
    a1 = seg("a1")
    a1h = a1.astype(BF16)
    a1l = (a1 - a1h.astype(F32)).astype(BF16)
    z = _dot(a1h, wa2h_ref[...]) + _dot(a1l, wa2h_ref[...]) + _dot(a1h, wa2l_ref[...]) + ba2_ref[...]
    la_ref[...] = _log_sigmoid(z) * (1.0 / GLA_TAU)


def _proj(x, wcat, bf, wa2h, wa2l, ba2, tm):
    B, T, D = x.shape
    grid = (B, T // tm)

    def row(n, dt):
        return pl.BlockSpec((None, tm, n), lambda b, t: (b, t, 0)), jax.ShapeDtypeStruct((B, T, n), dt)

    outs = [row(FOX_W, BF16), row(FOX_W, F32), row(FOX_W, BF16), row(FOX_W, F32), row(FOX_W, BF16),
            row(H_A, F32), row(LANES, F32), row(GLA_K, F32), row(GLA_K, F32), row(GLA_V, BF16), row(GLA_V, F32),
            row(GLA_K, F32), row(D_MODEL, BF16), row(D_MODEL, BF16)]
    return pl.pallas_call(
        _proj_kernel,
        grid=grid,
        in_specs=[pl.BlockSpec((None, tm, D), lambda b, t: (b, t, 0)),
                  _const_spec(wcat.shape), _const_spec(bf.shape), _const_spec(wa2h.shape),
                  _const_spec(wa2l.shape), _const_spec(ba2.shape)],
        out_specs=[o[0] for o in outs],
        out_shape=[o[1] for o in outs],
        compiler_params=_params("parallel", "parallel"),
        name="proj",
    )(x, wcat, bf, wa2h, wa2l, ba2)


def _cumsum_kernel(lf_ref, nc_ref, carry_ref):
    @pl.when(pl.program_id(1) == 0)
    def _():
        carry_ref[...] = jnp.zeros_like(carry_ref)

    x = lf_ref[...]
    tl = x.shape[1]
    r = lax.broadcasted_iota(jnp.int32, (tl, tl), 0)
    c = lax.broadcasted_iota(jnp.int32, (tl, tl), 1)
    tri = jnp.where(r <= c, 1.0, 0.0).astype(BF16)
    hi, mid, lo = _split3(x)
    cs = _dot(hi, tri) + _dot(mid, tri) + _dot(lo, tri) + carry_ref[...]
    nc_ref[...] = -cs
    carry_ref[...] = carry_ref[...] + jnp.sum(x, axis=1, keepdims=True)


def _neg_cumsum(lf_t, tl):
    B, H, L = lf_t.shape
    return pl.pallas_call(
        _cumsum_kernel,
        grid=(B, L // tl),
        in_specs=[pl.BlockSpec((None, H, tl), lambda b, t: (b, 0, t))],
        out_specs=pl.BlockSpec((None, H, tl), lambda b, t: (b, 0, t)),
        out_shape=jax.ShapeDtypeStruct((B, H, L), F32),
        scratch_shapes=[pltpu.VMEM((H, 1), F32)],
        compiler_params=_params("parallel", "arbitrary"),
        name="cumsum",
    )(lf_t)


FOX_BIAS_PARTS = 3


def _fox_bias_sel():
    sel = [[[0.0] * FOX_W for _ in range(LANES)] for _ in range(FOX_BIAS_PARTS)]
    for h in range(H_A):
        for j in range(FOX_BIAS_PARTS):
            sel[j][h][(h // 2) * LANES + FOX_BIAS_PARTS * (h % 2) + j] = 1.0
    return jnp.asarray(sel, dtype=BF16)


def _fox_bias_kernel(lf_ref, sel_ref, kb_ref, carry_ref):
    @pl.when(pl.program_id(1) == 0)
    def _():
        carry_ref[...] = jnp.zeros_like(carry_ref)

    x = lf_ref[...]
    tl = x.shape[0]
    r = lax.broadcasted_iota(jnp.int32, (tl, tl), 0)
    c = lax.broadcasted_iota(jnp.int32, (tl, tl), 1)
    tri = jnp.where(c <= r, 1.0, 0.0).astype(BF16)
    hi, mid, lo = _split3(x)
    cs = _dot(tri, hi) + _dot(tri, mid) + _dot(tri, lo) + carry_ref[...]
    carry_ref[...] = cs[tl - 1:tl, :]
    parts = _split3(cs * (-LOG2E))
    kb_ref[...] = (_dot(parts[0], sel_ref[0]) + _dot(parts[1], sel_ref[1])
                   + _dot(parts[2], sel_ref[2])).astype(kb_ref.dtype)


def _fox_bias(logf_pad, tl):
    B, T, _ = logf_pad.shape
    sel = _fox_bias_sel()
    return pl.pallas_call(
        _fox_bias_kernel,
        grid=(B, T // tl),
        in_specs=[pl.BlockSpec((None, tl, LANES), lambda b, t: (b, t, 0)), _const_spec(sel.shape)],
        out_specs=pl.BlockSpec((None, tl, FOX_W), lambda b, t: (b, t, 0)),
        out_shape=jax.ShapeDtypeStruct((B, T, FOX_W), BF16),
        scratch_shapes=[pltpu.VMEM((1, LANES), F32)],
        compiler_params=_params("parallel", "arbitrary"),
        name="fox_bias",
    )(logf_pad, sel)


def _fox_prompt_kernel(q_ref, k_ref, kb_ref, vt_ref, o_ref, m_ref, l_ref, acc_ref,
                       sa_ref, sb_ref, bma_ref, bmb_ref, *, tq):
    i = pl.program_id(2)
    lane = lax.broadcasted_iota(jnp.int32, (1, LANES), 1)
    q = q_ref[...]
    zero = jnp.zeros_like(q)
    qa = []
    for h in range(2):
        own = (lane < DH_A) if h == 0 else (lane >= DH_A)
        ones = jnp.logical_and(lane >= FOX_BIAS_PARTS * h, lane < FOX_BIAS_PARTS * (h + 1))
        ones = jnp.broadcast_to(jnp.where(ones, 1.0, 0.0).astype(q.dtype), q.shape)
        qa.append(jnp.concatenate([jnp.where(own, q, zero), ones], axis=1))
    m_ref[...] = jnp.full_like(m_ref, -jnp.inf)
    l_ref[...] = jnp.zeros_like(l_ref)
    acc_ref[...] = jnp.zeros_like(acc_ref)

    def scores(j, s_ref, bm_ref):
        start = pl.multiple_of(j * tq, tq)
        ka = jnp.concatenate([k_ref[pl.ds(start, tq), :], kb_ref[pl.ds(start, tq), :]], axis=1)
        for h in range(2):
            st = _dot_nt(ka, qa[h])
            s_ref[h] = st
            bm_ref[h] = jnp.max(st, axis=0, keepdims=True)

    def soft_pv(j, s_ref, bm_ref, masked):
        for h in range(2):
            st = s_ref[h]
            if masked:
                r = lax.broadcasted_iota(jnp.int32, (tq, tq), 0)
                c = lax.broadcasted_iota(jnp.int32, (tq, tq), 1)
                st = jnp.where(r <= c, st, -jnp.inf)
                bm = jnp.max(st, axis=0, keepdims=True)
            else:
                bm = bm_ref[h]
            m_old = m_ref[h]
            m_new = jnp.maximum(m_old, bm)
            p = jnp.exp2(st - m_new)
            a = jnp.exp2(m_old - m_new)
            l_ref[h] = a * l_ref[h] + jnp.sum(p, axis=0, keepdims=True)
            vt = vt_ref[j, h * DH_A:(h + 1) * DH_A, :]
            acc_ref[h] = a * acc_ref[h] + _dot(vt, p.astype(BF16))
            m_ref[h] = m_new

    scores(0, sa_ref, bma_ref)

    def pair(jj, carry):
        j = 2 * jj
        scores(j + 1, sb_ref, bmb_ref)
        soft_pv(j, sa_ref, bma_ref, False)
        scores(j + 2, sa_ref, bma_ref)
        soft_pv(j + 1, sb_ref, bmb_ref, False)
        return carry

    lax.fori_loop(0, lax.shift_right_logical(i, 1), pair, 0)
    odd = jnp.bitwise_and(i, 1) == 1

    @pl.when(odd)
    def _():
        scores(i, sb_ref, bmb_ref)
        soft_pv(i - 1, sa_ref, bma_ref, False)
        soft_pv(i, sb_ref, bmb_ref, True)

    @pl.when(jnp.logical_not(odd))
    def _():
        soft_pv(i, sa_ref, bma_ref, True)

    ot = jnp.concatenate([acc_ref[0] / l_ref[0], acc_ref[1] / l_ref[1]], axis=0)
    o_ref[...] = ot.T.astype(o_ref.dtype)


def _fox_prompt(qa, ka, kbias, vt, tq):
    B, T, _ = qa.shape
    nq = T // tq
    return pl.pallas_call(
        functools.partial(_fox_prompt_kernel, tq=tq),
        grid=(B, H_A // 2, nq),
        in_specs=[pl.BlockSpec((None, tq, LANES), lambda b, p, i: (b, i, p)),
                  pl.BlockSpec((None, T, LANES), lambda b, p, i: (b, 0, p)),
                  pl.BlockSpec((None, T, LANES), lambda b, p, i: (b, 0, p)),
                  pl.BlockSpec((None, None, nq, LANES, tq), lambda b, p, i: (b, p, 0, 0, 0))],
        out_specs=pl.BlockSpec((None, tq, LANES), lambda b, p, i: (b, i, p)),
        out_shape=jax.ShapeDtypeStruct((B, T, FOX_W), BF16),
        scratch_shapes=[pltpu.VMEM((2, 1, tq), F32), pltpu.VMEM((2, 1, tq), F32),
                        pltpu.VMEM((2, DH_A, tq), F32),
                        pltpu.VMEM((2, tq, tq), F32), pltpu.VMEM((2, tq, tq), F32),
                        pltpu.VMEM((2, 1, tq), F32), pltpu.VMEM((2, 1, tq), F32)],
        compiler_params=_params("parallel", "parallel", "arbitrary"),
        name="fox_prompt",
    )(qa, ka, kbias, vt)


def _fox_sample_kernel(q_ref, k_ref, v_ref, nc_ref, o_ref, *, past):
    first = lax.broadcasted_iota(jnp.int32, (1, LANES), 1) < DH_A
    q = q_ref[...]
    zero = jnp.zeros_like(q)
    tq = q.shape[0]
    kb = k_ref[...]
    vb = v_ref[...]
    lk = kb.shape[0]
    r = lax.broadcasted_iota(jnp.int32, (tq, lk), 0)
    c = lax.broadcasted_iota(jnp.int32, (tq, lk), 1)
    visible = c <= r + past
    outs = []
    for h in range(2):
        qh = jnp.where(first, q, zero) if h == 0 else jnp.where(first, zero, q)
        s = _dot_nt(qh, kb) + nc_ref[h:h + 1, :] * LOG2E
        s = jnp.where(visible, s, -jnp.inf)
        m = jnp.max(s, axis=1, keepdims=True)
        p = jnp.exp2(s - m)
        l = jnp.sum(p, axis=1, keepdims=True)
        outs.append(_dot(p.astype(BF16), vb) / l)
    o_ref[...] = jnp.where(first, outs[0], outs[1]).astype(o_ref.dtype)


def _fox_sample(qa, k_all, v_all, nc, past):
    B, Tq, _ = qa.shape
    Lk = k_all.shape[1]
    return pl.pallas_call(
        functools.partial(_fox_sample_kernel, past=past),
        grid=(B, H_A // 2),
        in_specs=[pl.BlockSpec((None, Tq, LANES), lambda b, p: (b, 0, p)),
                  pl.BlockSpec((None, Lk, LANES), lambda b, p: (b, 0, p)),
                  pl.BlockSpec((None, Lk, LANES), lambda b, p: (b, 0, p)),
                  pl.BlockSpec((None, None, 2, Lk), lambda b, p: (b, p, 0, 0))],
        out_specs=pl.BlockSpec((None, Tq, LANES), lambda b, p: (b, 0, p)),
        out_shape=jax.ShapeDtypeStruct((B, Tq, FOX_W), BF16),
        compiler_params=_params("parallel", "parallel"),
        name="fox_sample",
    )(qa, k_all, v_all, nc)


def _gla_kernel(q_ref, k_ref, v_ref, la_ref, rb_ref, g_ref, s0_ref, hb_ref, sfin_ref, st_ref,
                *, chunk, sub):
    t = pl.program_id(1)

    @pl.when(t == 0)
    def _():
        st_ref[...] = s0_ref[...]

    tc = q_ref.shape[0]
    nsub = chunk // sub
    r = lax.broadcasted_iota(jnp.int32, (tc, tc), 0)
    c = lax.broadcasted_iota(jnp.int32, (tc, tc), 1)
    log_chunk = chunk.bit_length() - 1
    same_chunk = jnp.right_shift(r, log_chunk) == jnp.right_shift(c, log_chunk)
    tri = jnp.where(jnp.logical_and(c <= r, same_chunk), 1.0, 0.0).astype(BF16)
    hi, mid, lo = _split3(la_ref[...])
    b_all = _dot(tri, hi) + _dot(tri, mid) + _dot(tri, lo)

    rr = lax.broadcasted_iota(jnp.int32, (chunk, chunk), 0)
    cc = lax.broadcasted_iota(jnp.int32, (chunk, chunk), 1)
    causal = cc <= rr
    row_sub = jnp.right_shift(lax.broadcasted_iota(jnp.int32, (chunk, 1), 0), sub.bit_length() - 1)

    for ci in range(tc // chunk):
        r0 = ci * chunk
        for h in range(H_B):
            kc = slice(h * DK_B, (h + 1) * DK_B)
            vc = slice(h * DV_B, (h + 1) * DV_B)
            b = b_all[r0:r0 + chunk, kc]
            q = q_ref[r0:r0 + chunk, kc]
            k = k_ref[r0:r0 + chunk, kc]
            v = v_ref[r0:r0 + chunk, vc]
            b_last = b[chunk - 1:chunk, :]
            st = st_ref[h]
            o = _dot_nt((q * jnp.exp(b)).astype(BF16), st.astype(BF16))
            refs = [b[i * sub:i * sub + 1, :] for i in range(nsub)]
            ref_rows = jnp.concatenate([jnp.broadcast_to(rf, (sub, DK_B)) for rf in refs], axis=0)
            qt = q * jnp.exp(b - ref_rows)
            qx, kx = [], []
            for i in range(nsub):
                hi_row = (i + 1) * sub
                qx.append(jnp.where(row_sub == i, qt, 0.0))
                kt = k[:hi_row] * jnp.exp(refs[i] - b[:hi_row])
                if hi_row < chunk:
                    kt = jnp.concatenate([kt, jnp.zeros((chunk - hi_row, DK_B), F32)], axis=0)
                kx.append(kt)
            qx = jnp.concatenate(qx, axis=1).astype(BF16)
            kx = jnp.concatenate(kx, axis=1).astype(BF16)
            a = jnp.where(causal, _dot_nt(qx, kx), 0.0)
            o = o + _dot(a.astype(BF16), v)
            kd = (k * jnp.exp(b_last - b)).astype(BF16)
            st_ref[h] = st * jnp.exp(b_last) + _dot_tn(v, kd)
            ms = jnp.mean(o * o, axis=-1, keepdims=True)
            on = o * lax.rsqrt(ms + LN_EPS) * g_ref[:, vc]
            rbv = rb_ref[r0:r0 + chunk, vc]
            hb_ref[r0:r0 + chunk, vc] = (on * (rbv * _sigmoid(rbv))).astype(hb_ref.dtype)

    @pl.when(t == pl.num_programs(1) - 1)
    def _():
        sfin_ref[...] = st_ref[...]


def _gla(qb, kb, vb, la, rb, g, s0_t, tc, chunk):
    B, T, _ = qb.shape
    sub = min(GLA_SUB, chunk)

    def row(n):
        return pl.BlockSpec((None, tc, n), lambda b, t: (b, t, 0))

    st_spec = pl.BlockSpec((None, H_B, DV_B, DK_B), lambda b, t: (b, 0, 0, 0))
    return pl.pallas_call(
        functools.partial(_gla_kernel, chunk=chunk, sub=sub),
        grid=(B, T // tc),
        in_specs=[row(GLA_K), row(GLA_K), row(GLA_V), row(GLA_K), row(GLA_V),
                  _const_spec(g.shape), st_spec],
        out_specs=[row(GLA_V), st_spec],
        out_shape=[jax.ShapeDtypeStruct((B, T, GLA_V), BF16),
                   jax.ShapeDtypeStruct((B, H_B, DV_B, DK_B), F32)],
        scratch_shapes=[pltpu.VMEM((H_B, DV_B, DK_B), F32)],
        compiler_params=_params("parallel", "arbitrary"),
        name="gla",
    )(qb, kb, vb, la, rb, g, s0_t)


def _merge_kernel(oa_ref, hb_ref, ga_ref, gb_ref, x_ref, wa_ref, wb_ref, wo_ref, g_ref, b_ref, x1_ref):
    ya = _dot(oa_ref[...], wa_ref[...])
    yb = _dot(hb_ref[...], wb_ref[...])
    merged = _sigmoid(ga_ref[...].astype(F32)) * ya + _sigmoid(gb_ref[...].astype(F32)) * yb
    z = ALPHA * x_ref[...] + _dot(merged.astype(BF16), wo_ref[...])
    x1_ref[...] = _layer_norm(z, g_ref[...], b_ref[...])


def _merge(oa, hb, ga, gb, x, wa, wb, wo, g, b, tm):
    B, T, D = x.shape

    def row(n):
        return pl.BlockSpec((None, tm, n), lambda bb, t: (bb, t, 0))

    return pl.pallas_call(
        _merge_kernel,
        grid=(B, T // tm),
        in_specs=[row(FOX_W), row(GLA_V), row(D), row(D), row(D),
                  _const_spec(wa.shape), _const_spec(wb.shape), _const_spec(wo.shape),
                  _const_spec(g.shape), _const_spec(b.shape)],
        out_specs=row(D),
        out_shape=jax.ShapeDtypeStruct((B, T, D), F32),
        compiler_params=_params("parallel", "parallel"),
        name="merge",
    )(oa, hb, ga, gb, x, wa, wb, wo, g, b)


def _gelu_tanh(x):
    return 0.5 * x * (1.0 + jnp.tanh(math.sqrt(2.0 / math.pi) * (x + 0.044715 * (x * x * x))))


def _ffn_kernel(x1_ref, p_ref, cprev_ref, wup_ref, cw_ref, cb_ref, wdn_ref, g2_ref, b2_ref,
                wpl_ref, wplg_ref, g3_ref, b3_ref, y_ref, cnew_ref, carry_ref):
    @pl.when(pl.program_id(1) == 0)
    def _():
        carry_ref[...] = cprev_ref[...]

    x1 = x1_ref[...]
    tm = x1.shape[0]
    up = _dot(x1.astype(BF16), wup_ref[...])
    prev = carry_ref[...]
    row = lax.broadcasted_iota(jnp.int32, (tm, 1), 0)
    up_m1 = jnp.where(row == 0, prev[1:2, :], pltpu.roll(up, 1, 0))
    up_m2 = jnp.where(row == 0, prev[0:1, :], jnp.where(row == 1, prev[1:2, :], pltpu.roll(up, 2, 0)))
    conv = cb_ref[...] + cw_ref[0:1, :] * up_m2 + cw_ref[1:2, :] * up_m1 + cw_ref[2:3, :] * up
    last2 = up[tm - 2:tm, :]
    carry_ref[...] = last2
    cnew_ref[...] = last2
    hgl = (conv[:, :D_FF] * _gelu_tanh(conv[:, D_FF:])).astype(BF16)
    z2 = ALPHA * x1 + _dot(hgl, wdn_ref[...])
    x2 = _layer_norm(z2, g2_ref[...], b2_ref[...])
    e = _dot(p_ref[...].astype(BF16), wpl_ref[...]) * _sigmoid(_dot(x2.astype(BF16), wplg_ref[...]))
    y_ref[...] = _layer_norm(ALPHA * x2 + e, g3_ref[...], b3_ref[...])


def _ffn(x1, p, cprev, wup, cw, cb, wdn, g2, b2, wpl, wplg, g3, b3, tm):
    B, T, D = x1.shape

    def row(n):
        return pl.BlockSpec((None, tm, n), lambda bb, t: (bb, t, 0))

    cspec = pl.BlockSpec((None, CONV_W - 1, 2 * D_FF), lambda bb, t: (bb, 0, 0))
    consts = (wup, cw, cb, wdn, g2, b2, wpl, wplg, g3, b3)
    return pl.pallas_call(
        _ffn_kernel,
        grid=(B, T // tm),
        in_specs=[row(D), row(PLE_DIM), cspec] + [_const_spec(c.shape) for c in consts],
        out_specs=[row(D), cspec],
        out_shape=[jax.ShapeDtypeStruct((B, T, D), F32),
                   jax.ShapeDtypeStruct((B, CONV_W - 1, 2 * D_FF), F32)],
        scratch_shapes=[pltpu.VMEM((CONV_W - 1, 2 * D_FF), F32)],
        compiler_params=_params("parallel", "arbitrary"),
        name="ffn",
    )(x1, p, cprev, *consts)


def _pick(T, want):
    t = min(T, want)
    while T % t:
        t //= 2
    return t


def _pack_w_in(w_in):
    pts, acc = {}, 0
    for name, n in (("qa", FOX_W), ("ka", FOX_W), ("va", FOX_W), ("fa", H_A), ("qb", GLA_K), ("kb", GLA_K),
                    ("vb", GLA_V), ("rb", GLA_V), ("a1", GLA_RANK), ("ga", D_MODEL), ("gb", D_MODEL)):
        pts[name] = (acc, n)
        acc += n
    cols = []
    for name in _SEG:
        lo, n = pts[name]
        piece = w_in[:, lo:lo + n]
        if n < _SEG[name][1]:
            piece = jnp.pad(piece, ((0, 0), (0, _SEG[name][1] - n)))
        cols.append(piece)
    return jnp.concatenate(cols, axis=1).astype(BF16)


def _layer(x, p, fox_past, s0, conv_prev, wts):
    (w_in, b_fgate, w_a2, b_a2, g_gla, w_a_out, w_b_out, w_o, ln1_g, ln1_b, w_up, conv_w, conv_b,
     w_down, ln2_g, ln2_b, w_pl, w_plg, ln3_g, ln3_b) = wts
    B, T, D = x.shape
    row2 = lambda a: a.reshape(1, -1).astype(F32)

    wcat = _pack_w_in(w_in)
    bf = jnp.pad(row2(b_fgate), ((0, 0), (0, LANES - H_A)))
    wa2 = jnp.pad(w_a2.astype(F32), ((0, LANES - GLA_RANK), (0, 0)))
    wa2h = wa2.astype(BF16)
    wa2l = (wa2 - wa2h.astype(F32)).astype(BF16)

    tm = _pick(T, 256)
    (qa, ka_f, ka_b, va_f, va_b, logf, logf_pad, qb, kb, vb, rb, la, ga, gb) = _proj(
        x, wcat, bf, wa2h, wa2l, row2(b_a2), tm)

    if fox_past is None:
        tq = _pick(T, 512)
        kbias = _fox_bias(logf_pad, _pick(T, 512))
        vt = va_b.reshape(B, T // tq, tq, H_A // 2, LANES).transpose(0, 3, 1, 4, 2)
        oa = _fox_prompt(qa, ka_b, kbias, vt, tq)
    else:
        past_k, past_v, past_logf = fox_past
        P = past_k.shape[1]
        lk = -(-(P + T) // LANES) * LANES
        padk = lambda a: jnp.pad(a, ((0, 0), (0, lk - P - T), (0, 0)))
        k_all = padk(jnp.concatenate([past_k.reshape(B, P, FOX_W).astype(BF16), ka_b], axis=1))
        v_all = padk(jnp.concatenate([past_v.reshape(B, P, FOX_W).astype(BF16), va_b], axis=1))
        lf_all = padk(jnp.concatenate([past_logf.astype(F32), logf], axis=1))
        nc = _neg_cumsum(jnp.transpose(lf_all, (0, 2, 1)), _pick(lk, 512))
        oa = _fox_sample(qa, k_all, v_all, nc.reshape(B, H_A // 2, 2, lk), P)

    chunk = min(GLA_CHUNK, T)
    tc = _pick(T, 4 * chunk)
    hb, s_t = _gla(qb, kb, vb, la, rb, row2(g_gla), jnp.swapaxes(s0.astype(F32), 2, 3), tc, chunk)
    s_fin = jnp.swapaxes(s_t, 2, 3)

    x1 = _merge(oa, hb, ga, gb, x, w_a_out.astype(BF16), w_b_out.astype(BF16), w_o.astype(BF16),
                row2(ln1_g), row2(ln1_b), tm)
    y, conv_new = _ffn(x1, p, conv_prev.astype(F32), w_up.astype(BF16), conv_w.astype(F32), row2(conv_b),
                       w_down.astype(BF16), row2(ln2_g), row2(ln2_b), w_pl.astype(BF16),
                       w_plg.astype(BF16), row2(ln3_g), row2(ln3_b), tm)
    k_out = ka_f.reshape(B, T, H_A, DH_A)
    v_out = va_f.reshape(B, T, H_A, DH_A)
    return y, k_out, v_out, logf, s_fin, conv_new


def kernel(x_prompt, x_sample, cache_fox_k, cache_fox_v, cache_fox_logf, state_gla, cache_ffn_conv, p_prompt, p_sample, w_in, b_fgate, w_a2, b_a2, g_gla, w_a_out, w_b_out, w_o, ln1_g, ln1_b, w_up, conv_w, conv_b, w_down, ln2_g, ln2_b, w_pl, w_plg, ln3_g, ln3_b):
    hp, hs = x_prompt, x_sample
    outs_p = [[] for _ in range(5)]
    outs_s = [[] for _ in range(5)]
    for i in range(DEPTH):
        wts = (w_in[i], b_fgate[i], w_a2[i], b_a2[i], g_gla[i], w_a_out[i], w_b_out[i], w_o[i],
               ln1_g[i], ln1_b[i], w_up[i], conv_w[i], conv_b[i], w_down[i], ln2_g[i], ln2_b[i],
               w_pl[i], w_plg[i], ln3_g[i], ln3_b[i])
        Bp = hp.shape[0]
        s0_p = jnp.zeros((Bp, H_B, DK_B, DV_B), F32)
        c0_p = jnp.zeros((Bp, CONV_W - 1, 2 * D_FF), F32)
        hp, *rest_p = _layer(hp, p_prompt[i], None, s0_p, c0_p, wts)
        hs, *rest_s = _layer(hs, p_sample[i], (cache_fox_k[i], cache_fox_v[i], cache_fox_logf[i]),
                             state_gla[i], cache_ffn_conv[i], wts)
        for dst, src in ((outs_p, rest_p), (outs_s, rest_s)):
            for lst, val in zip(dst, src):
                lst.append(val)
    return (hp, hs, *[jnp.stack(l) for l in outs_p], *[jnp.stack(l) for l in outs_s])
```

```python
import functools
import math

import jax
import jax.numpy as jnp
from jax import lax
from jax.experimental import pallas as pl
from jax.experimental.pallas import tpu as pltpu

F32 = jnp.float32
BF16 = jnp.bfloat16

D_MODEL = 1024
H_A, DH_A = 8, 64
FOX_W = H_A * DH_A
H_B, DK_B, DV_B = 4, 128, 256
GLA_K = H_B * DK_B
GLA_V = H_B * DV_B
GLA_RANK = 16
GLA_TAU = 16
D_FF = 2816
CONV_W = 3
PLE_DIM = 256
LN_EPS = 1e-5
DEPTH = 1
ALPHA = (2 * DEPTH) ** 0.25
GLA_CHUNK = 64
GLA_SUB = 16

LANES = 128
LOG2E = math.log2(math.e)
V7X_VMEM_LIMIT_BYTES = 56 * 1024 * 1024

_SEG = {}
_off = 0
for _name, _n in (("qa", FOX_W), ("ka", FOX_W), ("va", FOX_W), ("qb", GLA_K), ("kb", GLA_K),
                  ("vb", GLA_V), ("rb", GLA_V), ("ga", D_MODEL), ("gb", D_MODEL),
                  ("fa", LANES), ("a1", LANES)):
    _SEG[_name] = (_off, _n)
    _off += _n
W_CAT = _off


def _params(*sem):
    return pltpu.CompilerParams(dimension_semantics=sem, vmem_limit_bytes=V7X_VMEM_LIMIT_BYTES)


def _const_spec(shape):
    nd = len(shape)
    return pl.BlockSpec(shape, lambda *_: (0,) * nd, pipeline_mode=pl.Buffered(1))


def _log_sigmoid(z):
    return jnp.minimum(z, 0.0) - jnp.log(1.0 + jnp.exp(-jnp.abs(z)))


def _sigmoid(z):
    return 1.0 / (1.0 + jnp.exp(-z))


def _split3(a):
    hi = a.astype(BF16)
    r = a - hi.astype(F32)
    mid = r.astype(BF16)
    lo = (r - mid.astype(F32)).astype(BF16)
    return hi, mid, lo


def _layer_norm(z, g, b):
    mu = jnp.mean(z, axis=-1, keepdims=True)
    zc = z - mu
    var = jnp.mean(zc * zc, axis=-1, keepdims=True)
    return zc * lax.rsqrt(var + LN_EPS) * g + b


def _dot(a, b):
    return jnp.dot(a, b, preferred_element_type=F32)


def _dot_nt(a, b):
    return lax.dot_general(a, b, (((1,), (1,)), ((), ())), preferred_element_type=F32)


def _dot_tn(a, b):
    return lax.dot_general(a, b, (((0,), (0,)), ((), ())), preferred_element_type=F32)


def _proj_kernel(x_ref, w_ref, bf_ref, wa2h_ref, wa2l_ref, ba2_ref,
                 qa_ref, kaf_ref, kab_ref, vaf_ref, vab_ref, logf_ref, logfp_ref,
                 qb_ref, kb_ref, vb_ref, rb_ref, la_ref, ga_ref, gb_ref):
    xb = x_ref[...].astype(BF16)

    def seg(name):
        lo, n = _SEG[name]
        return _dot(xb, w_ref[:, lo:lo + n])

    qa_ref[...] = (seg("qa") * (DH_A ** -0.5 * LOG2E)).astype(BF16)
    ka = seg("ka")
    kaf_ref[...] = ka
    kab_ref[...] = ka.astype(BF16)
    va = seg("va")
    vaf_ref[...] = va
    vab_ref[...] = va.astype(BF16)
    qb_ref[...] = seg("qb") * (DK_B ** -0.5)
    kb_ref[...] = seg("kb")
    vb_ref[...] = seg("vb").astype(BF16)
    rb_ref[...] = seg("rb")
    ga_ref[...] = seg("ga").astype(BF16)
    gb_ref[...] = seg("gb").astype(BF16)
    logf = _log_sigmoid(seg("fa") + bf_ref[...])
    logf_ref[...] = logf[:, :H_A]
    logfp_ref[...] = logf
    a1 = seg("a1")
    a1h = a1.astype(BF16)
    a1l = (a1 - a1h.astype(F32)).astype(BF16)
    z = _dot(a1h, wa2h_ref[...]) + _dot(a1l, wa2h_ref[...]) + _dot(a1h, wa2l_ref[...]) + ba2_ref[...]
    la_ref[...] = _log_sigmoid(z) * (1.0 / GLA_TAU)


def _proj(x, wcat, bf, wa2h, wa2l, ba2, tm):
    B, T, D = x.shape
    grid = (B, T // tm)

    def row(n, dt):
        return pl.BlockSpec((None, tm, n), lambda b, t: (b, t, 0)), jax.ShapeDtypeStruct((B, T, n), dt)

    outs = [row(FOX_W, BF16), row(FOX_W, F32), row(FOX_W, BF16), row(FOX_W, F32), row(FOX_W, BF16),
            row(H_A, F32), row(LANES, F32), row(GLA_K, F32), row(GLA_K, F32), row(GLA_V, BF16), row(GLA_V, F32),
            row(GLA_K, F32), row(D_MODEL, BF16), row(D_MODEL, BF16)]
    return pl.pallas_call(
        _proj_kernel,
        grid=grid,
        in_specs=[pl.BlockSpec((None, tm, D), lambda b, t: (b, t, 0)),
                  _const_spec(wcat.shape), _const_spec(bf.shape), _const_spec(wa2h.shape),
                  _const_spec(wa2l.shape), _const_spec(ba2.shape)],
        out_specs=[o[0] for o in outs],
        out_shape=[o[1] for o in outs],
        compiler_params=_params("parallel", "parallel"),
        name="proj",
    )(x, wcat, bf, wa2h, wa2l, ba2)


def _cumsum_kernel(lf_ref, nc_ref, carry_ref):
    @pl.when(pl.program_id(1) == 0)
    def _():
        carry_ref[...] = jnp.zeros_like(carry_ref)

    x = lf_ref[...]
    tl = x.shape[1]
    r = lax.broadcasted_iota(jnp.int32, (tl, tl), 0)
    c = lax.broadcasted_iota(jnp.int32, (tl, tl), 1)
    tri = jnp.where(r <= c, 1.0, 0.0).astype(BF16)
    hi, mid, lo = _split3(x)
    cs = _dot(hi, tri) + _dot(mid, tri) + _dot(lo, tri) + carry_ref[...]
    nc_ref[...] = -cs
    carry_ref[...] = carry_ref[...] + jnp.sum(x, axis=1, keepdims=True)


def _neg_cumsum(lf_t, tl):
    B, H, L = lf_t.shape
    return pl.pallas_call(
        _cumsum_kernel,
        grid=(B, L // tl),
        in_specs=[pl.BlockSpec((None, H, tl), lambda b, t: (b, 0, t))],
        out_specs=pl.BlockSpec((None, H, tl), lambda b, t: (b, 0, t)),
        out_shape=jax.ShapeDtypeStruct((B, H, L), F32),
        scratch_shapes=[pltpu.VMEM((H, 1), F32)],
        compiler_params=_params("parallel", "arbitrary"),
        name="cumsum",
    )(lf_t)


FOX_KEY_PAD = 512
FOX_BIAS_PARTS = 3
FOX_SUM_ROWS = 16


def _fox_bias_sel():
    sel = [[[0.0] * FOX_W for _ in range(LANES)] for _ in range(FOX_BIAS_PARTS)]
    for h in range(H_A):
        for j in range(FOX_BIAS_PARTS):
            sel[j][h][(h // 2) * LANES + FOX_BIAS_PARTS * (h % 2) + j] = 1.0
    return jnp.asarray(sel, dtype=BF16)


def _fox_bias_kernel(lf_ref, sel_ref, kb_ref, carry_ref):
    @pl.when(pl.program_id(1) == 0)
    def _():
        carry_ref[...] = jnp.zeros_like(carry_ref)

    x = lf_ref[...]
    tl = x.shape[0]
    r = lax.broadcasted_iota(jnp.int32, (tl, tl), 0)
    c = lax.broadcasted_iota(jnp.int32, (tl, tl), 1)
    tri = jnp.where(c <= r, 1.0, 0.0).astype(BF16)
    hi, mid, lo = _split3(x)
    cs = _dot(tri, hi) + _dot(tri, mid) + _dot(tri, lo) + carry_ref[...]
    carry_ref[...] = cs[tl - 1:tl, :]
    parts = _split3(cs * (-LOG2E))
    kb_ref[...] = (_dot(parts[0], sel_ref[0]) + _dot(parts[1], sel_ref[1])
                   + _dot(parts[2], sel_ref[2])).astype(kb_ref.dtype)


def _fox_bias(logf_pad, tl):
    B, T, _ = logf_pad.shape
    sel = _fox_bias_sel()
    return pl.pallas_call(
        _fox_bias_kernel,
        grid=(B, T // tl),
        in_specs=[pl.BlockSpec((None, tl, LANES), lambda b, t: (b, t, 0)), _const_spec(sel.shape)],
        out_specs=pl.BlockSpec((None, tl, FOX_W), lambda b, t: (b, t, 0)),
        out_shape=jax.ShapeDtypeStruct((B, T, FOX_W), BF16),
        scratch_shapes=[pltpu.VMEM((1, LANES), F32)],
        compiler_params=_params("parallel", "arbitrary"),
        name="fox_bias",
    )(logf_pad, sel)


def _fox_prompt_kernel(q_ref, k_ref, kb_ref, vt_ref, o_ref, m_ref, acc_ref,
                       sa_ref, sb_ref, bma_ref, bmb_ref, *, tq):
    i = pl.program_id(2)
    lane = lax.broadcasted_iota(jnp.int32, (1, LANES), 1)
    q = q_ref[...]
    zero = jnp.zeros_like(q)
    qa = []
    for h in range(2):
        own = (lane < DH_A) if h == 0 else (lane >= DH_A)
        ones = jnp.logical_and(lane >= FOX_BIAS_PARTS * h, lane < FOX_BIAS_PARTS * (h + 1))
        ones = jnp.broadcast_to(jnp.where(ones, 1.0, 0.0).astype(q.dtype), q.shape)
        qa.append(jnp.concatenate([jnp.where(own, q, zero), ones], axis=1))
    m_ref[...] = jnp.full_like(m_ref, -jnp.inf)
    acc_ref[...] = jnp.zeros_like(acc_ref)
    sum_rows = jnp.ones((FOX_SUM_ROWS, tq), BF16)

    def scores(j, s_ref, bm_ref):
        start = pl.multiple_of(j * tq, tq)
        ka = jnp.concatenate([k_ref[pl.ds(start, tq), :], kb_ref[pl.ds(start, tq), :]], axis=1)
        for h in range(2):
            st = _dot_nt(ka, qa[h])
            s_ref[h] = st
            bm_ref[h] = jnp.max(st, axis=0, keepdims=True)

    def soft_pv(j, s_ref, bm_ref, masked):
        for h in range(2):
            st = s_ref[h]
            if masked:
                r = lax.broadcasted_iota(jnp.int32, (tq, tq), 0)
                c = lax.broadcasted_iota(jnp.int32, (tq, tq), 1)
                st = jnp.where(r <= c, st, -jnp.inf)
                bm = jnp.max(st, axis=0, keepdims=True)
            else:
                bm = bm_ref[h]
            m_old = m_ref[h]
            m_new = jnp.maximum(m_old, bm)
            p = jnp.exp2(st - m_new)
            a = jnp.exp2(m_old - m_new)
            vt = jnp.concatenate([vt_ref[j, h * DH_A:(h + 1) * DH_A, :], sum_rows], axis=0)
            acc_ref[h] = a * acc_ref[h] + _dot(vt, p.astype(BF16))
            m_ref[h] = m_new

    scores(0, sa_ref, bma_ref)

    def pair(jj, carry):
        j = 2 * jj
        scores(j + 1, sb_ref, bmb_ref)
        soft_pv(j, sa_ref, bma_ref, False)
        scores(j + 2, sa_ref, bma_ref)
        soft_pv(j + 1, sb_ref, bmb_ref, False)
        return carry

    lax.fori_loop(0, lax.shift_right_logical(i, 1), pair, 0)
    odd = jnp.bitwise_and(i, 1) == 1

    @pl.when(odd)
    def _():
        scores(i, sb_ref, bmb_ref)
        soft_pv(i - 1, sa_ref, bma_ref, False)
        soft_pv(i, sb_ref, bmb_ref, True)

    @pl.when(jnp.logical_not(odd))
    def _():
        soft_pv(i, sa_ref, bma_ref, True)

    ot = jnp.concatenate([acc_ref[h, :DH_A, :] / acc_ref[h, DH_A:DH_A + 1, :] for h in range(2)],
                         axis=0)
    o_ref[...] = ot.T.astype(o_ref.dtype)


def _fox_prompt(qa, ka, kbias, vt, tq):
    B, T, _ = qa.shape
    nq = T // tq
    return pl.pallas_call(
        functools.partial(_fox_prompt_kernel, tq=tq),
        grid=(B, H_A // 2, nq),
        in_specs=[pl.BlockSpec((None, tq, LANES), lambda b, p, i: (b, i, p)),
                  pl.BlockSpec((None, T, LANES), lambda b, p, i: (b, 0, p)),
                  pl.BlockSpec((None, T, LANES), lambda b, p, i: (b, 0, p)),
                  pl.BlockSpec((None, None, nq, LANES, tq), lambda b, p, i: (b, p, 0, 0, 0))],
        out_specs=pl.BlockSpec((None, tq, LANES), lambda b, p, i: (b, i, p)),
        out_shape=jax.ShapeDtypeStruct((B, T, FOX_W), BF16),
        scratch_shapes=[pltpu.VMEM((2, 1, tq), F32),
                        pltpu.VMEM((2, DH_A + FOX_SUM_ROWS, tq), F32),
                        pltpu.VMEM((2, tq, tq), F32), pltpu.VMEM((2, tq, tq), F32),
                        pltpu.VMEM((2, 1, tq), F32), pltpu.VMEM((2, 1, tq), F32)],
        compiler_params=_params("parallel", "parallel", "arbitrary"),
        name="fox_prompt",
    )(qa, ka, kbias, vt)


def _fox_sample_kernel(q_ref, k_ref, v_ref, nc_ref, o_ref, *, past):
    first = lax.broadcasted_iota(jnp.int32, (1, LANES), 1) < DH_A
    q = q_ref[...]
    zero = jnp.zeros_like(q)
    tq = q.shape[0]
    kb = k_ref[...]
    vb = v_ref[...]
    lk = kb.shape[0]
    r = lax.broadcasted_iota(jnp.int32, (tq, lk), 0)
    c = lax.broadcasted_iota(jnp.int32, (tq, lk), 1)
    visible = c <= r + past
    outs = []
    for h in range(2):
        qh = jnp.where(first, q, zero) if h == 0 else jnp.where(first, zero, q)
        s = _dot_nt(qh, kb) + nc_ref[h:h + 1, :] * LOG2E
        s = jnp.where(visible, s, -jnp.inf)
        m = jnp.max(s, axis=1, keepdims=True)
        p = jnp.exp2(s - m)
        l = jnp.sum(p, axis=1, keepdims=True)
        outs.append(_dot(p.astype(BF16), vb) / l)
    o_ref[...] = jnp.where(first, outs[0], outs[1]).astype(o_ref.dtype)


def _fox_sample(qa, k_all, v_all, nc, past):
    B, Tq, _ = qa.shape
    Lk = k_all.shape[1]
    return pl.pallas_call(
        functools.partial(_fox_sample_kernel, past=past),
        grid=(B, H_A // 2),
        in_specs=[pl.BlockSpec((None, Tq, LANES), lambda b, p: (b, 0, p)),
                  pl.BlockSpec((None, Lk, LANES), lambda b, p: (b, 0, p)),
                  pl.BlockSpec((None, Lk, LANES), lambda b, p: (b, 0, p)),
                  pl.BlockSpec((None, None, 2, Lk), lambda b, p: (b, p, 0, 0))],
        out_specs=pl.BlockSpec((None, Tq, LANES), lambda b, p: (b, 0, p)),
        out_shape=jax.ShapeDtypeStruct((B, Tq, FOX_W), BF16),
        compiler_params=_params("parallel", "parallel"),
        name="fox_sample",
    )(qa, k_all, v_all, nc)


def _gla_kernel(q_ref, k_ref, v_ref, la_ref, rb_ref, g_ref, s0_ref, hb_ref, sfin_ref, st_ref,
                *, chunk, sub):
    t = pl.program_id(1)

    @pl.when(t == 0)
    def _():
        st_ref[...] = s0_ref[...]

    tc = q_ref.shape[0]
    nsub = chunk // sub
    nchunk = tc // chunk
    r = lax.broadcasted_iota(jnp.int32, (tc, tc), 0)
    c = lax.broadcasted_iota(jnp.int32, (tc, tc), 1)
    log_chunk = chunk.bit_length() - 1
    log_sub = sub.bit_length() - 1
    same_chunk = jnp.right_shift(r, log_chunk) == jnp.right_shift(c, log_chunk)
    near = jnp.logical_and(c <= r, same_chunk)
    far = jnp.right_shift(c, log_chunk) < jnp.right_shift(r, log_chunk)
    tri = jnp.where(c <= r, 1.0, 0.0).astype(BF16)
    hi, mid, lo = _split3(la_ref[...])
    b_all = _dot(tri, hi) + _dot(tri, mid) + _dot(tri, lo)

    row = lax.broadcasted_iota(jnp.int32, (tc, 1), 0)
    row_sub = jnp.bitwise_and(jnp.right_shift(row, log_sub), nsub - 1)
    row_chunk = jnp.right_shift(row, log_chunk)

    def rows_of(refs, n):
        return jnp.concatenate([jnp.broadcast_to(rf, (n, DK_B)) for rf in refs], axis=0)

    def zeros(n):
        return jnp.zeros((n, DK_B), F32)

    for h in range(H_B):
        kc = slice(h * DK_B, (h + 1) * DK_B)
        vc = slice(h * DV_B, (h + 1) * DV_B)
        b = b_all[:, kc]
        q = q_ref[:, kc]
        k = k_ref[:, kc]
        v = v_ref[:, vc]
        b_last = b[tc - 1:tc, :]
        st = st_ref[h]
        sub_refs = [b[i * sub:i * sub + 1, :] for i in range(tc // sub)]
        chunk_refs = [b[ci * chunk:ci * chunk + 1, :] for ci in range(nchunk)]

        qt = q * jnp.exp(b - rows_of(sub_refs, sub))
        qx = [jnp.where(row_sub == i, qt, 0.0) for i in range(nsub)]
        kx = []
        for i in range(nsub):
            pieces = []
            for ci in range(nchunk):
                lo_row, n = ci * chunk, (i + 1) * sub
                pieces.append(k[lo_row:lo_row + n] * jnp.exp(sub_refs[ci * nsub + i] - b[lo_row:lo_row + n]))
                if n < chunk:
                    pieces.append(zeros(chunk - n))
            kx.append(jnp.concatenate(pieces, axis=0) if len(pieces) > 1 else pieces[0])
        a = jnp.where(near, _dot_nt(jnp.concatenate(qx, axis=1).astype(BF16),
                                    jnp.concatenate(kx, axis=1).astype(BF16)), 0.0)
        if nchunk > 1:
            qc = q * jnp.exp(b - rows_of(chunk_refs, chunk))
            qx = [jnp.where(row_chunk == ci, qc, 0.0) for ci in range(1, nchunk)]
            kx = []
            for ci in range(1, nchunk):
                n = ci * chunk
                kx.append(jnp.concatenate([k[:n] * jnp.exp(chunk_refs[ci] - b[:n]), zeros(tc - n)], axis=0))
            a = jnp.where(far, _dot_nt(jnp.concatenate(qx, axis=1).astype(BF16),
                                       jnp.concatenate(kx, axis=1).astype(BF16)), a)
        o = _dot(a.astype(BF16), v) + _dot_nt((q * jnp.exp(b)).astype(BF16), st.astype(BF16))
        kd = (k * jnp.exp(b_last - b)).astype(BF16)
        st_ref[h] = st * jnp.exp(b_last) + _dot_tn(v, kd)
        ms = jnp.mean(o * o, axis=-1, keepdims=True)
        on = o * lax.rsqrt(ms + LN_EPS) * g_ref[:, vc]
        rbv = rb_ref[:, vc]
        hb_ref[:, vc] = (on * (rbv * _sigmoid(rbv))).astype(hb_ref.dtype)

    @pl.when(t == pl.num_programs(1) - 1)
    def _():
        sfin_ref[...] = st_ref[...]


def _gla(qb, kb, vb, la, rb, g, s0_t, tc, chunk):
    B, T, _ = qb.shape
    sub = min(GLA_SUB, chunk)

    def row(n):
        return pl.BlockSpec((None, tc, n), lambda b, t: (b, t, 0))

    st_spec = pl.BlockSpec((None, H_B, DV_B, DK_B), lambda b, t: (b, 0, 0, 0))
    return pl.pallas_call(
        functools.partial(_gla_kernel, chunk=chunk, sub=sub),
        grid=(B, T // tc),
        in_specs=[row(GLA_K), row(GLA_K), row(GLA_V), row(GLA_K), row(GLA_V),
                  _const_spec(g.shape), st_spec],
        out_specs=[row(GLA_V), st_spec],
        out_shape=[jax.ShapeDtypeStruct((B, T, GLA_V), BF16),
                   jax.ShapeDtypeStruct((B, H_B, DV_B, DK_B), F32)],
        scratch_shapes=[pltpu.VMEM((H_B, DV_B, DK_B), F32)],
        compiler_params=_params("parallel", "arbitrary"),
        name="gla",
    )(qb, kb, vb, la, rb, g, s0_t)


def _merge_kernel(oa_ref, hb_ref, ga_ref, gb_ref, x_ref, wa_ref, wb_ref, wo_ref, g_ref, b_ref, x1_ref):
    ya = _dot(oa_ref[...], wa_ref[...])
    yb = _dot(hb_ref[...], wb_ref[...])
    merged = _sigmoid(ga_ref[...].astype(F32)) * ya + _sigmoid(gb_ref[...].astype(F32)) * yb
    z = ALPHA * x_ref[...] + _dot(merged.astype(BF16), wo_ref[...])
    x1_ref[...] = _layer_norm(z, g_ref[...], b_ref[...])


def _merge(oa, hb, ga, gb, x, wa, wb, wo, g, b, tm):
    B, T, D = x.shape

    def row(n):
        return pl.BlockSpec((None, tm, n), lambda bb, t: (bb, t, 0))

    return pl.pallas_call(
        _merge_kernel,
        grid=(B, T // tm),
        in_specs=[row(FOX_W), row(GLA_V), row(D), row(D), row(D),
                  _const_spec(wa.shape), _const_spec(wb.shape), _const_spec(wo.shape),
                  _const_spec(g.shape), _const_spec(b.shape)],
        out_specs=row(D),
        out_shape=jax.ShapeDtypeStruct((B, T, D), F32),
        compiler_params=_params("parallel", "parallel"),
        name="merge",
    )(oa, hb, ga, gb, x, wa, wb, wo, g, b)


def _gelu_tanh(x):
    return 0.5 * x * (1.0 + jnp.tanh(math.sqrt(2.0 / math.pi) * (x + 0.044715 * (x * x * x))))


def _ffn_kernel(x1_ref, p_ref, cprev_ref, wup_ref, cw_ref, cb_ref, wdn_ref, g2_ref, b2_ref,
                wpl_ref, wplg_ref, g3_ref, b3_ref, y_ref, cnew_ref, carry_ref):
    @pl.when(pl.program_id(1) == 0)
    def _():
        carry_ref[...] = cprev_ref[...]

    x1 = x1_ref[...]
    tm = x1.shape[0]
    up = _dot(x1.astype(BF16), wup_ref[...])
    prev = carry_ref[...]
    row = lax.broadcasted_iota(jnp.int32, (tm, 1), 0)
    up_m1 = jnp.where(row == 0, prev[1:2, :], pltpu.roll(up, 1, 0))
    up_m2 = jnp.where(row == 0, prev[0:1, :], jnp.where(row == 1, prev[1:2, :], pltpu.roll(up, 2, 0)))
    conv =cb_ref[...] + cw_ref[0:1, :] * up_m2 + cw_ref[1:2, :] * up_m1 + cw_ref[2:3, :] * up
    last2 = up[tm - 2:tm, :]
    carry_ref[...] = last2
    cnew_ref[...] = last2
    hgl = (conv[:, :D_FF] * _gelu_tanh(conv[:, D_FF:])).astype(BF16)
    z2 = ALPHA * x1 + _dot(hgl, wdn_ref[...])
    x2 = _layer_norm(z2, g2_ref[...], b2_ref[...])
    e = _dot(p_ref[...].astype(BF16), wpl_ref[...]) * _sigmoid(_dot(x2.astype(BF16), wplg_ref[...]))
    y_ref[...] = _layer_norm(ALPHA * x2 + e, g3_ref[...], b3_ref[...])


def _ffn(x1, p, cprev, wup, cw, cb, wdn, g2, b2, wpl, wplg, g3, b3, tm):
    B, T, D = x1.shape

    def row(n):
        return pl.BlockSpec((None, tm, n), lambda bb, t: (bb, t, 0))

    cspec = pl.BlockSpec((None, CONV_W - 1, 2 * D_FF), lambda bb, t: (bb, 0, 0))
    consts = (wup, cw, cb, wdn, g2, b2, wpl, wplg, g3, b3)
    return pl.pallas_call(
        _ffn_kernel,
        grid=(B, T // tm),
        in_specs=[row(D), row(PLE_DIM), cspec] + [_const_spec(c.shape) for c in consts],
        out_specs=[row(D), cspec],
        out_shape=[jax.ShapeDtypeStruct((B, T, D), F32),
                   jax.ShapeDtypeStruct((B, CONV_W - 1, 2 * D_FF), F32)],
        scratch_shapes=[pltpu.VMEM((CONV_W - 1, 2 * D_FF), F32)],
        compiler_params=_params("parallel", "arbitrary"),
        name="ffn",
    )(x1, p, cprev, *consts)


def _pick(T, want):
    t = min(T, want)
    while T % t:
        t //= 2
    return t


def _pack_w_in(w_in):
    pts, acc = {}, 0
    for name, n in (("qa", FOX_W), ("ka", FOX_W), ("va", FOX_W), ("fa", H_A), ("qb", GLA_K), ("kb", GLA_K),
                    ("vb", GLA_V), ("rb", GLA_V), ("a1", GLA_RANK), ("ga", D_MODEL), ("gb", D_MODEL)):
        pts[name] = (acc, n)
        acc += n
    cols = []
    for name in _SEG:
        lo, n = pts[name]
        piece = w_in[:, lo:lo + n]
        if n < _SEG[name][1]:
            piece = jnp.pad(piece, ((0, 0), (0, _SEG[name][1] - n)))
        cols.append(piece)
    return jnp.concatenate(cols, axis=1).astype(BF16)


def _layer(x, p, fox_past, s0, conv_prev, wts):
    (w_in, b_fgate, w_a2, b_a2, g_gla, w_a_out, w_b_out, w_o, ln1_g, ln1_b, w_up, conv_w, conv_b,
     w_down, ln2_g, ln2_b, w_pl, w_plg, ln3_g, ln3_b) = wts
    B, T, D = x.shape
    row2 = lambda a: a.reshape(1, -1).astype(F32)

    wcat = _pack_w_in(w_in)
    bf = jnp.pad(row2(b_fgate), ((0, 0), (0, LANES - H_A)))
    wa2 = jnp.pad(w_a2.astype(F32), ((0, LANES - GLA_RANK), (0, 0)))
    wa2h = wa2.astype(BF16)
    wa2l = (wa2 - wa2h.astype(F32)).astype(BF16)

    tm = _pick(T, 256)
    (qa, ka_f, ka_b, va_f, va_b, logf, logf_pad, qb, kb, vb, rb, la, ga, gb) = _proj(
        x, wcat, bf, wa2h, wa2l, row2(b_a2), tm)

    if fox_past is None:
        tq = _pick(T, 512)
        kbias = _fox_bias(logf_pad, _pick(T, 512))
        vt = va_b.reshape(B, T // tq, tq, H_A // 2, LANES).transpose(0, 3, 1, 4, 2)
        oa = _fox_prompt(qa, ka_b, kbias, vt, tq)
    else:
        past_k, past_v, past_logf = fox_past
        P = past_k.shape[1]
        lk = -(-(P + T) // FOX_KEY_PAD) * FOX_KEY_PAD
        padk = lambda a: jnp.pad(a, ((0, 0), (0, lk - P - T), (0, 0)))
        k_all = padk(jnp.concatenate([past_k.reshape(B, P, FOX_W).astype(BF16), ka_b], axis=1))
        v_all = padk(jnp.concatenate([past_v.reshape(B, P, FOX_W).astype(BF16), va_b], axis=1))
        lf_all = padk(jnp.concatenate([past_logf.astype(F32), logf], axis=1))
        nc = _neg_cumsum(jnp.transpose(lf_all, (0, 2, 1)), _pick(lk, 512))
        oa = _fox_sample(qa, k_all, v_all, nc.reshape(B, H_A // 2, 2, lk), P)

    chunk = min(GLA_CHUNK, T)
    tc = _pick(T, 4 * chunk)
    hb, s_t = _gla(qb, kb, vb, la, rb, row2(g_gla), jnp.swapaxes(s0.astype(F32), 2, 3), tc, chunk)
    s_fin = jnp.swapaxes(s_t, 2, 3)

    x1 = _merge(oa, hb, ga, gb, x, w_a_out.astype(BF16), w_b_out.astype(BF16), w_o.astype(BF16),
                row2(ln1_g), row2(ln1_b), _pick(T, 512))
    y, conv_new = _ffn(x1, p, conv_prev.astype(F32), w_up.astype(BF16), conv_w.astype(F32), row2(conv_b),
                       w_down.astype(BF16), row2(ln2_g), row2(ln2_b), w_pl.astype(BF16),
                       w_plg.astype(BF16), row2(ln3_g), row2(ln3_b), tm)
    k_out = ka_f.reshape(B, T, H_A, DH_A)
    v_out = va_f.reshape(B, T, H_A, DH_A)
    return y, k_out, v_out, logf, s_fin, conv_new


def kernel(x_prompt, x_sample, cache_fox_k, cache_fox_v, cache_fox_logf, state_gla, cache_ffn_conv, p_prompt, p_sample, w_in, b_fgate, w_a2, b_a2, g_gla, w_a_out, w_b_out, w_o, ln1_g, ln1_b, w_up, conv_w, conv_b, w_down, ln2_g, ln2_b, w_pl, w_plg, ln3_g, ln3_b):
    hp, hs = x_prompt, x_sample
    outs_p = [[] for _ in range(5)]
    outs_s = [[] for _ in range(5)]
    for i in range(DEPTH):
        wts = (w_in[i], b_fgate[i], w_a2[i], b_a2[i], g_gla[i], w_a_out[i], w_b_out[i], w_o[i],
               ln1_g[i], ln1_b[i], w_up[i], conv_w[i], conv_b[i], w_down[i], ln2_g[i], ln2_b[i],
               w_pl[i], w_plg[i], ln3_g[i], ln3_b[i])
        Bp = hp.shape[0]
        s0_p = jnp.zeros((Bp, H_B, DK_B, DV_B), F32)
        c0_p = jnp.zeros((Bp, CONV_W - 1, 2 * D_FF), F32)
        hp, *rest_p = _layer(hp, p_prompt[i], None, s0_p, c0_p, wts)
        hs, *rest_s = _layer(hs, p_sample[i], (cache_fox_k[i], cache_fox_v[i], cache_fox_logf[i]),
                             state_gla[i], cache_ffn_conv[i], wts)
        for dst, src in ((outs_p, rest_p), (outs_s, rest_s)):
            for lst, val in zip(dst, src):
                lst.append(val)
    return (hp, hs, *[jnp.stack(l) for l in outs_p], *[jnp.stack(l) for l in outs_s])
```

```python
import functools
import math

import jax
import jax.numpy as jnp
from jax import lax
from jax.experimental import pallas as pl
from jax.experimental.pallas import tpu as pltpu

F32 = jnp.float32
BF16 = jnp.bfloat16

D_MODEL = 1024
H_A, DH_A = 8, 64
FOX_W = H_A * DH_A
H_B, DK_B, DV_B = 4, 128, 256
GLA_K = H_B * DK_B
GLA_V = H_B * DV_B
GLA_RANK = 16
GLA_TAU = 16
D_FF = 2816
CONV_W = 3
PLE_DIM = 256
LN_EPS = 1e-5
DEPTH = 1
ALPHA = (2 * DEPTH) ** 0.25
GLA_CHUNK = 64
GLA_SUB = 16

LANES = 128
LOG2E = math.log2(math.e)
V7X_VMEM_LIMIT_BYTES = 56 * 1024 * 1024

_SEG = {}
_off = 0
for _name, _n in (("qa", FOX_W), ("ka", FOX_W), ("va", FOX_W), ("qb", GLA_K), ("kb", GLA_K),
                  ("vb", GLA_V), ("rb", GLA_V), ("ga", D_MODEL), ("gb", D_MODEL),
                  ("fa", LANES), ("a1", LANES)):
    _SEG[_name] = (_off, _n)
    _off += _n
W_CAT = _off


def _params(*sem):
    return pltpu.CompilerParams(dimension_semantics=sem, vmem_limit_bytes=V7X_VMEM_LIMIT_BYTES)


def _const_spec(shape):
    nd = len(shape)
    return pl.BlockSpec(shape, lambda *_: (0,) * nd, pipeline_mode=pl.Buffered(1))


def _log_sigmoid(z):
    return jnp.minimum(z, 0.0) - jnp.log(1.0 + jnp.exp(-jnp.abs(z)))


def _sigmoid(z):
    return 1.0 / (1.0 + jnp.exp(-z))


def _split3(a):
    hi = a.astype(BF16)
    r = a - hi.astype(F32)
    mid = r.astype(BF16)
    lo = (r - mid.astype(F32)).astype(BF16)
    return hi, mid, lo


def _layer_norm(z, g, b):
    mu = jnp.mean(z, axis=-1, keepdims=True)
    zc = z - mu
    var = jnp.mean(zc * zc, axis=-1, keepdims=True)
    return zc * lax.rsqrt(var + LN_EPS) * g + b


def _dot(a, b):
    return jnp.dot(a, b, preferred_element_type=F32)


def _dot_nt(a, b):
    return lax.dot_general(a, b, (((1,), (1,)), ((), ())), preferred_element_type=F32)


def _dot_tn(a, b):
    return lax.dot_general(a, b, (((0,), (0,)), ((), ())), preferred_element_type=F32)


def _proj_kernel(x_ref, w_ref, bf_ref, wa2h_ref, wa2l_ref, ba2_ref,
                 qa_ref, kaf_ref, kab_ref, vaf_ref, vab_ref, logf_ref, logfp_ref,
                 qb_ref, kb_ref, vb_ref, rb_ref, la_ref, ga_ref, gb_ref, *, values_transposed):
    xb = x_ref[...].astype(BF16)

    def seg(name):
        lo, n = _SEG[name]
        return _dot(xb, w_ref[:, lo:lo + n])

    qa_ref[...] = (seg("qa") * (DH_A ** -0.5 * LOG2E)).astype(BF16)
    ka = seg("ka")
    kaf_ref[...] = ka
    kab_ref[...] = ka.astype(BF16)
    va = seg("va")
    vaf_ref[...] = va
    if values_transposed:
        for p in range(H_A // 2):
            vab_ref[p] = va[:, p * LANES:(p + 1) * LANES].T.astype(BF16)
    else:
        vab_ref[...] = va.astype(BF16)
    qb_ref[...] = seg("qb") * (DK_B ** -0.5)
    kb_ref[...] = seg("kb")
    vb_ref[...] = seg("vb").astype(BF16)
    rb_ref[...] = seg("rb")
    ga_ref[...] = seg("ga").astype(BF16)
    gb_ref[...] = seg("gb").astype(BF16)
    logf = _log_sigmoid(seg("fa") + bf_ref[...])
    logf_ref[...] = logf[:, :H_A]
    logfp_ref[...] = logf
    a1 = seg("a1")
    a1h = a1.astype(BF16)
    a1l = (a1 - a1h.astype(F32)).astype(BF16)
    z = _dot(a1h, wa2h_ref[...]) + _dot(a1l, wa2h_ref[...]) + _dot(a1h, wa2l_ref[...]) + ba2_ref[...]
    la_ref[...] = _log_sigmoid(z) * (1.0 / GLA_TAU)


def _proj(x, wcat, bf, wa2h, wa2l, ba2, tm, vt_block=None):
    B, T, D = x.shape
    grid = (B, T // tm)

    def row(n, dt):
        return pl.BlockSpec((None, tm, n), lambda b, t: (b, t, 0)), jax.ShapeDtypeStruct((B, T, n), dt)

    outs = [row(FOX_W, BF16), row(FOX_W, F32), row(FOX_W, BF16), row(FOX_W, F32), row(FOX_W, BF16),
            row(H_A, F32), row(LANES, F32), row(GLA_K, F32), row(GLA_K, F32), row(GLA_V, BF16), row(GLA_V, F32),
            row(GLA_K, F32), row(D_MODEL, BF16), row(D_MODEL, BF16)]
    if vt_block is not None:
        per = vt_block // tm
        outs[4] = (pl.BlockSpec((None, H_A // 2, None, LANES, tm), lambda b, t: (b, 0, t // per, 0, t % per)),
                   jax.ShapeDtypeStruct((B, H_A // 2, T // vt_block, LANES, vt_block), BF16))
    return pl.pallas_call(
        functools.partial(_proj_kernel, values_transposed=vt_block is not None),
        grid=grid,
        in_specs=[pl.BlockSpec((None, tm, D), lambda b, t: (b, t, 0)),
                  _const_spec(wcat.shape), _const_spec(bf.shape), _const_spec(wa2h.shape),
                  _const_spec(wa2l.shape), _const_spec(ba2.shape)],
        out_specs=[o[0] for o in outs],
        out_shape=[o[1] for o in outs],
        compiler_params=_params("parallel", "parallel"),
        name="proj",
    )(x, wcat, bf, wa2h, wa2l, ba2)


def _cumsum_kernel(lf_ref, nc_ref, carry_ref):
    @pl.when(pl.program_id(1) == 0)
    def _():
        carry_ref[...] = jnp.zeros_like(carry_ref)

    x = lf_ref[...]
    tl = x.shape[1]
    r = lax.broadcasted_iota(jnp.int32, (tl, tl), 0)
    c = lax.broadcasted_iota(jnp.int32, (tl, tl), 1)
    tri = jnp.where(r <= c, 1.0, 0.0).astype(BF16)
    hi, mid, lo = _split3(x)
    cs = _dot(hi, tri) + _dot(mid, tri) + _dot(lo, tri) + carry_ref[...]
    nc_ref[...] = -cs
    carry_ref[...] = carry_ref[...] + jnp.sum(x, axis=1, keepdims=True)


def _neg_cumsum(lf_t, tl):
    B, H, L = lf_t.shape
    return pl.pallas_call(
        _cumsum_kernel,
        grid=(B, L // tl),
        in_specs=[pl.BlockSpec((None, H, tl), lambda b, t: (b, 0, t))],
        out_specs=pl.BlockSpec((None, H, tl), lambda b, t: (b, 0, t)),
        out_shape=jax.ShapeDtypeStruct((B, H, L), F32),
        scratch_shapes=[pltpu.VMEM((H, 1), F32)],
        compiler_params=_params("parallel", "arbitrary"),
        name="cumsum",
    )(lf_t)


FOX_KEY_PAD = 512
FOX_BIAS_PARTS = 3
FOX_SUM_ROWS = 16


def _fox_bias_sel():
    sel = [[[0.0] * FOX_W for _ in range(LANES)] for _ in range(FOX_BIAS_PARTS)]
    for h in range(H_A):
        for j in range(FOX_BIAS_PARTS):
            sel[j][h][(h // 2) * LANES + FOX_BIAS_PARTS * (h % 2) + j] = 1.0
    return jnp.asarray(sel, dtype=BF16)


def _fox_bias_kernel(lf_ref, sel_ref, kb_ref, carry_ref):
    @pl.when(pl.program_id(1) == 0)
    def _():
        carry_ref[...] = jnp.zeros_like(carry_ref)

    x = lf_ref[...]
    tl = x.shape[0]
    r = lax.broadcasted_iota(jnp.int32, (tl, tl), 0)
    c = lax.broadcasted_iota(jnp.int32, (tl, tl), 1)
    tri = jnp.where(c <= r, 1.0, 0.0).astype(BF16)
    hi, mid, lo = _split3(x)
    cs = _dot(tri, hi) + _dot(tri, mid) + _dot(tri, lo) + carry_ref[...]
    carry_ref[...] = cs[tl - 1:tl, :]
    parts = _split3(cs * (-LOG2E))
    kb_ref[...] = (_dot(parts[0], sel_ref[0]) + _dot(parts[1], sel_ref[1])
                   + _dot(parts[2], sel_ref[2])).astype(kb_ref.dtype)


def _fox_bias(logf_pad, tl):
    B, T, _ = logf_pad.shape
    sel = _fox_bias_sel()
    return pl.pallas_call(
        _fox_bias_kernel,
        grid=(B, T // tl),
        in_specs=[pl.BlockSpec((None, tl, LANES), lambda b, t: (b, t, 0)), _const_spec(sel.shape)],
        out_specs=pl.BlockSpec((None, tl, FOX_W), lambda b, t: (b, t, 0)),
        out_shape=jax.ShapeDtypeStruct((B, T, FOX_W), BF16),
        scratch_shapes=[pltpu.VMEM((1, LANES), F32)],
        compiler_params=_params("parallel", "arbitrary"),
        name="fox_bias",
    )(logf_pad, sel)


def _fox_prompt_kernel(q_ref, k_ref, kb_ref, vt_ref, o_ref, m_ref, acc_ref,
                       sa_ref, sb_ref, bma_ref, bmb_ref, *, tq):
    i = pl.program_id(2)
    lane = lax.broadcasted_iota(jnp.int32, (1, LANES), 1)
    q = q_ref[...]
    zero = jnp.zeros_like(q)
    qa = []
    for h in range(2):
        own = (lane < DH_A) if h == 0 else (lane >= DH_A)
        ones = jnp.logical_and(lane >= FOX_BIAS_PARTS * h, lane < FOX_BIAS_PARTS * (h + 1))
        ones = jnp.broadcast_to(jnp.where(ones, 1.0, 0.0).astype(q.dtype), q.shape)
        qa.append(jnp.concatenate([jnp.where(own, q, zero), ones], axis=1))
    m_ref[...] = jnp.full_like(m_ref, -jnp.inf)
    acc_ref[...] = jnp.zeros_like(acc_ref)
    sum_rows = jnp.ones((FOX_SUM_ROWS, tq), BF16)

    def scores(j, s_ref, bm_ref):
        start = pl.multiple_of(j * tq, tq)
        ka = jnp.concatenate([k_ref[pl.ds(start, tq), :], kb_ref[pl.ds(start, tq), :]], axis=1)
        for h in range(2):
            st = _dot_nt(ka, qa[h])
            s_ref[h] = st
            bm_ref[h] = jnp.max(st, axis=0, keepdims=True)

    def soft_pv(j, s_ref, bm_ref, masked):
        for h in range(2):
            st = s_ref[h]
            if masked:
                r = lax.broadcasted_iota(jnp.int32, (tq, tq), 0)
                c = lax.broadcasted_iota(jnp.int32, (tq, tq), 1)
                st = jnp.where(r <= c, st, -jnp.inf)
                bm = jnp.max(st, axis=0, keepdims=True)
            else:
                bm = bm_ref[h]
            m_old = m_ref[h]
            m_new = jnp.maximum(m_old, bm)
            p = jnp.exp2(st - m_new)
            a = jnp.exp2(m_old - m_new)
            vt = jnp.concatenate([vt_ref[j, h * DH_A:(h + 1) * DH_A, :], sum_rows], axis=0)
            acc_ref[h] = a * acc_ref[h] + _dot(vt, p.astype(BF16))
            m_ref[h] = m_new

    scores(i, sa_ref, bma_ref)

    @pl.when(i == 0)
    def _():
        soft_pv(0, sa_ref, bma_ref, True)

    @pl.when(i > 0)
    def _():
        scores(i - 1, sb_ref, bmb_ref)
        soft_pv(i, sa_ref, bma_ref, True)

        def pair(t, carry):
            j = i - 1 - 2 * t
            scores(j - 1, sa_ref, bma_ref)
            soft_pv(j, sb_ref, bmb_ref, False)
            scores(j - 2, sb_ref, bmb_ref)
            soft_pv(j - 1, sa_ref, bma_ref, False)
            return carry

        lax.fori_loop(0, lax.shift_right_logical(i - 1, 1), pair, 0)
        odd = jnp.bitwise_and(i, 1) == 1

        @pl.when(odd)
        def _():
            soft_pv(0, sb_ref, bmb_ref, False)

        @pl.when(jnp.logical_not(odd))
        def _():
            scores(0, sa_ref, bma_ref)
            soft_pv(1, sb_ref, bmb_ref, False)
            soft_pv(0, sa_ref, bma_ref, False)

    ot = jnp.concatenate([acc_ref[h, :DH_A, :] / acc_ref[h, DH_A:DH_A + 1, :] for h in range(2)],
                         axis=0)
    o_ref[...] = ot.T.astype(o_ref.dtype)


def _fox_prompt(qa, ka, kbias, vt, tq):
    B, T, _ = qa.shape
    nq = T // tq
    return pl.pallas_call(
        functools.partial(_fox_prompt_kernel, tq=tq),
        grid=(B, H_A // 2, nq),
        in_specs=[pl.BlockSpec((None, tq, LANES), lambda b, p, i: (b, i, p)),
                  pl.BlockSpec((None, T, LANES), lambda b, p, i: (b, 0, p)),
                  pl.BlockSpec((None, T, LANES), lambda b, p, i: (b, 0, p)),
                  pl.BlockSpec((None, None, nq, LANES, tq), lambda b, p, i: (b, p, 0, 0, 0))],
        out_specs=pl.BlockSpec((None, tq, LANES), lambda b, p, i: (b, i, p)),
        out_shape=jax.ShapeDtypeStruct((B, T, FOX_W), BF16),
        scratch_shapes=[pltpu.VMEM((2, 1, tq), F32),
                        pltpu.VMEM((2, DH_A + FOX_SUM_ROWS, tq), F32),
                        pltpu.VMEM((2, tq, tq), F32), pltpu.VMEM((2, tq, tq), F32),
                        pltpu.VMEM((2, 1, tq), F32), pltpu.VMEM((2, 1, tq), F32)],
        compiler_params=_params("parallel", "parallel", "arbitrary"),
        name="fox_prompt",
    )(qa, ka, kbias, vt)


def _fox_sample_kernel(q_ref, k_ref, v_ref, nc_ref, o_ref, *, past):
    first = lax.broadcasted_iota(jnp.int32, (1, LANES), 1) < DH_A
    q = q_ref[...]
    zero = jnp.zeros_like(q)
    tq = q.shape[0]
    kb = k_ref[...]
    vb = v_ref[...]
    lk = kb.shape[0]
    r = lax.broadcasted_iota(jnp.int32, (tq, lk), 0)
    c = lax.broadcasted_iota(jnp.int32, (tq, lk), 1)
    visible = c <= r + past
    outs = []
    for h in range(2):
        qh = jnp.where(first, q, zero) if h == 0 else jnp.where(first, zero, q)
        s = _dot_nt(qh, kb) + nc_ref[h:h + 1, :] * LOG2E
        s = jnp.where(visible, s, -jnp.inf)
        m = jnp.max(s, axis=1, keepdims=True)
        p = jnp.exp2(s - m)
        l = jnp.sum(p, axis=1, keepdims=True)
        outs.append(_dot(p.astype(BF16), vb) / l)
    o_ref[...] = jnp.where(first, outs[0], outs[1]).astype(o_ref.dtype)


def _fox_sample(qa, k_all, v_all, nc, past):
    B, Tq, _ = qa.shape
    Lk = k_all.shape[1]
    return pl.pallas_call(
        functools.partial(_fox_sample_kernel, past=past),
        grid=(B, H_A // 2),
        in_specs=[pl.BlockSpec((None, Tq, LANES), lambda b, p: (b, 0, p)),
                  pl.BlockSpec((None, Lk, LANES), lambda b, p: (b, 0, p)),
                  pl.BlockSpec((None, Lk, LANES), lambda b, p: (b, 0, p)),
                  pl.BlockSpec((None, None, 2, Lk), lambda b, p: (b, p, 0, 0))],
        out_specs=pl.BlockSpec((None, Tq, LANES), lambda b, p: (b, 0, p)),
        out_shape=jax.ShapeDtypeStruct((B, Tq, FOX_W), BF16),
        compiler_params=_params("parallel", "parallel"),
        name="fox_sample",
    )(qa, k_all, v_all, nc)


def _gla_kernel(q_ref, k_ref, v_ref, la_ref, rb_ref, g_ref, s0_ref, hb_ref, sfin_ref, st_ref,
                *, chunk, sub):
    t = pl.program_id(1)

    @pl.when(t == 0)
    def _():
        st_ref[...] = s0_ref[...]

    tc = q_ref.shape[0]
    nsub = chunk // sub
    nchunk = tc // chunk
    r = lax.broadcasted_iota(jnp.int32, (tc, tc), 0)
    c = lax.broadcasted_iota(jnp.int32, (tc, tc), 1)
    log_chunk = chunk.bit_length() - 1
    log_sub = sub.bit_length() - 1
    same_chunk = jnp.right_shift(r, log_chunk) == jnp.right_shift(c, log_chunk)
    near = jnp.logical_and(c <= r, same_chunk)
    far = jnp.right_shift(c, log_chunk) < jnp.right_shift(r, log_chunk)
    tri = jnp.where(c <= r, 1.0, 0.0).astype(BF16)
    hi, mid, lo = _split3(la_ref[...])
    b_all = (_dot(tri, hi) + _dot(tri, mid) + _dot(tri, lo)) * LOG2E

    row = lax.broadcasted_iota(jnp.int32, (tc, 1), 0)
    row_sub = jnp.bitwise_and(jnp.right_shift(row, log_sub), nsub - 1)
    row_chunk = jnp.right_shift(row, log_chunk)

    def rows_of(refs, n):
        return jnp.concatenate([jnp.broadcast_to(rf, (n, DK_B)) for rf in refs], axis=0)

    def zeros(n):
        return jnp.zeros((n, DK_B), F32)

    for h in range(H_B):
        kc = slice(h * DK_B, (h + 1) * DK_B)
        vc = slice(h * DV_B, (h + 1) * DV_B)
        b = b_all[:, kc]
        q = q_ref[:, kc]
        k = k_ref[:, kc]
        v = v_ref[:, vc]
        b_last = b[tc - 1:tc, :]
        st = st_ref[h]
        sub_refs = [b[i * sub:i * sub + 1, :] for i in range(tc // sub)]
        chunk_refs = [b[ci * chunk:ci * chunk + 1, :] for ci in range(nchunk)]

        qt = q * jnp.exp2(b - rows_of(sub_refs, sub))
        qx = [jnp.where(row_sub == i, qt, 0.0) for i in range(nsub)]
        kx = []
        for i in range(nsub):
            pieces = []
            for ci in range(nchunk):
                lo_row, n = ci * chunk, (i + 1) * sub
                pieces.append(k[lo_row:lo_row + n] * jnp.exp2(sub_refs[ci * nsub + i] - b[lo_row:lo_row + n]))
                if n < chunk:
                    pieces.append(zeros(chunk - n))
            kx.append(jnp.concatenate(pieces, axis=0) if len(pieces) > 1 else pieces[0])
        a = jnp.where(near, _dot_nt(jnp.concatenate(qx, axis=1).astype(BF16),
                                    jnp.concatenate(kx, axis=1).astype(BF16)), 0.0)
        if nchunk > 1:
            qc = q * jnp.exp2(b - rows_of(chunk_refs, chunk))
            qx = [jnp.where(row_chunk == ci, qc, 0.0) for ci in range(1, nchunk)]
            kx = []
            for ci in range(1, nchunk):
                n = ci * chunk
                kx.append(jnp.concatenate([k[:n] * jnp.exp2(chunk_refs[ci] - b[:n]), zeros(tc - n)], axis=0))
            a = jnp.where(far, _dot_nt(jnp.concatenate(qx, axis=1).astype(BF16),
                                       jnp.concatenate(kx, axis=1).astype(BF16)), a)
        o = _dot(a.astype(BF16), v) + _dot_nt((q * jnp.exp2(b)).astype(BF16), st.astype(BF16))
        kd = (k * jnp.exp2(b_last - b)).astype(BF16)
        st_ref[h] = st * jnp.exp2(b_last) + _dot_tn(v, kd)
        ms = jnp.mean(o * o, axis=-1, keepdims=True)
        on = o * lax.rsqrt(ms + LN_EPS) * g_ref[:, vc]
        rbv = rb_ref[:, vc]
        hb_ref[:, vc] = (on * (rbv * _sigmoid(rbv))).astype(hb_ref.dtype)

    @pl.when(t == pl.num_programs(1) - 1)
    def _():
        sfin_ref[...] = st_ref[...]


def _gla(qb, kb, vb, la, rb, g, s0_t, tc, chunk):
    B, T, _ = qb.shape
    sub = min(GLA_SUB, chunk)

    def row(n):
        return pl.BlockSpec((None, tc, n), lambda b, t: (b, t, 0))

    st_spec = pl.BlockSpec((None, H_B, DV_B, DK_B), lambda b, t: (b, 0, 0, 0))
    return pl.pallas_call(
        functools.partial(_gla_kernel, chunk=chunk, sub=sub),
        grid=(B, T // tc),
        in_specs=[row(GLA_K), row(GLA_K), row(GLA_V), row(GLA_K), row(GLA_V),
                  _const_spec(g.shape), st_spec],
        out_specs=[row(GLA_V), st_spec],
        out_shape=[jax.ShapeDtypeStruct((B, T, GLA_V), BF16),
                   jax.ShapeDtypeStruct((B, H_B, DV_B, DK_B), F32)],
        scratch_shapes=[pltpu.VMEM((H_B, DV_B, DK_B), F32)],
        compiler_params=_params("parallel", "arbitrary"),
        name="gla",
    )(qb, kb, vb, la, rb, g, s0_t)


def _merge_kernel(oa_ref, hb_ref, ga_ref, gb_ref, x_ref, wa_ref, wb_ref, wo_ref, g_ref, b_ref, x1_ref):
    ya = _dot(oa_ref[...], wa_ref[...])
    yb = _dot(hb_ref[...], wb_ref[...])
    merged = _sigmoid(ga_ref[...].astype(F32)) * ya + _sigmoid(gb_ref[...].astype(F32)) * yb
    z = ALPHA * x_ref[...] + _dot(merged.astype(BF16), wo_ref[...])
    x1_ref[...] = _layer_norm(z, g_ref[...], b_ref[...])


def _merge(oa, hb, ga, gb, x, wa, wb, wo, g, b, tm):
    B, T, D = x.shape

    def row(n):
        return pl.BlockSpec((None, tm, n), lambda bb, t: (bb, t, 0))

    return pl.pallas_call(
        _merge_kernel,
        grid=(B, T // tm),
        in_specs=[row(FOX_W), row(GLA_V), row(D), row(D), row(D),
                  _const_spec(wa.shape), _const_spec(wb.shape), _const_spec(wo.shape),
                  _const_spec(g.shape), _const_spec(b.shape)],
        out_specs=row(D),
        out_shape=jax.ShapeDtypeStruct((B, T, D), F32),
        compiler_params=_params("parallel", "parallel"),
        name="merge",
    )(oa, hb, ga, gb, x, wa, wb, wo, g, b)


def _gelu_tanh(x):
    return 0.5 * x * (1.0 + jnp.tanh(math.sqrt(2.0 / math.pi) * (x + 0.044715 * (x * x * x))))


def _ffn_kernel(x1_ref, p_ref, cprev_ref, wup_ref, cw_ref, cb_ref, wdn_ref, g2_ref, b2_ref,
                wpl_ref, wplg_ref, g3_ref, b3_ref, y_ref, cnew_ref, carry_ref):
    @pl.when(pl.program_id(1) == 0)
    def _():
        carry_ref[...] = cprev_ref[...]

    x1 = x1_ref[...]
    tm = x1.shape[0]
    up = _dot(x1.astype(BF16), wup_ref[...])
    prev = carry_ref[...]
    row = lax.broadcasted_iota(jnp.int32, (tm, 1), 0)
    up_m1 = jnp.where(row == 0, prev[1:2, :], pltpu.roll(up, 1, 0))
    up_m2 = jnp.where(row == 0, prev[0:1, :], jnp.where(row == 1, prev[1:2, :], pltpu.roll(up, 2, 0)))
    conv =cb_ref[...] + cw_ref[0:1, :] * up_m2 + cw_ref[1:2, :] * up_m1 + cw_ref[2:3, :] * up
    last2 = up[tm - 2:tm, :]
    carry_ref[...] = last2
    cnew_ref[...] = last2
    hgl = (conv[:, :D_FF] * _gelu_tanh(conv[:, D_FF:])).astype(BF16)
    z2 = ALPHA * x1 + _dot(hgl, wdn_ref[...])
    x2 = _layer_norm(z2, g2_ref[...], b2_ref[...])
    e = _dot(p_ref[...].astype(BF16), wpl_ref[...]) * _sigmoid(_dot(x2.astype(BF16), wplg_ref[...]))
    y_ref[...] = _layer_norm(ALPHA * x2 + e, g3_ref[...], b3_ref[...])


def _ffn(x1, p, cprev, wup, cw, cb, wdn, g2, b2, wpl, wplg, g3, b3, tm):
    B, T, D = x1.shape

    def row(n):
        return pl.BlockSpec((None, tm, n), lambda bb, t: (bb, t, 0))

    cspec = pl.BlockSpec((None, CONV_W - 1, 2 * D_FF), lambda bb, t: (bb, 0, 0))
    consts = (wup, cw, cb, wdn, g2, b2, wpl, wplg, g3, b3)
    return pl.pallas_call(
        _ffn_kernel,
        grid=(B, T // tm),
        in_specs=[row(D), row(PLE_DIM), cspec] + [_const_spec(c.shape) for c in consts],
        out_specs=[row(D), cspec],
        out_shape=[jax.ShapeDtypeStruct((B, T, D), F32),
                   jax.ShapeDtypeStruct((B, CONV_W - 1, 2 * D_FF), F32)],
        scratch_shapes=[pltpu.VMEM((CONV_W - 1, 2 * D_FF), F32)],
        compiler_params=_params("parallel", "arbitrary"),
        name="ffn",
    )(x1, p, cprev, *consts)


def _pick(T, want):
    t = min(T, want)
    while T % t:
        t //= 2
    return t


def _pack_w_in(w_in):
    pts, acc = {}, 0
    for name, n in (("qa", FOX_W), ("ka", FOX_W), ("va", FOX_W), ("fa", H_A), ("qb", GLA_K), ("kb", GLA_K),
                    ("vb", GLA_V), ("rb", GLA_V), ("a1", GLA_RANK), ("ga", D_MODEL), ("gb", D_MODEL)):
        pts[name] = (acc, n)
        acc += n
    cols = []
    for name in _SEG:
        lo, n = pts[name]
        piece = w_in[:, lo:lo + n]
        if n < _SEG[name][1]:
            piece = jnp.pad(piece, ((0, 0), (0, _SEG[name][1] - n)))
        cols.append(piece)
    return jnp.concatenate(cols, axis=1).astype(BF16)


def _layer(x, p, fox_past, s0, conv_prev, wts):
    (w_in, b_fgate, w_a2, b_a2, g_gla, w_a_out, w_b_out, w_o, ln1_g, ln1_b, w_up, conv_w, conv_b,
     w_down, ln2_g, ln2_b, w_pl, w_plg, ln3_g, ln3_b) = wts
    B, T, D = x.shape
    row2 = lambda a: a.reshape(1, -1).astype(F32)

    wcat = _pack_w_in(w_in)
    bf = jnp.pad(row2(b_fgate), ((0, 0), (0, LANES - H_A)))
    wa2 = jnp.pad(w_a2.astype(F32), ((0, LANES - GLA_RANK), (0, 0)))
    wa2h = wa2.astype(BF16)
    wa2l = (wa2 - wa2h.astype(F32)).astype(BF16)

    tm = _pick(T, 256)
    tq = _pick(T, 512)
    proj_out = _proj(x, wcat, bf, wa2h, wa2l, row2(b_a2), tm, vt_block=tq if fox_past is None else None)
    (qa, ka_f, ka_b, va_f, va_b, logf, logf_pad, qb, kb, vb, rb, la, ga, gb) = proj_out

    if fox_past is None:
        kbias = _fox_bias(logf_pad, _pick(T, 512))
        oa = _fox_prompt(qa, ka_b, kbias, va_b, tq)
    else:
        past_k, past_v, past_logf = fox_past
        P = past_k.shape[1]
        lk = -(-(P + T) // FOX_KEY_PAD) * FOX_KEY_PAD
        padk = lambda a: jnp.pad(a, ((0, 0), (0, lk - P - T), (0, 0)))
        k_all = padk(jnp.concatenate([past_k.reshape(B, P, FOX_W).astype(BF16), ka_b], axis=1))
        v_all = padk(jnp.concatenate([past_v.reshape(B, P, FOX_W).astype(BF16), va_b], axis=1))
        lf_all = padk(jnp.concatenate([past_logf.astype(F32), logf], axis=1))
        nc = _neg_cumsum(jnp.transpose(lf_all, (0, 2, 1)), _pick(lk, 512))
        oa = _fox_sample(qa, k_all, v_all, nc.reshape(B, H_A // 2, 2, lk), P)

    chunk = min(GLA_CHUNK, T)
    tc = _pick(T, 4 * chunk)
    hb, s_t = _gla(qb, kb, vb, la, rb, row2(g_gla), jnp.swapaxes(s0.astype(F32), 2, 3), tc, chunk)
    s_fin = jnp.swapaxes(s_t, 2, 3)

    x1 = _merge(oa, hb, ga, gb, x, w_a_out.astype(BF16), w_b_out.astype(BF16), w_o.astype(BF16),
                row2(ln1_g), row2(ln1_b), _pick(T, 512))
    y, conv_new = _ffn(x1, p, conv_prev.astype(F32), w_up.astype(BF16), conv_w.astype(F32), row2(conv_b),
                       w_down.astype(BF16), row2(ln2_g), row2(ln2_b), w_pl.astype(BF16),
                       w_plg.astype(BF16), row2(ln3_g), row2(ln3_b), tm)
    k_out = ka_f.reshape(B, T, H_A, DH_A)
    v_out = va_f.reshape(B, T, H_A, DH_A)
    return y, k_out, v_out, logf, s_fin, conv_new


def kernel(x_prompt, x_sample, cache_fox_k, cache_fox_v, cache_fox_logf, state_gla, cache_ffn_conv, p_prompt, p_sample, w_in, b_fgate, w_a2, b_a2, g_gla, w_a_out, w_b_out, w_o, ln1_g, ln1_b, w_up, conv_w, conv_b, w_down, ln2_g, ln2_b, w_pl, w_plg, ln3_g, ln3_b):
    hp, hs = x_prompt, x_sample
    outs_p = [[] for _ in range(5)]
    outs_s = [[] for _ in range(5)]
    for i in range(DEPTH):
        wts = (w_in[i], b_fgate[i], w_a2[i], b_a2[i], g_gla[i], w_a_out[i], w_b_out[i], w_o[i],
               ln1_g[i], ln1_b[i], w_up[i], conv_w[i], conv_b[i], w_down[i], ln2_g[i], ln2_b[i],
               w_pl[i], w_plg[i], ln3_g[i], ln3_b[i])
        Bp = hp.shape[0]
        s0_p = jnp.zeros((Bp, H_B, DK_B, DV_B), F32)
        c0_p = jnp.zeros((Bp, CONV_W - 1, 2 * D_FF), F32)
        hp, *rest_p = _layer(hp, p_prompt[i], None, s0_p, c0_p, wts)
        hs, *rest_s = _layer(hs, p_sample[i], (cache_fox_k[i], cache_fox_v[i], cache_fox_logf[i]),
                             state_gla[i], cache_ffn_conv[i], wts)
        for dst, src in ((outs_p, rest_p), (outs_s, rest_s)):
            for lst, val in zip(dst, src):
                lst.append(val)
    return (hp, hs, *[jnp.stack(l) for l in outs_p], *[jnp.stack(l) for l in outs_s])
```

```python
import functools
import math

import jax
import jax.numpy as jnp
from jax import lax
from jax.experimental import pallas as pl
from jax.experimental.pallas import tpu as pltpu

F32 = jnp.float32
BF16 = jnp.bfloat16

D_MODEL = 1024
H_A, DH_A = 8, 64
FOX_W = H_A * DH_A
H_B, DK_B, DV_B = 4, 128, 256
GLA_K = H_B * DK_B
GLA_V = H_B * DV_B
GLA_RANK = 16
GLA_TAU = 16
D_FF = 2816
CONV_W = 3
PLE_DIM = 256
LN_EPS = 1e-5
DEPTH = 1
ALPHA = (2 * DEPTH) ** 0.25
GLA_CHUNK = 64
GLA_SUB = 16

LANES = 128
LOG2E = math.log2(math.e)
V7X_VMEM_LIMIT_BYTES = 56 * 1024 * 1024

_SEG = {}
_off = 0
for _name, _n in (("qa", FOX_W), ("ka", FOX_W), ("va", FOX_W), ("qb", GLA_K), ("kb", GLA_K),
                  ("vb", GLA_V), ("rb", GLA_V), ("ga", D_MODEL), ("gb", D_MODEL),
                  ("fa", LANES), ("a1", LANES)):
    _SEG[_name] = (_off, _n)
    _off += _n
W_CAT = _off


def _params(*sem):
    return pltpu.CompilerParams(dimension_semantics=sem, vmem_limit_bytes=V7X_VMEM_LIMIT_BYTES)


def _const_spec(shape):
    nd = len(shape)
    return pl.BlockSpec(shape, lambda *_: (0,) * nd, pipeline_mode=pl.Buffered(1))


def _log_sigmoid(z):
    return jnp.minimum(z, 0.0) - jnp.log(1.0 + jnp.exp(-jnp.abs(z)))


def _sigmoid(z):
    return 1.0 / (1.0 + jnp.exp(-z))


def _split3(a):
    hi = a.astype(BF16)
    r = a - hi.astype(F32)
    mid = r.astype(BF16)
    lo = (r - mid.astype(F32)).astype(BF16)
    return hi, mid, lo


def _layer_norm(z, g, b):
    mu = jnp.mean(z, axis=-1, keepdims=True)
    zc = z - mu
    var = jnp.mean(zc * zc, axis=-1, keepdims=True)
    return zc * lax.rsqrt(var + LN_EPS) * g + b


def _dot(a, b):
    return jnp.dot(a, b, preferred_element_type=F32)


def _dot_nt(a, b):
    return lax.dot_general(a, b, (((1,), (1,)), ((), ())), preferred_element_type=F32)


def _dot_tn(a, b):
    return lax.dot_general(a, b, (((0,), (0,)), ((), ())), preferred_element_type=F32)


def _proj_kernel(x_ref, w_ref, bf_ref, wa2h_ref, wa2l_ref, ba2_ref,
                 qa_ref, kaf_ref, kab_ref, vaf_ref, vab_ref, logf_ref, logfp_ref,
                 qb_ref, kb_ref, vb_ref, rb_ref, la_ref, ga_ref, gb_ref, *, values_transposed):
    xb = x_ref[...].astype(BF16)

    def seg(name):
        lo, n = _SEG[name]
        return _dot(xb, w_ref[:, lo:lo + n])

    qa_ref[...] = (seg("qa") * (DH_A ** -0.5 * LOG2E)).astype(BF16)
    ka = seg("ka")
    kaf_ref[...] = ka
    kab_ref[...] = ka.astype(BF16)
    va = seg("va")
    vaf_ref[...] = va
    if values_transposed:
        for p in range(H_A // 2):
            vab_ref[p] = va[:, p * LANES:(p + 1) * LANES].T.astype(BF16)
    else:
        vab_ref[...] = va.astype(BF16)
    qb_ref[...] = seg("qb") * (DK_B ** -0.5)
    kb_ref[...] = seg("kb")
    vb_ref[...] = seg("vb").astype(BF16)
    rb_ref[...] = seg("rb")
    ga_ref[...] = seg("ga").astype(BF16)
    gb_ref[...] = seg("gb").astype(BF16)
    logf = _log_sigmoid(seg("fa") + bf_ref[...])
    logf_ref[...] = logf[:, :H_A]
    logfp_ref[...] = logf
    a1 = seg("a1")
    a1h = a1.astype(BF16)
    a1l = (a1 - a1h.astype(F32)).astype(BF16)
    z = _dot(a1h, wa2h_ref[...]) + _dot(a1l, wa2h_ref[...]) + _dot(a1h, wa2l_ref[...]) + ba2_ref[...]
    la_ref[...] = _log_sigmoid(z) * (1.0 / GLA_TAU)


def _proj(x, wcat, bf, wa2h, wa2l, ba2, tm, vt_block=None):
    B, T, D = x.shape
    grid = (B, T // tm)

    def row(n, dt):
        return pl.BlockSpec((None, tm, n), lambda b, t: (b, t, 0)), jax.ShapeDtypeStruct((B, T, n), dt)

    outs = [row(FOX_W, BF16), row(FOX_W, F32), row(FOX_W, BF16), row(FOX_W, F32), row(FOX_W, BF16),
            row(H_A, F32), row(LANES, F32), row(GLA_K, F32), row(GLA_K, F32), row(GLA_V, BF16), row(GLA_V, F32),
            row(GLA_K, F32), row(D_MODEL, BF16), row(D_MODEL, BF16)]
    if vt_block is not None:
        per = vt_block // tm
        outs[4] = (pl.BlockSpec((None, H_A // 2, None, LANES, tm), lambda b, t: (b, 0, t // per, 0, t % per)),
                   jax.ShapeDtypeStruct((B, H_A // 2, T // vt_block, LANES, vt_block), BF16))
    return pl.pallas_call(
        functools.partial(_proj_kernel, values_transposed=vt_block is not None),
        grid=grid,
        in_specs=[pl.BlockSpec((None, tm, D), lambda b, t: (b, t, 0)),
                  _const_spec(wcat.shape), _const_spec(bf.shape), _const_spec(wa2h.shape),
                  _const_spec(wa2l.shape), _const_spec(ba2.shape)],
        out_specs=[o[0] for o in outs],
        out_shape=[o[1] for o in outs],
        compiler_params=_params("parallel", "parallel"),
        name="proj",
    )(x, wcat, bf, wa2h, wa2l, ba2)


def _cumsum_kernel(lf_ref, nc_ref, carry_ref):
    @pl.when(pl.program_id(1) == 0)
    def _():
        carry_ref[...] = jnp.zeros_like(carry_ref)

    x = lf_ref[...]
    tl = x.shape[1]
    r = lax.broadcasted_iota(jnp.int32, (tl, tl), 0)
    c = lax.broadcasted_iota(jnp.int32, (tl, tl), 1)
    tri = jnp.where(r <= c, 1.0, 0.0).astype(BF16)
    hi, mid, lo = _split3(x)
    cs = _dot(hi, tri) + _dot(mid, tri) + _dot(lo, tri) + carry_ref[...]
    nc_ref[...] = -cs
    carry_ref[...] = carry_ref[...] + jnp.sum(x, axis=1, keepdims=True)


def _neg_cumsum(lf_t, tl):
    B, H, L = lf_t.shape
    return pl.pallas_call(
        _cumsum_kernel,
        grid=(B, L // tl),
        in_specs=[pl.BlockSpec((None, H, tl), lambda b, t: (b, 0, t))],
        out_specs=pl.BlockSpec((None, H, tl), lambda b, t: (b, 0, t)),
        out_shape=jax.ShapeDtypeStruct((B, H, L), F32),
        scratch_shapes=[pltpu.VMEM((H, 1), F32)],
        compiler_params=_params("parallel", "arbitrary"),
        name="cumsum",
    )(lf_t)


FOX_KEY_PAD = 512
FOX_BIAS_PARTS = 3
FOX_UNROLL = 4
FOX_SUM_ROWS = 16


def _fox_bias_sel():
    sel = [[[0.0] * FOX_W for _ in range(LANES)] for _ in range(FOX_BIAS_PARTS)]
    for h in range(H_A):
        for j in range(FOX_BIAS_PARTS):
            sel[j][h][(h // 2) * LANES + FOX_BIAS_PARTS * (h % 2) + j] = 1.0
    return jnp.asarray(sel, dtype=BF16)


def _fox_bias_kernel(lf_ref, sel_ref, kb_ref, carry_ref):
    @pl.when(pl.program_id(1) == 0)
    def _():
        carry_ref[...] = jnp.zeros_like(carry_ref)

    x = lf_ref[...]
    tl = x.shape[0]
    r = lax.broadcasted_iota(jnp.int32, (tl, tl), 0)
    c = lax.broadcasted_iota(jnp.int32, (tl, tl), 1)
    tri = jnp.where(c <= r, 1.0, 0.0).astype(BF16)
    hi, mid, lo = _split3(x)
    cs = _dot(tri, hi) + _dot(tri, mid) + _dot(tri, lo) + carry_ref[...]
    carry_ref[...] = cs[tl - 1:tl, :]
    parts = _split3(cs * (-LOG2E))
    kb_ref[...] = (_dot(parts[0], sel_ref[0]) + _dot(parts[1], sel_ref[1])
                   + _dot(parts[2], sel_ref[2])).astype(kb_ref.dtype)


def _fox_bias(logf_pad, tl):
    B, T, _ = logf_pad.shape
    sel = _fox_bias_sel()
    return pl.pallas_call(
        _fox_bias_kernel,
        grid=(B, T // tl),
        in_specs=[pl.BlockSpec((None, tl, LANES), lambda b, t: (b, t, 0)), _const_spec(sel.shape)],
        out_specs=pl.BlockSpec((None, tl, FOX_W), lambda b, t: (b, t, 0)),
        out_shape=jax.ShapeDtypeStruct((B, T, FOX_W), BF16),
        scratch_shapes=[pltpu.VMEM((1, LANES), F32)],
        compiler_params=_params("parallel", "arbitrary"),
        name="fox_bias",
    )(logf_pad, sel)


def _fox_prompt_kernel(q_ref, k_ref, kb_ref, vt_ref, o_ref, m_ref, acc_ref,
                       sa_ref, sb_ref, bma_ref, bmb_ref, *, tq):
    i = pl.program_id(2)
    lane = lax.broadcasted_iota(jnp.int32, (1, LANES), 1)
    q = q_ref[...]
    zero = jnp.zeros_like(q)
    qa = []
    for h in range(2):
        own = (lane < DH_A) if h == 0 else (lane >= DH_A)
        ones = jnp.logical_and(lane >= FOX_BIAS_PARTS * h, lane < FOX_BIAS_PARTS * (h + 1))
        ones = jnp.broadcast_to(jnp.where(ones, 1.0, 0.0).astype(q.dtype), q.shape)
        qa.append(jnp.concatenate([jnp.where(own, q, zero), ones], axis=1))
    m_ref[...] = jnp.full_like(m_ref, -jnp.inf)
    acc_ref[...] = jnp.zeros_like(acc_ref)
    sum_rows = jnp.ones((FOX_SUM_ROWS, tq), BF16)

    def scores(j, s_ref, bm_ref):
        start = pl.multiple_of(j * tq, tq)
        ka = jnp.concatenate([k_ref[pl.ds(start, tq), :], kb_ref[pl.ds(start, tq), :]], axis=1)
        for h in range(2):
            st = _dot_nt(ka, qa[h])
            s_ref[h] = st
            bm_ref[h] = jnp.max(st, axis=0, keepdims=True)

    def soft_pv(j, s_ref, bm_ref, masked):
        for h in range(2):
            st = s_ref[h]
            if masked:
                r = lax.broadcasted_iota(jnp.int32, (tq, tq), 0)
                c = lax.broadcasted_iota(jnp.int32, (tq, tq), 1)
                st = jnp.where(r <= c, st, -jnp.inf)
                bm = jnp.max(st, axis=0, keepdims=True)
            else:
                bm = bm_ref[h]
            m_old = m_ref[h]
            m_new = jnp.maximum(m_old, bm)
            p = jnp.exp2(st - m_new)
            a = jnp.exp2(m_old - m_new)
            vt = jnp.concatenate([vt_ref[j, h * DH_A:(h + 1) * DH_A, :], sum_rows], axis=0)
            acc_ref[h] = a * acc_ref[h] + _dot(vt, p.astype(BF16))
            m_ref[h] = m_new

    slots = ((sa_ref, bma_ref), (sb_ref, bmb_ref))

    def run(j0, count, last_masked, prefetch_after):
        for u in range(count):
            if u + 1 < count or prefetch_after:
                scores(j0 + u + 1, *slots[(u + 1) % 2])
            soft_pv(j0 + u, *slots[u % 2], last_masked and u + 1 == count)

    scores(0, *slots[0])

    def body(t, carry):
        run(FOX_UNROLL * t, FOX_UNROLL, False, True)
        return carry

    log_unroll = FOX_UNROLL.bit_length() - 1
    lax.fori_loop(0, lax.shift_right_logical(i, log_unroll), body, 0)
    rest = jnp.bitwise_and(i, FOX_UNROLL - 1)
    for n in range(FOX_UNROLL):
        @pl.when(rest == n)
        def _(n=n):
            run(i - n, n + 1, True, False)

    ot = jnp.concatenate([acc_ref[h, :DH_A, :] / acc_ref[h, DH_A:DH_A + 1, :] for h in range(2)],
                         axis=0)
    o_ref[...] = ot.T.astype(o_ref.dtype)


def _fox_prompt(qa, ka, kbias, vt, tq):
    B, T, _ = qa.shape
    nq = T // tq
    return pl.pallas_call(
        functools.partial(_fox_prompt_kernel, tq=tq),
        grid=(B, H_A // 2, nq),
        in_specs=[pl.BlockSpec((None, tq, LANES), lambda b, p, i: (b, i, p)),
                  pl.BlockSpec((None, T, LANES), lambda b, p, i: (b, 0, p)),
                  pl.BlockSpec((None, T, LANES), lambda b, p, i: (b, 0, p)),
                  pl.BlockSpec((None, None, nq, LANES, tq), lambda b, p, i: (b, p, 0, 0, 0))],
        out_specs=pl.BlockSpec((None, tq, LANES), lambda b, p, i: (b, i, p)),
        out_shape=jax.ShapeDtypeStruct((B, T, FOX_W), BF16),
        scratch_shapes=[pltpu.VMEM((2, 1, tq), F32),
                        pltpu.VMEM((2, DH_A + FOX_SUM_ROWS, tq), F32),
                        pltpu.VMEM((2, tq, tq), F32), pltpu.VMEM((2, tq, tq), F32),
                        pltpu.VMEM((2, 1, tq), F32), pltpu.VMEM((2, 1, tq), F32)],
        compiler_params=_params("parallel", "parallel", "arbitrary"),
        name="fox_prompt",
    )(qa, ka, kbias, vt)


def _fox_sample_kernel(q_ref, k_ref, v_ref, nc_ref, o_ref, *, past):
    first = lax.broadcasted_iota(jnp.int32, (1, LANES), 1) < DH_A
    q = q_ref[...]
    zero = jnp.zeros_like(q)
    tq = q.shape[0]
    kb = k_ref[...]
    vb = v_ref[...]
    lk = kb.shape[0]
    r = lax.broadcasted_iota(jnp.int32, (tq, lk), 0)
    c = lax.broadcasted_iota(jnp.int32, (tq, lk), 1)
    visible = c <= r + past
    outs = []
    for h in range(2):
        qh = jnp.where(first, q, zero) if h == 0 else jnp.where(first, zero, q)
        s = _dot_nt(qh, kb) + nc_ref[h:h + 1, :] * LOG2E
        s = jnp.where(visible, s, -jnp.inf)
        m = jnp.max(s, axis=1, keepdims=True)
        p = jnp.exp2(s - m)
        l = jnp.sum(p, axis=1, keepdims=True)
        outs.append(_dot(p.astype(BF16), vb) / l)
    o_ref[...] = jnp.where(first, outs[0], outs[1]).astype(o_ref.dtype)


def _fox_sample(qa, k_all, v_all, nc, past):
    B, Tq, _ = qa.shape
    Lk = k_all.shape[1]
    return pl.pallas_call(
        functools.partial(_fox_sample_kernel, past=past),
        grid=(B, H_A // 2),
        in_specs=[pl.BlockSpec((None, Tq, LANES), lambda b, p: (b, 0, p)),
                  pl.BlockSpec((None, Lk, LANES), lambda b, p: (b, 0, p)),
                  pl.BlockSpec((None, Lk, LANES), lambda b, p: (b, 0, p)),
                  pl.BlockSpec((None, None, 2, Lk), lambda b, p: (b, p, 0, 0))],
        out_specs=pl.BlockSpec((None, Tq, LANES), lambda b, p: (b, 0, p)),
        out_shape=jax.ShapeDtypeStruct((B, Tq, FOX_W), BF16),
        compiler_params=_params("parallel", "parallel"),
        name="fox_sample",
    )(qa, k_all, v_all, nc)


def _gla_kernel(q_ref, k_ref, v_ref, la_ref, rb_ref, g_ref, s0_ref, hb_ref, sfin_ref, st_ref,
                *, chunk, sub):
    t = pl.program_id(1)

    @pl.when(t == 0)
    def _():
        st_ref[...] = s0_ref[...]

    tc = q_ref.shape[0]
    nsub = chunk // sub
    nchunk = tc // chunk
    r = lax.broadcasted_iota(jnp.int32, (tc, tc), 0)
    c = lax.broadcasted_iota(jnp.int32, (tc, tc), 1)
    log_chunk = chunk.bit_length() - 1
    log_sub = sub.bit_length() - 1
    same_chunk = jnp.right_shift(r, log_chunk) == jnp.right_shift(c, log_chunk)
    near = jnp.logical_and(c <= r, same_chunk)
    far = jnp.right_shift(c, log_chunk) < jnp.right_shift(r, log_chunk)
    tri = jnp.where(c <= r, 1.0, 0.0).astype(BF16)
    hi, mid, lo = _split3(la_ref[...])
    b_all = (_dot(tri, hi) + _dot(tri, mid) + _dot(tri, lo)) * LOG2E

    row = lax.broadcasted_iota(jnp.int32, (tc, 1), 0)
    row_sub = jnp.bitwise_and(jnp.right_shift(row, log_sub), nsub - 1)
    row_chunk = jnp.right_shift(row, log_chunk)

    def rows_of(refs, n):
        return jnp.concatenate([jnp.broadcast_to(rf, (n, DK_B)) for rf in refs], axis=0)

    def zeros(n):
        return jnp.zeros((n, DK_B), F32)

    for h in range(H_B):
        kc = slice(h * DK_B, (h + 1) * DK_B)
        vc = slice(h * DV_B, (h + 1) * DV_B)
        b = b_all[:, kc]
        q = q_ref[:, kc]
        k = k_ref[:, kc]
        v = v_ref[:, vc]
        b_last = b[tc - 1:tc, :]
        st = st_ref[h]
        sub_refs = [b[i * sub:i * sub + 1, :] for i in range(tc // sub)]
        chunk_refs = [b[ci * chunk:ci * chunk + 1, :] for ci in range(nchunk)]

        qt = q * jnp.exp2(b - rows_of(sub_refs, sub))
        qx = [jnp.where(row_sub == i, qt, 0.0) for i in range(nsub)]
        kx = []
        for i in range(nsub):
            pieces = []
            for ci in range(nchunk):
                lo_row, n = ci * chunk, (i + 1) * sub
                pieces.append(k[lo_row:lo_row + n] * jnp.exp2(sub_refs[ci * nsub + i] - b[lo_row:lo_row + n]))
                if n < chunk:
                    pieces.append(zeros(chunk - n))
            kx.append(jnp.concatenate(pieces, axis=0) if len(pieces) > 1 else pieces[0])
        a = jnp.where(near, _dot_nt(jnp.concatenate(qx, axis=1).astype(BF16),
                                    jnp.concatenate(kx, axis=1).astype(BF16)), 0.0)
        if nchunk > 1:
            qc = q * jnp.exp2(b - rows_of(chunk_refs, chunk))
            qx = [jnp.where(row_chunk == ci, qc, 0.0) for ci in range(1, nchunk)]
            kx = []
            for ci in range(1, nchunk):
                n = ci * chunk
                kx.append(jnp.concatenate([k[:n] * jnp.exp2(chunk_refs[ci] - b[:n]), zeros(tc - n)], axis=0))
            a = jnp.where(far, _dot_nt(jnp.concatenate(qx, axis=1).astype(BF16),
                                       jnp.concatenate(kx, axis=1).astype(BF16)), a)
        o = _dot(a.astype(BF16), v) + _dot_nt((q * jnp.exp2(b)).astype(BF16), st.astype(BF16))
        kd = (k * jnp.exp2(b_last - b)).astype(BF16)
        st_ref[h] = st * jnp.exp2(b_last) + _dot_tn(v, kd)
        ms = jnp.mean(o * o, axis=-1, keepdims=True)
        on = o * lax.rsqrt(ms + LN_EPS) * g_ref[:, vc]
        rbv = rb_ref[:, vc]
        hb_ref[:, vc] = (on * (rbv * _sigmoid(rbv))).astype(hb_ref.dtype)

    @pl.when(t == pl.num_programs(1) - 1)
    def _():
        sfin_ref[...] = st_ref[...]


def _gla(qb, kb, vb, la, rb, g, s0_t, tc, chunk):
    B, T, _ = qb.shape
    sub = min(GLA_SUB, chunk)

    def row(n):
        return pl.BlockSpec((None, tc, n), lambda b, t: (b, t, 0))

    st_spec = pl.BlockSpec((None, H_B, DV_B, DK_B), lambda b, t: (b, 0, 0, 0))
    return pl.pallas_call(
        functools.partial(_gla_kernel, chunk=chunk, sub=sub),
        grid=(B, T // tc),
        in_specs=[row(GLA_K), row(GLA_K), row(GLA_V), row(GLA_K), row(GLA_V),
                  _const_spec(g.shape), st_spec],
        out_specs=[row(GLA_V), st_spec],
        out_shape=[jax.ShapeDtypeStruct((B, T, GLA_V), BF16),
                   jax.ShapeDtypeStruct((B, H_B, DV_B, DK_B), F32)],
        scratch_shapes=[pltpu.VMEM((H_B, DV_B, DK_B), F32)],
        compiler_params=_params("parallel", "arbitrary"),
        name="gla",
    )(qb, kb, vb, la, rb, g, s0_t)


def _merge_kernel(oa_ref, hb_ref, ga_ref, gb_ref, x_ref, wa_ref, wb_ref, wo_ref, g_ref, b_ref, x1_ref):
    ya = _dot(oa_ref[...], wa_ref[...])
    yb = _dot(hb_ref[...], wb_ref[...])
    merged = _sigmoid(ga_ref[...].astype(F32)) * ya + _sigmoid(gb_ref[...].astype(F32)) * yb
    z = ALPHA * x_ref[...] + _dot(merged.astype(BF16), wo_ref[...])
    x1_ref[...] = _layer_norm(z, g_ref[...], b_ref[...])


def _merge(oa, hb, ga, gb, x, wa, wb, wo, g, b, tm):
    B, T, D = x.shape

    def row(n):
        return pl.BlockSpec((None, tm, n), lambda bb, t: (bb, t, 0))

    return pl.pallas_call(
        _merge_kernel,
        grid=(B, T // tm),
        in_specs=[row(FOX_W), row(GLA_V), row(D), row(D), row(D),
                  _const_spec(wa.shape), _const_spec(wb.shape), _const_spec(wo.shape),
                  _const_spec(g.shape), _const_spec(b.shape)],
        out_specs=row(D),
        out_shape=jax.ShapeDtypeStruct((B, T, D), F32),
        compiler_params=_params("parallel", "parallel"),
        name="merge",
    )(oa, hb, ga, gb, x, wa, wb, wo, g, b)


def _gelu_tanh(x):
    return 0.5 * x * (1.0 + jnp.tanh(math.sqrt(2.0 / math.pi) * (x + 0.044715 * (x * x * x))))


def _ffn_kernel(x1_ref, p_ref, cprev_ref, wup_ref, cw_ref, cb_ref, wdn_ref, g2_ref, b2_ref,
                wpl_ref, wplg_ref, g3_ref, b3_ref, y_ref, cnew_ref, carry_ref):
    @pl.when(pl.program_id(1) == 0)
    def _():
        carry_ref[...] = cprev_ref[...]

    x1 = x1_ref[...]
    tm = x1.shape[0]
    up = _dot(x1.astype(BF16), wup_ref[...])
    prev = carry_ref[...]
    row = lax.broadcasted_iota(jnp.int32, (tm, 1), 0)
    up_m1 = jnp.where(row == 0, prev[1:2, :], pltpu.roll(up, 1, 0))
    up_m2 = jnp.where(row == 0, prev[0:1, :], jnp.where(row == 1, prev[1:2, :], pltpu.roll(up, 2, 0)))
    conv =cb_ref[...] + cw_ref[0:1, :] * up_m2 + cw_ref[1:2, :] * up_m1 + cw_ref[2:3, :] * up
    last2 = up[tm - 2:tm, :]
    carry_ref[...] = last2
    cnew_ref[...] = last2
    hgl = (conv[:, :D_FF] * _gelu_tanh(conv[:, D_FF:])).astype(BF16)
    z2 = ALPHA * x1 + _dot(hgl, wdn_ref[...])
    x2 = _layer_norm(z2, g2_ref[...], b2_ref[...])
    e = _dot(p_ref[...].astype(BF16), wpl_ref[...]) * _sigmoid(_dot(x2.astype(BF16), wplg_ref[...]))
    y_ref[...] = _layer_norm(ALPHA * x2 + e, g3_ref[...], b3_ref[...])


def _ffn(x1, p, cprev, wup, cw, cb, wdn, g2, b2, wpl, wplg, g3, b3, tm):
    B, T, D = x1.shape

    def row(n):
        return pl.BlockSpec((None, tm, n), lambda bb, t: (bb, t, 0))

    cspec = pl.BlockSpec((None, CONV_W - 1, 2 * D_FF), lambda bb, t: (bb, 0, 0))
    consts = (wup, cw, cb, wdn, g2, b2, wpl, wplg, g3, b3)
    return pl.pallas_call(
        _ffn_kernel,
        grid=(B, T // tm),
        in_specs=[row(D), row(PLE_DIM), cspec] + [_const_spec(c.shape) for c in consts],
        out_specs=[row(D), cspec],
        out_shape=[jax.ShapeDtypeStruct((B, T, D), F32),
                   jax.ShapeDtypeStruct((B, CONV_W - 1, 2 * D_FF), F32)],
        scratch_shapes=[pltpu.VMEM((CONV_W - 1, 2 * D_FF), F32)],
        compiler_params=_params("parallel", "arbitrary"),
        name="ffn",
    )(x1, p, cprev, *consts)


def _pick(T, want):
    t = min(T, want)
    while T % t:
        t //= 2
    return t


def _pack_w_in(w_in):
    pts, acc = {}, 0
    for name, n in (("qa", FOX_W), ("ka", FOX_W), ("va", FOX_W), ("fa", H_A), ("qb", GLA_K), ("kb", GLA_K),
                    ("vb", GLA_V), ("rb", GLA_V), ("a1", GLA_RANK), ("ga", D_MODEL), ("gb", D_MODEL)):
        pts[name] = (acc, n)
        acc += n
    cols = []
    for name in _SEG:
        lo, n = pts[name]
        piece = w_in[:, lo:lo + n]
        if n < _SEG[name][1]:
            piece = jnp.pad(piece, ((0, 0), (0, _SEG[name][1] - n)))
        cols.append(piece)
    return jnp.concatenate(cols, axis=1).astype(BF16)


def _layer(x, p, fox_past, s0, conv_prev, wts):
    (w_in, b_fgate, w_a2, b_a2, g_gla, w_a_out, w_b_out, w_o, ln1_g, ln1_b, w_up, conv_w, conv_b,
     w_down, ln2_g, ln2_b, w_pl, w_plg, ln3_g, ln3_b) = wts
    B, T, D = x.shape
    row2 = lambda a: a.reshape(1, -1).astype(F32)

    wcat = _pack_w_in(w_in)
    bf = jnp.pad(row2(b_fgate), ((0, 0), (0, LANES - H_A)))
    wa2 = jnp.pad(w_a2.astype(F32), ((0, LANES - GLA_RANK), (0, 0)))
    wa2h = wa2.astype(BF16)
    wa2l = (wa2 - wa2h.astype(F32)).astype(BF16)

    tm = _pick(T, 256)
    tq = _pick(T, 512)
    proj_out = _proj(x, wcat, bf, wa2h, wa2l, row2(b_a2), _pick(T, 512),
                     vt_block=tq if fox_past is None else None)
    (qa, ka_f, ka_b, va_f, va_b, logf, logf_pad, qb, kb, vb, rb, la, ga, gb) = proj_out

    if fox_past is None:
        kbias = _fox_bias(logf_pad, _pick(T, 256))
        oa = _fox_prompt(qa, ka_b, kbias, va_b, tq)
    else:
        past_k, past_v, past_logf = fox_past
        P = past_k.shape[1]
        lk = -(-(P + T) // FOX_KEY_PAD) * FOX_KEY_PAD
        padk = lambda a: jnp.pad(a, ((0, 0), (0, lk - P - T), (0, 0)))
        k_all = padk(jnp.concatenate([past_k.reshape(B, P, FOX_W).astype(BF16), ka_b], axis=1))
        v_all = padk(jnp.concatenate([past_v.reshape(B, P, FOX_W).astype(BF16), va_b], axis=1))
        lf_all = padk(jnp.concatenate([past_logf.astype(F32), logf], axis=1))
        nc = _neg_cumsum(jnp.transpose(lf_all, (0, 2, 1)), _pick(lk, 512))
        oa = _fox_sample(qa, k_all, v_all, nc.reshape(B, H_A // 2, 2, lk), P)

    chunk = min(GLA_CHUNK, T)
    tc = _pick(T, 4 * chunk)
    hb, s_t = _gla(qb, kb, vb, la, rb, row2(g_gla), jnp.swapaxes(s0.astype(F32), 2, 3), tc, chunk)
    s_fin = jnp.swapaxes(s_t, 2, 3)

    x1 = _merge(oa, hb, ga, gb, x, w_a_out.astype(BF16), w_b_out.astype(BF16), w_o.astype(BF16),
                row2(ln1_g), row2(ln1_b), _pick(T, 512))
    y, conv_new = _ffn(x1, p, conv_prev.astype(F32), w_up.astype(BF16), conv_w.astype(F32), row2(conv_b),
                       w_down.astype(BF16), row2(ln2_g), row2(ln2_b), w_pl.astype(BF16),
                       w_plg.astype(BF16), row2(ln3_g), row2(ln3_b), tm)
    k_out = ka_f.reshape(B, T, H_A, DH_A)
    v_out = va_f.reshape(B, T, H_A, DH_A)
    return y, k_out, v_out, logf, s_fin, conv_new


def kernel(x_prompt, x_sample, cache_fox_k, cache_fox_v, cache_fox_logf, state_gla, cache_ffn_conv, p_prompt, p_sample, w_in, b_fgate, w_a2, b_a2, g_gla, w_a_out, w_b_out, w_o, ln1_g, ln1_b, w_up, conv_w, conv_b, w_down, ln2_g, ln2_b, w_pl, w_plg, ln3_g, ln3_b):
    hp, hs = x_prompt, x_sample
    outs_p = [[] for _ in range(5)]
    outs_s = [[] for _ in range(5)]
    for i in range(DEPTH):
        wts = (w_in[i], b_fgate[i], w_a2[i], b_a2[i], g_gla[i], w_a_out[i], w_b_out[i], w_o[i],
               ln1_g[i], ln1_b[i], w_up[i], conv_w[i], conv_b[i], w_down[i], ln2_g[i], ln2_b[i],
               w_pl[i], w_plg[i], ln3_g[i], ln3_b[i])
        Bp = hp.shape[0]
        s0_p = jnp.zeros((Bp, H_B, DK_B, DV_B), F32)
        c0_p = jnp.zeros((Bp, CONV_W - 1, 2 * D_FF), F32)
        hp, *rest_p = _layer(hp, p_prompt[i], None, s0_p, c0_p, wts)
        hs, *rest_s = _layer(hs, p_sample[i], (cache_fox_k[i], cache_fox_v[i], cache_fox_logf[i]),
                             state_gla[i], cache_ffn_conv[i], wts)
        for dst, src in ((outs_p, rest_p), (outs_s, rest_s)):
            for lst, val in zip(dst, src):
                lst.append(val)
    return (hp, hs, *[jnp.stack(l) for l in outs_p], *[jnp.stack(l) for l in outs_s])
```

```python
import functools
import math

import jax
import jax.numpy as jnp
from jax import lax
from jax.experimental import pallas as pl
from jax.experimental.pallas import tpu as pltpu

F32 = jnp.float32
BF16 = jnp.bfloat16

D_MODEL = 1024
H_A, DH_A = 8, 64
FOX_W = H_A * DH_A
H_B, DK_B, DV_B = 4, 128, 256
GLA_K = H_B * DK_B
GLA_V = H_B * DV_B
GLA_RANK = 16
GLA_TAU = 16
D_FF = 2816
CONV_W = 3
PLE_DIM = 256
LN_EPS = 1e-5
DEPTH = 1
ALPHA = (2 * DEPTH) ** 0.25
GLA_CHUNK = 64
GLA_SUB = 16

LANES = 128
LOG2E = math.log2(math.e)
V7X_VMEM_LIMIT_BYTES = 56 * 1024 * 1024

_SEG = {}
_off = 0
for _name, _n in (("qa", FOX_W), ("ka", FOX_W), ("va", FOX_W), ("qb", GLA_K), ("kb", GLA_K),
                  ("vb", GLA_V), ("rb", GLA_V), ("ga", D_MODEL), ("gb", D_MODEL),
                  ("fa", LANES), ("a1", LANES)):
    _SEG[_name] = (_off, _n)
    _off += _n
W_CAT = _off


def _params(*sem):
    return pltpu.CompilerParams(dimension_semantics=sem, vmem_limit_bytes=V7X_VMEM_LIMIT_BYTES)


def _const_spec(shape):
    nd = len(shape)
    return pl.BlockSpec(shape, lambda *_: (0,) * nd, pipeline_mode=pl.Buffered(1))


def _log_sigmoid(z):
    return jnp.minimum(z, 0.0) - jnp.log(1.0 + jnp.exp(-jnp.abs(z)))


def _sigmoid(z):
    return 1.0 / (1.0 + jnp.exp(-z))


def _split3(a):
    hi = a.astype(BF16)
    r = a - hi.astype(F32)
    mid = r.astype(BF16)
    lo = (r - mid.astype(F32)).astype(BF16)
    return hi, mid, lo


def _layer_norm(z, g, b):
    mu = jnp.mean(z, axis=-1, keepdims=True)
    zc = z - mu
    var = jnp.mean(zc * zc, axis=-1, keepdims=True)
    return zc * lax.rsqrt(var + LN_EPS) * g + b


def _dot(a, b):
    return jnp.dot(a, b, preferred_element_type=F32)


def _dot_nt(a, b):
    return lax.dot_general(a, b, (((1,), (1,)), ((), ())), preferred_element_type=F32)


def _dot_tn(a, b):
    return lax.dot_general(a, b, (((0,), (0,)), ((), ())), preferred_element_type=F32)


def _proj_kernel(x_ref, w_ref, bf_ref, wa2h_ref, wa2l_ref, ba2_ref,
                 qa_ref, kaf_ref, kab_ref, vaf_ref, vab_ref, logf_ref, logfp_ref,
                 qb_ref, kb_ref, vb_ref, rb_ref, la_ref, ga_ref, gb_ref, *, values_transposed):
    xb = x_ref[...].astype(BF16)

    def seg(name):
        lo, n = _SEG[name]
        return _dot(xb, w_ref[:, lo:lo + n])

    qa_ref[...] = (seg("qa") * (DH_A ** -0.5 * LOG2E)).astype(BF16)
    ka = seg("ka")
    kaf_ref[...] = ka
    kab_ref[...] = ka.astype(BF16)
    va = seg("va")
    vaf_ref[...] = va
    if values_transposed:
        for p in range(H_A // 2):
            vab_ref[p] = va[:, p * LANES:(p + 1) * LANES].T.astype(BF16)
    else:
        vab_ref[...] = va.astype(BF16)
    qb_ref[...] = seg("qb") * (DK_B ** -0.5)
    kb_ref[...] = seg("kb")
    vb_ref[...] = seg("vb").astype(BF16)
    rb_ref[...] = seg("rb")
    ga_ref[...] = seg("ga").astype(BF16)
    gb_ref[...] = seg("gb").astype(BF16)
    logf = _log_sigmoid(seg("fa") + bf_ref[...])
    logf_ref[...] = logf[:, :H_A]
    logfp_ref[...] = logf
    a1 = seg("a1")
    a1h = a1.astype(BF16)
    a1l = (a1 - a1h.astype(F32)).astype(BF16)
    z = _dot(a1h, wa2h_ref[...]) + _dot(a1l, wa2h_ref[...]) + _dot(a1h, wa2l_ref[...]) + ba2_ref[...]
    la_ref[...] = _log_sigmoid(z) * (1.0 / GLA_TAU)


def _proj(x, wcat, bf, wa2h, wa2l, ba2, tm, vt_block=None):
    B, T, D = x.shape
    grid = (B, T // tm)

    def row(n, dt):
        return pl.BlockSpec((None, tm, n), lambda b, t: (b, t, 0)), jax.ShapeDtypeStruct((B, T, n), dt)

    outs = [row(FOX_W, BF16), row(FOX_W, F32), row(FOX_W, BF16), row(FOX_W, F32), row(FOX_W, BF16),
            row(H_A, F32), row(LANES, F32), row(GLA_K, F32), row(GLA_K, F32), row(GLA_V, BF16), row(GLA_V, F32),
            row(GLA_K, F32), row(D_MODEL, BF16), row(D_MODEL, BF16)]
    if vt_block is not None:
        per = vt_block // tm
        outs[4] = (pl.BlockSpec((None, H_A // 2, None, LANES, tm), lambda b, t: (b, 0, t // per, 0, t % per)),
                   jax.ShapeDtypeStruct((B, H_A // 2, T // vt_block, LANES, vt_block), BF16))
    return pl.pallas_call(
        functools.partial(_proj_kernel, values_transposed=vt_block is not None),
        grid=grid,
        in_specs=[pl.BlockSpec((None, tm, D), lambda b, t: (b, t, 0)),
                  _const_spec(wcat.shape), _const_spec(bf.shape), _const_spec(wa2h.shape),
                  _const_spec(wa2l.shape), _const_spec(ba2.shape)],
        out_specs=[o[0] for o in outs],
        out_shape=[o[1] for o in outs],
        compiler_params=_params("parallel", "parallel"),
        name="proj",
    )(x, wcat, bf, wa2h, wa2l, ba2)


def _cumsum_kernel(lf_ref, nc_ref, carry_ref):
    @pl.when(pl.program_id(1) == 0)
    def _():
        carry_ref[...] = jnp.zeros_like(carry_ref)

    x = lf_ref[...]
    tl = x.shape[1]
    r = lax.broadcasted_iota(jnp.int32, (tl, tl), 0)
    c = lax.broadcasted_iota(jnp.int32, (tl, tl), 1)
    tri = jnp.where(r <= c, 1.0, 0.0).astype(BF16)
    hi, mid, lo = _split3(x)
    cs = _dot(hi, tri) + _dot(mid, tri) + _dot(lo, tri) + carry_ref[...]
    nc_ref[...] = -cs
    carry_ref[...] = carry_ref[...] + jnp.sum(x, axis=1, keepdims=True)


def _neg_cumsum(lf_t, tl):
    B, H, L = lf_t.shape
    return pl.pallas_call(
        _cumsum_kernel,
        grid=(B, L // tl),
        in_specs=[pl.BlockSpec((None, H, tl), lambda b, t: (b, 0, t))],
        out_specs=pl.BlockSpec((None, H, tl), lambda b, t: (b, 0, t)),
        out_shape=jax.ShapeDtypeStruct((B, H, L), F32),
        scratch_shapes=[pltpu.VMEM((H, 1), F32)],
        compiler_params=_params("parallel", "arbitrary"),
        name="cumsum",
    )(lf_t)


FOX_KEY_PAD = 512
FOX_BIAS_PARTS = 3
FOX_UNROLL = 4
FOX_SUM_ROWS = 16


def _fox_bias_sel():
    sel = [[[0.0] * FOX_W for _ in range(LANES)] for _ in range(FOX_BIAS_PARTS)]
    for h in range(H_A):
        for j in range(FOX_BIAS_PARTS):
            sel[j][h][(h // 2) * LANES + FOX_BIAS_PARTS * (h % 2) + j] = 1.0
    return jnp.asarray(sel, dtype=BF16)


def _fox_bias_kernel(lf_ref, sel_ref, kb_ref, carry_ref):
    @pl.when(pl.program_id(1) == 0)
    def _():
        carry_ref[...] = jnp.zeros_like(carry_ref)

    x = lf_ref[...]
    tl = x.shape[0]
    r = lax.broadcasted_iota(jnp.int32, (tl, tl), 0)
    c = lax.broadcasted_iota(jnp.int32, (tl, tl), 1)
    tri = jnp.where(c <= r, 1.0, 0.0).astype(BF16)
    hi, mid, lo = _split3(x)
    cs = _dot(tri, hi) + _dot(tri, mid) + _dot(tri, lo) + carry_ref[...]
    carry_ref[...] = cs[tl - 1:tl, :]
    parts = _split3(cs * (-LOG2E))
    kb_ref[...] = (_dot(parts[0], sel_ref[0]) + _dot(parts[1], sel_ref[1])
                   + _dot(parts[2], sel_ref[2])).astype(kb_ref.dtype)


def _fox_bias(logf_pad, tl):
    B, T, _ = logf_pad.shape
    sel = _fox_bias_sel()
    return pl.pallas_call(
        _fox_bias_kernel,
        grid=(B, T // tl),
        in_specs=[pl.BlockSpec((None, tl, LANES), lambda b, t: (b, t, 0)), _const_spec(sel.shape)],
        out_specs=pl.BlockSpec((None, tl, FOX_W), lambda b, t: (b, t, 0)),
        out_shape=jax.ShapeDtypeStruct((B, T, FOX_W), BF16),
        scratch_shapes=[pltpu.VMEM((1, LANES), F32)],
        compiler_params=_params("parallel", "arbitrary"),
        name="fox_bias",
    )(logf_pad, sel)


def _fox_prompt_kernel(q_ref, k_ref, kb_ref, vt_ref, o_ref, m_ref, acc_ref,
                       sa_ref, sb_ref, bma_ref, bmb_ref, *, tq):
    i = pl.program_id(2)
    lane = lax.broadcasted_iota(jnp.int32, (1, LANES), 1)
    q = q_ref[...]
    zero = jnp.zeros_like(q)
    qa = []
    for h in range(2):
        own = (lane < DH_A) if h == 0 else (lane >= DH_A)
        ones = jnp.logical_and(lane >= FOX_BIAS_PARTS * h, lane < FOX_BIAS_PARTS * (h + 1))
        ones = jnp.broadcast_to(jnp.where(ones, 1.0, 0.0).astype(q.dtype), q.shape)
        qa.append(jnp.concatenate([jnp.where(own, q, zero), ones], axis=1))
    m_ref[...] = jnp.full_like(m_ref, -jnp.inf)
    acc_ref[...] = jnp.zeros_like(acc_ref)
    sum_rows = jnp.ones((FOX_SUM_ROWS, tq), BF16)

    def scores(j, s_ref, bm_ref):
        start = pl.multiple_of(j * tq, tq)
        ka = jnp.concatenate([k_ref[pl.ds(start, tq), :], kb_ref[pl.ds(start, tq), :]], axis=1)
        for h in range(2):
            st = _dot_nt(ka, qa[h])
            s_ref[h] = st
            bm_ref[h] = jnp.max(st, axis=0, keepdims=True)

    def soft_pv(j, s_ref, bm_ref, masked):
        for h in range(2):
            st = s_ref[h]
            if masked:
                r = lax.broadcasted_iota(jnp.int32, (tq, tq), 0)
                c = lax.broadcasted_iota(jnp.int32, (tq, tq), 1)
                st = jnp.where(r <= c, st, -jnp.inf)
                bm = jnp.max(st, axis=0, keepdims=True)
            else:
                bm = bm_ref[h]
            m_old = m_ref[h]
            m_new = jnp.maximum(m_old, bm)
            p = jnp.exp2(st - m_new)
            a = jnp.exp2(m_old - m_new)
            vt = jnp.concatenate([vt_ref[j, h * DH_A:(h + 1) * DH_A, :], sum_rows], axis=0)
            acc_ref[h] = a * acc_ref[h] + _dot(vt, p.astype(BF16))
            m_ref[h] = m_new

    slots = ((sa_ref, bma_ref), (sb_ref, bmb_ref))

    def run(j0, count, last_masked, prefetch_after):
        for u in range(count):
            if u + 1 < count or prefetch_after:
                scores(j0 + u + 1, *slots[(u + 1) % 2])
            soft_pv(j0 + u, *slots[u % 2], last_masked and u + 1 == count)

    scores(0, *slots[0])

    def body(t, carry):
        run(FOX_UNROLL * t, FOX_UNROLL, False, True)
        return carry

    log_unroll = FOX_UNROLL.bit_length() - 1
    lax.fori_loop(0, lax.shift_right_logical(i, log_unroll), body, 0)
    rest = jnp.bitwise_and(i, FOX_UNROLL - 1)
    for n in range(FOX_UNROLL):
        @pl.when(rest == n)
        def _(n=n):
            run(i - n, n + 1, True, False)

    ot = jnp.concatenate([acc_ref[h, :DH_A, :] / acc_ref[h, DH_A:DH_A + 1, :] for h in range(2)],
                         axis=0)
    o_ref[...] = ot.T.astype(o_ref.dtype)


def _fox_prompt(qa, ka, kbias, vt, tq):
    B, T, _ = qa.shape
    nq = T // tq
    return pl.pallas_call(
        functools.partial(_fox_prompt_kernel, tq=tq),
        grid=(B, H_A // 2, nq),
        in_specs=[pl.BlockSpec((None, tq, LANES), lambda b, p, i: (b, i, p)),
                  pl.BlockSpec((None, T, LANES), lambda b, p, i: (b, 0, p)),
                  pl.BlockSpec((None, T, LANES), lambda b, p, i: (b, 0, p)),
                  pl.BlockSpec((None, None, nq, LANES, tq), lambda b, p, i: (b, p, 0, 0, 0))],
        out_specs=pl.BlockSpec((None, tq, LANES), lambda b, p, i: (b, i, p)),
        out_shape=jax.ShapeDtypeStruct((B, T, FOX_W), BF16),
        scratch_shapes=[pltpu.VMEM((2, 1, tq), F32),
                        pltpu.VMEM((2, DH_A + FOX_SUM_ROWS, tq), F32),
                        pltpu.VMEM((2, tq, tq), F32), pltpu.VMEM((2, tq, tq), F32),
                        pltpu.VMEM((2, 1, tq), F32), pltpu.VMEM((2, 1, tq), F32)],
        compiler_params=_params("parallel", "parallel", "arbitrary"),
        name="fox_prompt",
    )(qa, ka, kbias, vt)


def _fox_sample_kernel(q_ref, k_ref, v_ref, nc_ref, o_ref, *, past):
    first = lax.broadcasted_iota(jnp.int32, (1, LANES), 1) < DH_A
    q = q_ref[...]
    zero = jnp.zeros_like(q)
    tq = q.shape[0]
    kb = k_ref[...]
    vb = v_ref[...]
    lk = kb.shape[0]
    r = lax.broadcasted_iota(jnp.int32, (tq, lk), 0)
    c = lax.broadcasted_iota(jnp.int32, (tq, lk), 1)
    visible = c <= r + past
    outs = []
    for h in range(2):
        qh = jnp.where(first, q, zero) if h == 0 else jnp.where(first, zero, q)
        s = _dot_nt(qh, kb) + nc_ref[h:h + 1, :] * LOG2E
        s = jnp.where(visible, s, -jnp.inf)
        m = jnp.max(s, axis=1, keepdims=True)
        p = jnp.exp2(s - m)
        l = jnp.sum(p, axis=1, keepdims=True)
        outs.append(_dot(p.astype(BF16), vb) / l)
    o_ref[...] = jnp.where(first, outs[0], outs[1]).astype(o_ref.dtype)


def _fox_sample(qa, k_all, v_all, nc, past):
    B, Tq, _ = qa.shape
    Lk = k_all.shape[1]
    return pl.pallas_call(
        functools.partial(_fox_sample_kernel, past=past),
        grid=(B, H_A // 2),
        in_specs=[pl.BlockSpec((None, Tq, LANES), lambda b, p: (b, 0, p)),
                  pl.BlockSpec((None, Lk, LANES), lambda b, p: (b, 0, p)),
                  pl.BlockSpec((None, Lk, LANES), lambda b, p: (b, 0, p)),
                  pl.BlockSpec((None, None, 2, Lk), lambda b, p: (b, p, 0, 0))],
        out_specs=pl.BlockSpec((None, Tq, LANES), lambda b, p: (b, 0, p)),
        out_shape=jax.ShapeDtypeStruct((B, Tq, FOX_W), BF16),
        compiler_params=_params("parallel", "parallel"),
        name="fox_sample",
    )(qa, k_all, v_all, nc)


def _gla_kernel(q_ref, k_ref, v_ref, la_ref, rb_ref, g_ref, s0_ref, hb_ref, sfin_ref, st_ref,
                *, chunk, sub):
    t = pl.program_id(1)

    @pl.when(t == 0)
    def _():
        st_ref[...] = s0_ref[...]

    tc = q_ref.shape[0]
    nsub = chunk // sub
    nchunk = tc // chunk
    r = lax.broadcasted_iota(jnp.int32, (tc, tc), 0)
    c = lax.broadcasted_iota(jnp.int32, (tc, tc), 1)
    log_chunk = chunk.bit_length() - 1
    log_sub = sub.bit_length() - 1
    same_chunk = jnp.right_shift(r, log_chunk) == jnp.right_shift(c, log_chunk)
    near = jnp.logical_and(c <= r, same_chunk)
    far = jnp.right_shift(c, log_chunk) < jnp.right_shift(r, log_chunk)
    tri = jnp.where(c <= r, 1.0, 0.0).astype(BF16)
    hi, mid, lo = _split3(la_ref[...])
    b_all = (_dot(tri, hi) + _dot(tri, mid) + _dot(tri, lo)) * LOG2E

    row = lax.broadcasted_iota(jnp.int32, (tc, 1), 0)
    row_sub = jnp.bitwise_and(jnp.right_shift(row, log_sub), nsub - 1)
    row_chunk = jnp.right_shift(row, log_chunk)

    def rows_of(refs, n):
        return jnp.concatenate([jnp.broadcast_to(rf, (n, DK_B)) for rf in refs], axis=0)

    def zeros(n):
        return jnp.zeros((n, DK_B), F32)

    for h in range(H_B):
        kc = slice(h * DK_B, (h + 1) * DK_B)
        vc = slice(h * DV_B, (h + 1) * DV_B)
        b = b_all[:, kc]
        q = q_ref[:, kc]
        k = k_ref[:, kc]
        v = v_ref[:, vc]
        b_last = b[tc - 1:tc, :]
        st = st_ref[h]
        sub_refs = [b[i * sub:i * sub + 1, :] for i in range(tc // sub)]
        chunk_refs = [b[ci * chunk:ci * chunk + 1, :] for ci in range(nchunk)]

        qt = q * jnp.exp2(b - rows_of(sub_refs, sub))
        qx = [jnp.where(row_sub == i, qt, 0.0) for i in range(nsub)]
        kx = []
        for i in range(nsub):
            pieces = []
            for ci in range(nchunk):
                lo_row, n = ci * chunk, (i + 1) * sub
                pieces.append(k[lo_row:lo_row + n] * jnp.exp2(sub_refs[ci * nsub + i] - b[lo_row:lo_row + n]))
                if n < chunk:
                    pieces.append(zeros(chunk - n))
            kx.append(jnp.concatenate(pieces, axis=0) if len(pieces) > 1 else pieces[0])
        a = jnp.where(near, _dot_nt(jnp.concatenate(qx, axis=1).astype(BF16),
                                    jnp.concatenate(kx, axis=1).astype(BF16)), 0.0)
        if nchunk > 1:
            qc = q * jnp.exp2(b - rows_of(chunk_refs, chunk))
            qx = [jnp.where(row_chunk == ci, qc, 0.0) for ci in range(1, nchunk)]
            kx = []
            for ci in range(1, nchunk):
                n = ci * chunk
                kx.append(jnp.concatenate([k[:n] * jnp.exp2(chunk_refs[ci] - b[:n]), zeros(tc - n)], axis=0))
            a = jnp.where(far, _dot_nt(jnp.concatenate(qx, axis=1).astype(BF16),
                                       jnp.concatenate(kx, axis=1).astype(BF16)), a)
        o = _dot(a.astype(BF16), v) + _dot_nt((q * jnp.exp2(b)).astype(BF16), st.astype(BF16))
        kd = (k * jnp.exp2(b_last - b)).astype(BF16)
        st_ref[h] = st * jnp.exp2(b_last) + _dot_tn(v, kd)
        ms = jnp.mean(o * o, axis=-1, keepdims=True)
        on = o * lax.rsqrt(ms + LN_EPS) * g_ref[:, vc]
        rbv = rb_ref[:, vc]
        hb_ref[:, vc] = (on * (rbv * _sigmoid(rbv))).astype(hb_ref.dtype)

    @pl.when(t == pl.num_programs(1) - 1)
    def _():
        sfin_ref[...] = st_ref[...]


def _gla(qb, kb, vb, la, rb, g, s0_t, tc, chunk):
    B, T, _ = qb.shape
    sub = min(GLA_SUB, chunk)

    def row(n):
        return pl.BlockSpec((None, tc, n), lambda b, t: (b, t, 0))

    st_spec = pl.BlockSpec((None, H_B, DV_B, DK_B), lambda b, t: (b, 0, 0, 0))
    return pl.pallas_call(
        functools.partial(_gla_kernel, chunk=chunk, sub=sub),
        grid=(B, T // tc),
        in_specs=[row(GLA_K), row(GLA_K), row(GLA_V), row(GLA_K), row(GLA_V),
                  _const_spec(g.shape), st_spec],
        out_specs=[row(GLA_V), st_spec],
        out_shape=[jax.ShapeDtypeStruct((B, T, GLA_V), BF16),
                   jax.ShapeDtypeStruct((B, H_B, DV_B, DK_B), F32)],
        scratch_shapes=[pltpu.VMEM((H_B, DV_B, DK_B), F32)],
        compiler_params=_params("parallel", "arbitrary"),
        name="gla",
    )(qb, kb, vb, la, rb, g, s0_t)


def _merge_kernel(oa_ref, hb_ref, ga_ref, gb_ref, x_ref, wa_ref, wb_ref, wo_ref, g_ref, b_ref, x1_ref):
    ya = _dot(oa_ref[...], wa_ref[...])
    yb = _dot(hb_ref[...], wb_ref[...])
    merged = _sigmoid(ga_ref[...].astype(F32)) * ya + _sigmoid(gb_ref[...].astype(F32)) * yb
    z = ALPHA * x_ref[...] + _dot(merged.astype(BF16), wo_ref[...])
    x1_ref[...] = _layer_norm(z, g_ref[...], b_ref[...])


def _merge(oa, hb, ga, gb, x, wa, wb, wo, g, b, tm):
    B, T, D = x.shape

    def row(n):
        return pl.BlockSpec((None, tm, n), lambda bb, t: (bb, t, 0))

    return pl.pallas_call(
        _merge_kernel,
        grid=(B, T // tm),
        in_specs=[row(FOX_W), row(GLA_V), row(D), row(D), row(D),
                  _const_spec(wa.shape), _const_spec(wb.shape), _const_spec(wo.shape),
                  _const_spec(g.shape), _const_spec(b.shape)],
        out_specs=row(D),
        out_shape=jax.ShapeDtypeStruct((B, T, D), F32),
        compiler_params=_params("parallel", "parallel"),
        name="merge",
    )(oa, hb, ga, gb, x, wa, wb, wo, g, b)


FFN_GROUP = 256


def _gelu_tanh(x):
    return 0.5 * x * (1.0 + jnp.tanh(math.sqrt(2.0 / math.pi) * (x + 0.044715 * (x * x * x))))


def _ffn_pack(a):
    lead = a.shape[:-1]
    return a.reshape(*lead, 2, D_FF // FFN_GROUP, FFN_GROUP).swapaxes(-3, -2).reshape(*lead, 2 * D_FF)


def _ffn_unpack(a):
    lead = a.shape[:-1]
    return a.reshape(*lead, D_FF // FFN_GROUP, 2, FFN_GROUP).swapaxes(-3, -2).reshape(*lead, 2 * D_FF)


def _ffn_kernel(x1_ref, p_ref, cprev_ref, wup_ref, cw_ref, cb_ref, wdn_ref, g2_ref, b2_ref,
                wpl_ref, wplg_ref, g3_ref, b3_ref, y_ref, cnew_ref, carry_ref):
    @pl.when(pl.program_id(1) == 0)
    def _():
        carry_ref[...] = cprev_ref[...]

    x1 = x1_ref[...]
    tm = x1.shape[0]
    up = _dot(x1.astype(BF16), wup_ref[...])
    prev = carry_ref[...]
    row = lax.broadcasted_iota(jnp.int32, (tm, 1), 0)
    up_m1 = jnp.where(row == 0, prev[1:2, :], pltpu.roll(up, 1, 0))
    up_m2 = jnp.where(row == 0, prev[0:1, :], jnp.where(row == 1, prev[1:2, :], pltpu.roll(up, 2, 0)))
    conv = cb_ref[...] + cw_ref[0:1, :] * up_m2 + cw_ref[1:2, :] * up_m1 + cw_ref[2:3, :] * up
    last2 = up[tm - 2:tm, :]
    carry_ref[...] = last2
    cnew_ref[...] = last2
    hgl = jnp.concatenate(
        [conv[:, 2 * j * FFN_GROUP:(2 * j + 1) * FFN_GROUP]
         * _gelu_tanh(conv[:, (2 * j + 1) * FFN_GROUP:(2 * j + 2) * FFN_GROUP])
         for j in range(D_FF // FFN_GROUP)], axis=1).astype(BF16)
    z2 = ALPHA * x1 + _dot(hgl, wdn_ref[...])
    x2 = _layer_norm(z2, g2_ref[...], b2_ref[...])
    e = _dot(p_ref[...].astype(BF16), wpl_ref[...]) * _sigmoid(_dot(x2.astype(BF16), wplg_ref[...]))
    y_ref[...] = _layer_norm(ALPHA * x2 + e, g3_ref[...], b3_ref[...])


def _ffn(x1, p, cprev, wup, cw, cb, wdn, g2, b2, wpl, wplg, g3, b3, tm):
    B, T, D = x1.shape

    def row(n):
        return pl.BlockSpec((None, tm, n), lambda bb, t: (bb, t, 0))

    cspec = pl.BlockSpec((None, CONV_W - 1, 2 * D_FF), lambda bb, t: (bb, 0, 0))
    consts = (wup, cw, cb, wdn, g2, b2, wpl, wplg, g3, b3)
    return pl.pallas_call(
        _ffn_kernel,
        grid=(B, T // tm),
        in_specs=[row(D), row(PLE_DIM), cspec] + [_const_spec(c.shape) for c in consts],
        out_specs=[row(D), cspec],
        out_shape=[jax.ShapeDtypeStruct((B, T, D), F32),
                   jax.ShapeDtypeStruct((B, CONV_W - 1, 2 * D_FF), F32)],
        scratch_shapes=[pltpu.VMEM((CONV_W - 1, 2 * D_FF), F32)],
        compiler_params=_params("parallel", "arbitrary"),
        name="ffn",
    )(x1, p, cprev, *consts)


def _pick(T, want):
    t = min(T, want)
    while T % t:
        t //= 2
    return t


def _pack_w_in(w_in):
    pts, acc = {}, 0
    for name, n in (("qa", FOX_W), ("ka", FOX_W), ("va", FOX_W), ("fa", H_A), ("qb", GLA_K), ("kb", GLA_K),
                    ("vb", GLA_V), ("rb", GLA_V), ("a1", GLA_RANK), ("ga", D_MODEL), ("gb", D_MODEL)):
        pts[name] = (acc, n)
        acc += n
    cols = []
    for name in _SEG:
        lo, n = pts[name]
        piece = w_in[:, lo:lo + n]
        if n < _SEG[name][1]:
            piece = jnp.pad(piece, ((0, 0), (0, _SEG[name][1] - n)))
        cols.append(piece)
    return jnp.concatenate(cols, axis=1).astype(BF16)


def _layer(x, p, fox_past, s0, conv_prev, wts):
    (w_in, b_fgate, w_a2, b_a2, g_gla, w_a_out, w_b_out, w_o, ln1_g, ln1_b, w_up, conv_w, conv_b,
     w_down, ln2_g, ln2_b, w_pl, w_plg, ln3_g, ln3_b) = wts
    B, T, D = x.shape
    row2 = lambda a: a.reshape(1, -1).astype(F32)

    wcat = _pack_w_in(w_in)
    bf = jnp.pad(row2(b_fgate), ((0, 0), (0, LANES - H_A)))
    wa2 = jnp.pad(w_a2.astype(F32), ((0, LANES - GLA_RANK), (0, 0)))
    wa2h = wa2.astype(BF16)
    wa2l = (wa2 - wa2h.astype(F32)).astype(BF16)

    tm = _pick(T, 256)
    tq = _pick(T, 512)
    proj_out = _proj(x, wcat, bf, wa2h, wa2l, row2(b_a2), _pick(T, 512),
                     vt_block=tq if fox_past is None else None)
    (qa, ka_f, ka_b, va_f, va_b, logf, logf_pad, qb, kb, vb, rb, la, ga, gb) = proj_out

    if fox_past is None:
        kbias = _fox_bias(logf_pad, _pick(T, 512))
        oa = _fox_prompt(qa, ka_b, kbias, va_b, tq)
    else:
        past_k, past_v, past_logf = fox_past
        P = past_k.shape[1]
        lk = -(-(P + T) // FOX_KEY_PAD) * FOX_KEY_PAD
        padk = lambda a: jnp.pad(a, ((0, 0), (0, lk - P - T), (0, 0)))
        k_all = padk(jnp.concatenate([past_k.reshape(B, P, FOX_W).astype(BF16), ka_b], axis=1))
        v_all = padk(jnp.concatenate([past_v.reshape(B, P, FOX_W).astype(BF16), va_b], axis=1))
        lf_all = padk(jnp.concatenate([past_logf.astype(F32), logf], axis=1))
        nc = _neg_cumsum(jnp.transpose(lf_all, (0, 2, 1)), _pick(lk, 512))
        oa = _fox_sample(qa, k_all, v_all, nc.reshape(B, H_A // 2, 2, lk), P)

    chunk = min(GLA_CHUNK, T)
    tc = _pick(T, 4 * chunk)
    hb, s_t = _gla(qb, kb, vb, la, rb, row2(g_gla), jnp.swapaxes(s0.astype(F32), 2, 3), tc, chunk)
    s_fin = jnp.swapaxes(s_t, 2, 3)

    x1 = _merge(oa, hb, ga, gb, x, w_a_out.astype(BF16), w_b_out.astype(BF16), w_o.astype(BF16),
                row2(ln1_g), row2(ln1_b), _pick(T, 512))
    y, conv_new = _ffn(x1, p, _ffn_pack(conv_prev.astype(F32)), _ffn_pack(w_up.astype(BF16)),
                       _ffn_pack(conv_w.astype(F32)), _ffn_pack(row2(conv_b)),
                       w_down.astype(BF16), row2(ln2_g), row2(ln2_b), w_pl.astype(BF16),
                       w_plg.astype(BF16), row2(ln3_g), row2(ln3_b), _pick(T, 512))
    conv_new = _ffn_unpack(conv_new)
    k_out = ka_f.reshape(B, T, H_A, DH_A)
    v_out = va_f.reshape(B, T, H_A, DH_A)
    return y, k_out, v_out, logf, s_fin, conv_new


def kernel(x_prompt, x_sample, cache_fox_k, cache_fox_v, cache_fox_logf, state_gla, cache_ffn_conv, p_prompt, p_sample, w_in, b_fgate, w_a2, b_a2, g_gla, w_a_out, w_b_out, w_o, ln1_g, ln1_b, w_up, conv_w, conv_b, w_down, ln2_g, ln2_b, w_pl, w_plg, ln3_g, ln3_b):
    hp, hs = x_prompt, x_sample
    outs_p = [[] for _ in range(5)]
    outs_s = [[] for _ in range(5)]
    for i in range(DEPTH):
        wts = (w_in[i], b_fgate[i], w_a2[i], b_a2[i], g_gla[i], w_a_out[i], w_b_out[i], w_o[i],
               ln1_g[i], ln1_b[i], w_up[i], conv_w[i], conv_b[i], w_down[i], ln2_g[i], ln2_b[i],
               w_pl[i], w_plg[i], ln3_g[i], ln3_b[i])
        Bp = hp.shape[0]
        s0_p = jnp.zeros((Bp, H_B, DK_B, DV_B), F32)
        c0_p = jnp.zeros((Bp, CONV_W - 1, 2 * D_FF), F32)
        hp, *rest_p = _layer(hp, p_prompt[i], None, s0_p, c0_p, wts)
        hs, *rest_s = _layer(hs, p_sample[i], (cache_fox_k[i], cache_fox_v[i], cache_fox_logf[i]),
                             state_gla[i], cache_ffn_conv[i], wts)
        for dst, src in ((outs_p, rest_p), (outs_s, rest_s)):
            for lst, val in zip(dst, src):
                lst.append(val)
    return (hp, hs, *[jnp.stack(l) for l in outs_p], *[jnp.stack(l) for l in outs_s])
```

```python
import functools
import math

import jax
import jax.numpy as jnp
from jax import lax
from jax.experimental import pallas as pl
from jax.experimental.pallas import tpu as pltpu

F32 = jnp.float32
BF16 = jnp.bfloat16

D_MODEL = 1024
H_A, DH_A = 8, 64
FOX_W = H_A * DH_A
H_B, DK_B, DV_B = 4, 128, 256
GLA_K = H_B * DK_B
GLA_V = H_B * DV_B
GLA_RANK = 16
GLA_TAU = 16
D_FF = 2816
CONV_W = 3
PLE_DIM = 256
LN_EPS = 1e-5
DEPTH = 1
ALPHA = (2 * DEPTH) ** 0.25
GLA_CHUNK = 64
GLA_SUB = 16

LANES = 128
LOG2E = math.log2(math.e)
V7X_VMEM_LIMIT_BYTES = 56 * 1024 * 1024

_SEG = {}
_off = 0
for _name, _n in (("qa", FOX_W), ("ka", FOX_W), ("va", FOX_W), ("qb", GLA_K), ("kb", GLA_K),
                  ("vb", GLA_V), ("rb", GLA_V), ("ga", D_MODEL), ("gb", D_MODEL),
                  ("fa", LANES), ("a1", LANES)):
    _SEG[_name] = (_off, _n)
    _off += _n
W_CAT = _off


def _params(*sem):
    return pltpu.CompilerParams(dimension_semantics=sem, vmem_limit_bytes=V7X_VMEM_LIMIT_BYTES)


def _const_spec(shape):
    nd = len(shape)
    return pl.BlockSpec(shape, lambda *_: (0,) * nd, pipeline_mode=pl.Buffered(1))


def _log_sigmoid(z):
    return jnp.minimum(z, 0.0) - jnp.log(1.0 + jnp.exp(-jnp.abs(z)))


def _sigmoid(z):
    return 1.0 / (1.0 + jnp.exp(-z))


def _split3(a):
    hi = a.astype(BF16)
    r = a - hi.astype(F32)
    mid = r.astype(BF16)
    lo = (r - mid.astype(F32)).astype(BF16)
    return hi, mid, lo


def _layer_norm(z, g, b):
    mu = jnp.mean(z, axis=-1, keepdims=True)
    zc = z - mu
    var = jnp.mean(zc * zc, axis=-1, keepdims=True)
    return zc * lax.rsqrt(var + LN_EPS) * g + b


def _dot(a, b):
    return jnp.dot(a, b, preferred_element_type=F32)


def _dot_nt(a, b):
    return lax.dot_general(a, b, (((1,), (1,)), ((), ())), preferred_element_type=F32)


def _dot_tn(a, b):
    return lax.dot_general(a, b, (((0,), (0,)), ((), ())), preferred_element_type=F32)


def _proj_kernel(x_ref, w_ref, bf_ref, wa2h_ref, wa2l_ref, ba2_ref,
                 qa_ref, kaf_ref, kab_ref, vaf_ref, vab_ref, logf_ref, logfp_ref,
                 qb_ref, kb_ref, vb_ref, rb_ref, la_ref, ga_ref, gb_ref, *, values_transposed):
    xb = x_ref[...].astype(BF16)

    def seg(name):
        lo, n = _SEG[name]
        return _dot(xb, w_ref[:, lo:lo + n])

    qa_ref[...] = (seg("qa") * (DH_A ** -0.5 * LOG2E)).astype(BF16)
    ka = seg("ka")
    kaf_ref[...] = ka
    kab_ref[...] = ka.astype(BF16)
    va = seg("va")
    vaf_ref[...] = va
    if values_transposed:
        for p in range(H_A // 2):
            vab_ref[p] = va[:, p * LANES:(p + 1) * LANES].T.astype(BF16)
    else:
        vab_ref[...] = va.astype(BF16)
    qb_ref[...] = seg("qb") * (DK_B ** -0.5)
    kb_ref[...] = seg("kb")
    vb_ref[...] = seg("vb").astype(BF16)
    rb_ref[...] = seg("rb")
    ga_ref[...] = seg("ga").astype(BF16)
    gb_ref[...] = seg("gb").astype(BF16)
    logf = _log_sigmoid(seg("fa") + bf_ref[...])
    logf_ref[...] = logf[:, :H_A]
    logfp_ref[...] = logf
    a1 = seg("a1")
    a1h = a1.astype(BF16)
    a1l = (a1 - a1h.astype(F32)).astype(BF16)
    z = _dot(a1h, wa2h_ref[...]) + _dot(a1l, wa2h_ref[...]) + _dot(a1h, wa2l_ref[...]) + ba2_ref[...]
    la_ref[...] = _log_sigmoid(z) * (1.0 / GLA_TAU)


def _proj(x, wcat, bf, wa2h, wa2l, ba2, tm, vt_block=None):
    B, T, D = x.shape
    grid = (B, T // tm)

    def row(n, dt):
        return pl.BlockSpec((None, tm, n), lambda b, t: (b, t, 0)), jax.ShapeDtypeStruct((B, T, n), dt)

    outs = [row(FOX_W, BF16), row(FOX_W, F32), row(FOX_W, BF16), row(FOX_W, F32), row(FOX_W, BF16),
            row(H_A, F32), row(LANES, F32), row(GLA_K, F32), row(GLA_K, F32), row(GLA_V, BF16), row(GLA_V, F32),
            row(GLA_K, F32), row(D_MODEL, BF16), row(D_MODEL, BF16)]
    if vt_block is not None:
        per = vt_block // tm
        outs[4] = (pl.BlockSpec((None, H_A // 2, None, LANES, tm), lambda b, t: (b, 0, t // per, 0, t % per)),
                   jax.ShapeDtypeStruct((B, H_A // 2, T // vt_block, LANES, vt_block), BF16))
    return pl.pallas_call(
        functools.partial(_proj_kernel, values_transposed=vt_block is not None),
        grid=grid,
        in_specs=[pl.BlockSpec((None, tm, D), lambda b, t: (b, t, 0)),
                  _const_spec(wcat.shape), _const_spec(bf.shape), _const_spec(wa2h.shape),
                  _const_spec(wa2l.shape), _const_spec(ba2.shape)],
        out_specs=[o[0] for o in outs],
        out_shape=[o[1] for o in outs],
        compiler_params=_params("parallel", "parallel"),
        name="proj",
    )(x, wcat, bf, wa2h, wa2l, ba2)


def _cumsum_kernel(lf_ref, nc_ref, carry_ref):
    @pl.when(pl.program_id(1) == 0)
    def _():
        carry_ref[...] = jnp.zeros_like(carry_ref)

    x = lf_ref[...]
    tl = x.shape[1]
    r = lax.broadcasted_iota(jnp.int32, (tl, tl), 0)
    c = lax.broadcasted_iota(jnp.int32, (tl, tl), 1)
    tri = jnp.where(r <= c, 1.0, 0.0).astype(BF16)
    hi, mid, lo = _split3(x)
    cs = _dot(hi, tri) + _dot(mid, tri) + _dot(lo, tri) + carry_ref[...]
    nc_ref[...] = -cs
    carry_ref[...] = carry_ref[...] + jnp.sum(x, axis=1, keepdims=True)


def _neg_cumsum(lf_t, tl):
    B, H, L = lf_t.shape
    return pl.pallas_call(
        _cumsum_kernel,
        grid=(B, L // tl),
        in_specs=[pl.BlockSpec((None, H, tl), lambda b, t: (b, 0, t))],
        out_specs=pl.BlockSpec((None, H, tl), lambda b, t: (b, 0, t)),
        out_shape=jax.ShapeDtypeStruct((B, H, L), F32),
        scratch_shapes=[pltpu.VMEM((H, 1), F32)],
        compiler_params=_params("parallel", "arbitrary"),
        name="cumsum",
    )(lf_t)


FOX_KEY_PAD = 512
FOX_BIAS_PARTS = 3
FOX_UNROLL = 4
FOX_SUM_ROWS = 16


def _fox_bias_sel():
    sel = [[[0.0] * FOX_W for _ in range(LANES)] for _ in range(FOX_BIAS_PARTS)]
    for h in range(H_A):
        for j in range(FOX_BIAS_PARTS):
            sel[j][h][(h // 2) * LANES + FOX_BIAS_PARTS * (h % 2) + j] = 1.0
    return jnp.asarray(sel, dtype=BF16)


def _fox_bias_kernel(lf_ref, sel_ref, kb_ref, carry_ref):
    @pl.when(pl.program_id(1) == 0)
    def _():
        carry_ref[...] = jnp.zeros_like(carry_ref)

    x = lf_ref[...]
    tl = x.shape[0]
    r = lax.broadcasted_iota(jnp.int32, (tl, tl), 0)
    c = lax.broadcasted_iota(jnp.int32, (tl, tl), 1)
    tri = jnp.where(c <= r, 1.0, 0.0).astype(BF16)
    hi, mid, lo = _split3(x)
    cs = _dot(tri, hi) + _dot(tri, mid) + _dot(tri, lo) + carry_ref[...]
    carry_ref[...] = cs[tl - 1:tl, :]
    parts = _split3(cs * (-LOG2E))
    kb_ref[...] = (_dot(parts[0], sel_ref[0]) + _dot(parts[1], sel_ref[1])
                   + _dot(parts[2], sel_ref[2])).astype(kb_ref.dtype)


def _fox_bias(logf_pad, tl):
    B, T, _ = logf_pad.shape
    sel = _fox_bias_sel()
    return pl.pallas_call(
        _fox_bias_kernel,
        grid=(B, T // tl),
        in_specs=[pl.BlockSpec((None, tl, LANES), lambda b, t: (b, t, 0)), _const_spec(sel.shape)],
        out_specs=pl.BlockSpec((None, tl, FOX_W), lambda b, t: (b, t, 0)),
        out_shape=jax.ShapeDtypeStruct((B, T, FOX_W), BF16),
        scratch_shapes=[pltpu.VMEM((1, LANES), F32)],
        compiler_params=_params("parallel", "arbitrary"),
        name="fox_bias",
    )(logf_pad, sel)


def _fox_prompt_kernel(q_ref, k_ref, kb_ref, vt_ref, o_ref, m_ref, acc_ref,
                       sa_ref, sb_ref, bma_ref, bmb_ref, *, tq, tk):
    i = pl.program_id(2)
    lane = lax.broadcasted_iota(jnp.int32, (1, LANES), 1)
    q = q_ref[...]
    zero = jnp.zeros_like(q)
    qa = []
    for h in range(2):
        own = (lane < DH_A) if h == 0 else (lane >= DH_A)
        ones = jnp.logical_and(lane >= FOX_BIAS_PARTS * h, lane < FOX_BIAS_PARTS * (h + 1))
        ones = jnp.broadcast_to(jnp.where(ones, 1.0, 0.0).astype(q.dtype), q.shape)
        qa.append(jnp.concatenate([jnp.where(own, q, zero), ones], axis=1))
    m_ref[...] = jnp.full_like(m_ref, -jnp.inf)
    acc_ref[...] = jnp.zeros_like(acc_ref)
    sum_rows = jnp.ones((FOX_SUM_ROWS, tk), BF16)
    per = tq // tk

    def scores(j, s_ref, bm_ref):
        start = pl.multiple_of(j * tk, tk)
        ka = jnp.concatenate([k_ref[pl.ds(start, tk), :], kb_ref[pl.ds(start, tk), :]], axis=1)
        for h in range(2):
            st = _dot_nt(ka, qa[h])
            s_ref[h] = st
            bm_ref[h] = jnp.max(st, axis=0, keepdims=True)

    def soft_pv(j, s_ref, bm_ref, diag):
        for h in range(2):
            st = s_ref[h]
            if diag is not None:
                r = lax.broadcasted_iota(jnp.int32, (tk, tq), 0)
                c = lax.broadcasted_iota(jnp.int32, (tk, tq), 1)
                st = jnp.where(r + diag * tk <= c, st, -jnp.inf)
                bm = jnp.max(st, axis=0, keepdims=True)
            else:
                bm = bm_ref[h]
            m_old = m_ref[h]
            m_new = jnp.maximum(m_old, bm)
            p = jnp.exp2(st - m_new)
            a = jnp.exp2(m_old - m_new)
            vt = jnp.concatenate([vt_ref[j, h * DH_A:(h + 1) * DH_A, :], sum_rows], axis=0)
            acc_ref[h] = a * acc_ref[h] + _dot(vt, p.astype(BF16))
            m_ref[h] = m_new

    slots = ((sa_ref, bma_ref), (sb_ref, bmb_ref))

    def run(j0, diags, prefetch_after):
        for u, diag in enumerate(diags):
            if u + 1 < len(diags) or prefetch_after:
                scores(j0 + u + 1, *slots[(u + 1) % 2])
            soft_pv(j0 + u, *slots[u % 2], diag)

    scores(0, *slots[0])

    def body(t, carry):
        run(FOX_UNROLL * t, [None] * FOX_UNROLL, True)
        return carry

    full = i * per
    log_unroll = FOX_UNROLL.bit_length() - 1
    lax.fori_loop(0, lax.shift_right_logical(full, log_unroll), body, 0)
    rest = jnp.bitwise_and(full, FOX_UNROLL - 1)
    for n in range(0, FOX_UNROLL, per):
        @pl.when(rest == n)
        def _(n=n):
            run(full - n, [None] * n + list(range(per)), False)

    ot = jnp.concatenate([acc_ref[h, :DH_A, :] / acc_ref[h, DH_A:DH_A + 1, :] for h in range(2)],
                         axis=0)
    o_ref[...] = ot.T.astype(o_ref.dtype)


def _fox_prompt(qa, ka, kbias, vt, tq, tk):
    B, T, _ = qa.shape
    nk = T // tk
    return pl.pallas_call(
        functools.partial(_fox_prompt_kernel, tq=tq, tk=tk),
        grid=(B, H_A // 2, T // tq),
        in_specs=[pl.BlockSpec((None, tq, LANES), lambda b, p, i: (b, i, p)),
                  pl.BlockSpec((None, T, LANES), lambda b, p, i: (b, 0, p)),
                  pl.BlockSpec((None, T, LANES), lambda b, p, i: (b, 0, p)),
                  pl.BlockSpec((None, None, nk, LANES, tk), lambda b, p, i: (b, p, 0, 0, 0))],
        out_specs=pl.BlockSpec((None, tq, LANES), lambda b, p, i: (b, i, p)),
        out_shape=jax.ShapeDtypeStruct((B, T, FOX_W), BF16),
        scratch_shapes=[pltpu.VMEM((2, 1, tq), F32),
                        pltpu.VMEM((2, DH_A + FOX_SUM_ROWS, tq), F32),
                        pltpu.VMEM((2, tk, tq), F32), pltpu.VMEM((2, tk, tq), F32),
                        pltpu.VMEM((2, 1, tq), F32), pltpu.VMEM((2, 1, tq), F32)],
        compiler_params=_params("parallel", "parallel", "arbitrary"),
        name="fox_prompt",
    )(qa, ka, kbias, vt)


def _fox_sample_kernel(q_ref, k_ref, v_ref, nc_ref, o_ref, *, past):
    first = lax.broadcasted_iota(jnp.int32, (1, LANES), 1) < DH_A
    q = q_ref[...]
    zero = jnp.zeros_like(q)
    tq = q.shape[0]
    kb = k_ref[...]
    vb = v_ref[...]
    lk = kb.shape[0]
    r = lax.broadcasted_iota(jnp.int32, (tq, lk), 0)
    c = lax.broadcasted_iota(jnp.int32, (tq, lk), 1)
    visible = c <= r + past
    outs = []
    for h in range(2):
        qh = jnp.where(first, q, zero) if h == 0 else jnp.where(first, zero, q)
        s = _dot_nt(qh, kb) + nc_ref[h:h + 1, :] * LOG2E
        s = jnp.where(visible, s, -jnp.inf)
        m = jnp.max(s, axis=1, keepdims=True)
        p = jnp.exp2(s - m)
        l = jnp.sum(p, axis=1, keepdims=True)
        outs.append(_dot(p.astype(BF16), vb) / l)
    o_ref[...] = jnp.where(first, outs[0], outs[1]).astype(o_ref.dtype)


def _fox_sample(qa, k_all, v_all, nc, past):
    B, Tq, _ = qa.shape
    Lk = k_all.shape[1]
    return pl.pallas_call(
        functools.partial(_fox_sample_kernel, past=past),
        grid=(B, H_A // 2),
        in_specs=[pl.BlockSpec((None, Tq, LANES), lambda b, p: (b, 0, p)),
                  pl.BlockSpec((None, Lk, LANES), lambda b, p: (b, 0, p)),
                  pl.BlockSpec((None, Lk, LANES), lambda b, p: (b, 0, p)),
                  pl.BlockSpec((None, None, 2, Lk), lambda b, p: (b, p, 0, 0))],
        out_specs=pl.BlockSpec((None, Tq, LANES), lambda b, p: (b, 0, p)),
        out_shape=jax.ShapeDtypeStruct((B, Tq, FOX_W), BF16),
        compiler_params=_params("parallel", "parallel"),
        name="fox_sample",
    )(qa, k_all, v_all, nc)


def _gla_kernel(q_ref, k_ref, v_ref, la_ref, rb_ref, g_ref, s0_ref, hb_ref, sfin_ref, st_ref,
                *, chunk, sub):
    t = pl.program_id(1)

    @pl.when(t == 0)
    def _():
        st_ref[...] = s0_ref[...]

    tc = q_ref.shape[0]
    nsub = chunk // sub
    nchunk = tc // chunk
    r = lax.broadcasted_iota(jnp.int32, (tc, tc), 0)
    c = lax.broadcasted_iota(jnp.int32, (tc, tc), 1)
    log_chunk = chunk.bit_length() - 1
    log_sub = sub.bit_length() - 1
    same_chunk = jnp.right_shift(r, log_chunk) == jnp.right_shift(c, log_chunk)
    near = jnp.logical_and(c <= r, same_chunk)
    far = jnp.right_shift(c, log_chunk) < jnp.right_shift(r, log_chunk)
    tri = jnp.where(c <= r, 1.0, 0.0).astype(BF16)
    hi, mid, lo = _split3(la_ref[...])
    b_all = (_dot(tri, hi) + _dot(tri, mid) + _dot(tri, lo)) * LOG2E

    row = lax.broadcasted_iota(jnp.int32, (tc, 1), 0)
    row_sub = jnp.bitwise_and(jnp.right_shift(row, log_sub), nsub - 1)
    row_chunk = jnp.right_shift(row, log_chunk)

    def rows_of(refs, n):
        return jnp.concatenate([jnp.broadcast_to(rf, (n, DK_B)) for rf in refs], axis=0)

    def zeros(n):
        return jnp.zeros((n, DK_B), F32)

    for h in range(H_B):
        kc = slice(h * DK_B, (h + 1) * DK_B)
        vc = slice(h * DV_B, (h + 1) * DV_B)
        b = b_all[:, kc]
        q = q_ref[:, kc]
        k = k_ref[:, kc]
        v = v_ref[:, vc]
        b_last = b[tc - 1:tc, :]
        st = st_ref[h]
        sub_refs = [b[i * sub:i * sub + 1, :] for i in range(tc // sub)]
        chunk_refs = [b[ci * chunk:ci * chunk + 1, :] for ci in range(nchunk)]

        qt = q * jnp.exp2(b - rows_of(sub_refs, sub))
        qx = [jnp.where(row_sub == i, qt, 0.0) for i in range(nsub)]
        kx = []
        for i in range(nsub):
            pieces = []
            for ci in range(nchunk):
                lo_row, n = ci * chunk, (i + 1) * sub
                pieces.append(k[lo_row:lo_row + n] * jnp.exp2(sub_refs[ci * nsub + i] - b[lo_row:lo_row + n]))
                if n < chunk:
                    pieces.append(zeros(chunk - n))
            kx.append(jnp.concatenate(pieces, axis=0) if len(pieces) > 1 else pieces[0])
        a = jnp.where(near, _dot_nt(jnp.concatenate(qx, axis=1).astype(BF16),
                                    jnp.concatenate(kx, axis=1).astype(BF16)), 0.0)
        if nchunk > 1:
            qc = q * jnp.exp2(b - rows_of(chunk_refs, chunk))
            qx = [jnp.where(row_chunk == ci, qc, 0.0) for ci in range(1, nchunk)]
            kx = []
            for ci in range(1, nchunk):
                n = ci * chunk
                kx.append(jnp.concatenate([k[:n] * jnp.exp2(chunk_refs[ci] - b[:n]), zeros(tc - n)], axis=0))
            a = jnp.where(far, _dot_nt(jnp.concatenate(qx, axis=1).astype(BF16),
                                       jnp.concatenate(kx, axis=1).astype(BF16)), a)
        o = _dot(a.astype(BF16), v) + _dot_nt((q * jnp.exp2(b)).astype(BF16), st.astype(BF16))
        kd = (k * jnp.exp2(b_last - b)).astype(BF16)
        st_ref[h] = st * jnp.exp2(b_last) + _dot_tn(v, kd)
        ms = jnp.mean(o * o, axis=-1, keepdims=True)
        on = o * lax.rsqrt(ms + LN_EPS) * g_ref[:, vc]
        rbv = rb_ref[:, vc]
        hb_ref[:, vc] = (on * (rbv * _sigmoid(rbv))).astype(hb_ref.dtype)

    @pl.when(t == pl.num_programs(1) - 1)
    def _():
        sfin_ref[...] = st_ref[...]


def _gla(qb, kb, vb, la, rb, g, s0_t, tc, chunk):
    B, T, _ = qb.shape
    sub = min(GLA_SUB, chunk)

    def row(n):
        return pl.BlockSpec((None, tc, n), lambda b, t: (b, t, 0))

    st_spec = pl.BlockSpec((None, H_B, DV_B, DK_B), lambda b, t: (b, 0, 0, 0))
    return pl.pallas_call(
        functools.partial(_gla_kernel, chunk=chunk, sub=sub),
        grid=(B, T // tc),
        in_specs=[row(GLA_K), row(GLA_K), row(GLA_V), row(GLA_K), row(GLA_V),
                  _const_spec(g.shape), st_spec],
        out_specs=[row(GLA_V), st_spec],
        out_shape=[jax.ShapeDtypeStruct((B, T, GLA_V), BF16),
                   jax.ShapeDtypeStruct((B, H_B, DV_B, DK_B), F32)],
        scratch_shapes=[pltpu.VMEM((H_B, DV_B, DK_B), F32)],
        compiler_params=_params("parallel", "arbitrary"),
        name="gla",
    )(qb, kb, vb, la, rb, g, s0_t)


def _merge_kernel(oa_ref, hb_ref, ga_ref, gb_ref, x_ref, wa_ref, wb_ref, wo_ref, g_ref, b_ref, x1_ref):
    ya = _dot(oa_ref[...], wa_ref[...])
    yb = _dot(hb_ref[...], wb_ref[...])
    merged = _sigmoid(ga_ref[...].astype(F32)) * ya + _sigmoid(gb_ref[...].astype(F32)) * yb
    z = ALPHA * x_ref[...] + _dot(merged.astype(BF16), wo_ref[...])
    x1_ref[...] = _layer_norm(z, g_ref[...], b_ref[...])


def _merge(oa, hb, ga, gb, x, wa, wb, wo, g, b, tm):
    B, T, D = x.shape

    def row(n):
        return pl.BlockSpec((None, tm, n), lambda bb, t: (bb, t, 0))

    return pl.pallas_call(
        _merge_kernel,
        grid=(B, T // tm),
        in_specs=[row(FOX_W), row(GLA_V), row(D), row(D), row(D),
                  _const_spec(wa.shape), _const_spec(wb.shape), _const_spec(wo.shape),
                  _const_spec(g.shape), _const_spec(b.shape)],
        out_specs=row(D),
        out_shape=jax.ShapeDtypeStruct((B, T, D), F32),
        compiler_params=_params("parallel", "parallel"),
        name="merge",
    )(oa, hb, ga, gb, x, wa, wb, wo, g, b)


FFN_GROUP = 256


def _gelu_tanh(x):
    return 0.5 * x * (1.0 + jnp.tanh(math.sqrt(2.0 / math.pi) * (x + 0.044715 * (x * x * x))))


def _ffn_pack(a):
    return jnp.concatenate([a[..., k * D_FF + j * FFN_GROUP:k * D_FF + (j + 1) * FFN_GROUP]
                            for j in range(D_FF // FFN_GROUP) for k in range(2)], axis=-1)


def _ffn_unpack(a):
    lead = a.shape[:-1]
    return a.reshape(*lead, D_FF // FFN_GROUP, 2, FFN_GROUP).swapaxes(-3, -2).reshape(*lead, 2 * D_FF)


def _ffn_kernel(x1_ref, p_ref, cprev_ref, wup_ref, cw_ref, cb_ref, wdn_ref, g2_ref, b2_ref,
                wpl_ref, wplg_ref, g3_ref, b3_ref, y_ref, cnew_ref, carry_ref):
    @pl.when(pl.program_id(1) == 0)
    def _():
        carry_ref[...] = cprev_ref[...]

    x1 = x1_ref[...]
    tm = x1.shape[0]
    up = _dot(x1.astype(BF16), wup_ref[...])
    prev = carry_ref[...]
    row = lax.broadcasted_iota(jnp.int32, (tm, 1), 0)
    up_m1 = jnp.where(row == 0, prev[1:2, :], pltpu.roll(up, 1, 0))
    up_m2 = jnp.where(row == 0, prev[0:1, :], jnp.where(row == 1, prev[1:2, :], pltpu.roll(up, 2, 0)))
    conv = cb_ref[...] + cw_ref[0:1, :] * up_m2 + cw_ref[1:2, :] * up_m1 + cw_ref[2:3, :] * up
    last2 = up[tm - 2:tm, :]
    carry_ref[...] = last2
    cnew_ref[...] = last2
    hgl = jnp.concatenate(
        [conv[:, 2 * j * FFN_GROUP:(2 * j + 1) * FFN_GROUP]
         * _gelu_tanh(conv[:, (2 * j + 1) * FFN_GROUP:(2 * j + 2) * FFN_GROUP])
         for j in range(D_FF // FFN_GROUP)], axis=1).astype(BF16)
    z2 = ALPHA * x1 + _dot(hgl, wdn_ref[...])
    x2 = _layer_norm(z2, g2_ref[...], b2_ref[...])
    e = _dot(p_ref[...].astype(BF16), wpl_ref[...]) * _sigmoid(_dot(x2.astype(BF16), wplg_ref[...]))
    y_ref[...] = _layer_norm(ALPHA * x2 + e, g3_ref[...], b3_ref[...])


def _ffn(x1, p, cprev, wup, cw, cb, wdn, g2, b2, wpl, wplg, g3, b3, tm):
    B, T, D = x1.shape

    def row(n):
        return pl.BlockSpec((None, tm, n), lambda bb, t: (bb, t, 0))

    cspec = pl.BlockSpec((None, CONV_W - 1, 2 * D_FF), lambda bb, t: (bb, 0, 0))
    consts = (wup, cw, cb, wdn, g2, b2, wpl, wplg, g3, b3)
    return pl.pallas_call(
        _ffn_kernel,
        grid=(B, T // tm),
        in_specs=[row(D), row(PLE_DIM), cspec] + [_const_spec(c.shape) for c in consts],
        out_specs=[row(D), cspec],
        out_shape=[jax.ShapeDtypeStruct((B, T, D), F32),
                   jax.ShapeDtypeStruct((B, CONV_W - 1, 2 * D_FF), F32)],
        scratch_shapes=[pltpu.VMEM((CONV_W - 1, 2 * D_FF), F32)],
        compiler_params=_params("parallel", "arbitrary"),
        name="ffn",
    )(x1, p, cprev, *consts)


def _pick(T, want):
    t = min(T, want)
    while T % t:
        t //= 2
    return t


def _pack_w_in(w_in):
    pts, acc = {}, 0
    for name, n in (("qa", FOX_W), ("ka", FOX_W), ("va", FOX_W), ("fa", H_A), ("qb", GLA_K), ("kb", GLA_K),
                    ("vb", GLA_V), ("rb", GLA_V), ("a1", GLA_RANK), ("ga", D_MODEL), ("gb", D_MODEL)):
        pts[name] = (acc, n)
        acc += n
    cols = []
    for name in _SEG:
        lo, n = pts[name]
        piece = w_in[:, lo:lo + n]
        if n < _SEG[name][1]:
            piece = jnp.pad(piece, ((0, 0), (0, _SEG[name][1] - n)))
        cols.append(piece)
    return jnp.concatenate(cols, axis=1).astype(BF16)


def _layer(x, p, fox_past, s0, conv_prev, wts):
    (w_in, b_fgate, w_a2, b_a2, g_gla, w_a_out, w_b_out, w_o, ln1_g, ln1_b, w_up, conv_w, conv_b,
     w_down, ln2_g, ln2_b, w_pl, w_plg, ln3_g, ln3_b) = wts
    B, T, D = x.shape
    row2 = lambda a: a.reshape(1, -1).astype(F32)

    wcat = _pack_w_in(w_in)
    bf = jnp.pad(row2(b_fgate), ((0, 0), (0, LANES - H_A)))
    wa2 = jnp.pad(w_a2.astype(F32), ((0, LANES - GLA_RANK), (0, 0)))
    wa2h = wa2.astype(BF16)
    wa2l = (wa2 - wa2h.astype(F32)).astype(BF16)

    tq, tk = _pick(T, 512), _pick(T, 512)
    fold = fox_past is not None and T < LANES
    rows = (lambda a: a.reshape(1, B * T, a.shape[-1])) if fold else (lambda a: a)
    unrows = (lambda a: a.reshape(B, T, a.shape[-1])) if fold else (lambda a: a)
    proj_out = _proj(rows(x), wcat, bf, wa2h, wa2l, row2(b_a2), _pick(B * T if fold else T, 512),
                     vt_block=tk if fox_past is None else None)
    if fold:
        proj_out = [unrows(o) for o in proj_out]
    (qa, ka_f, ka_b, va_f, va_b, logf, logf_pad, qb, kb, vb, rb, la, ga, gb) = proj_out

    if fox_past is None:
        kbias = _fox_bias(logf_pad, _pick(T, 512))
        oa = _fox_prompt(qa, ka_b, kbias, va_b, tq, tk)
    else:
        past_k, past_v, past_logf = fox_past
        P = past_k.shape[1]
        lk = -(-(P + T) // FOX_KEY_PAD) * FOX_KEY_PAD
        padk = lambda a: jnp.pad(a, ((0, 0), (0, lk - P - T), (0, 0)))
        k_all = padk(jnp.concatenate([past_k.reshape(B, P, FOX_W).astype(BF16), ka_b], axis=1))
        v_all = padk(jnp.concatenate([past_v.reshape(B, P, FOX_W).astype(BF16), va_b], axis=1))
        lf_all = padk(jnp.concatenate([past_logf.astype(F32), logf], axis=1))
        nc = _neg_cumsum(jnp.transpose(lf_all, (0, 2, 1)), _pick(lk, 512))
        oa = _fox_sample(qa, k_all, v_all, nc.reshape(B, H_A // 2, 2, lk), P)

    chunk = min(GLA_CHUNK, T)
    tc = _pick(T, 4 * chunk)
    hb, s_t = _gla(qb, kb, vb, la, rb, row2(g_gla), jnp.swapaxes(s0.astype(F32), 2, 3), tc, chunk)
    s_fin = jnp.swapaxes(s_t, 2, 3)

    x1 = unrows(_merge(rows(oa), rows(hb), rows(ga), rows(gb), rows(x), w_a_out.astype(BF16),
                       w_b_out.astype(BF16), w_o.astype(BF16), row2(ln1_g), row2(ln1_b),
                       _pick(B * T if fold else T, 512)))
    y, conv_new = _ffn(x1, p, _ffn_pack(conv_prev.astype(F32)), _ffn_pack(w_up.astype(BF16)),
                       _ffn_pack(conv_w.astype(F32)), _ffn_pack(row2(conv_b)),
                       w_down.astype(BF16), row2(ln2_g), row2(ln2_b), w_pl.astype(BF16),
                       w_plg.astype(BF16), row2(ln3_g), row2(ln3_b), _pick(T, 512))
    conv_new = _ffn_unpack(conv_new)
    k_out = ka_f.reshape(B, T, H_A, DH_A)
    v_out = va_f.reshape(B, T, H_A, DH_A)
    return y, k_out, v_out, logf, s_fin, conv_new


def kernel(x_prompt, x_sample, cache_fox_k, cache_fox_v, cache_fox_logf, state_gla, cache_ffn_conv, p_prompt, p_sample, w_in, b_fgate, w_a2, b_a2, g_gla, w_a_out, w_b_out, w_o, ln1_g, ln1_b, w_up, conv_w, conv_b, w_down, ln2_g, ln2_b, w_pl, w_plg, ln3_g, ln3_b):
    hp, hs = x_prompt, x_sample
    outs_p = [[] for _ in range(5)]
    outs_s = [[] for _ in range(5)]
    for i in range(DEPTH):
        wts = (w_in[i], b_fgate[i], w_a2[i], b_a2[i], g_gla[i], w_a_out[i], w_b_out[i], w_o[i],
               ln1_g[i], ln1_b[i], w_up[i], conv_w[i], conv_b[i], w_down[i], ln2_g[i], ln2_b[i],
               w_pl[i], w_plg[i], ln3_g[i], ln3_b[i])
        Bp = hp.shape[0]
        s0_p = jnp.zeros((Bp, H_B, DK_B, DV_B), F32)
        c0_p = jnp.zeros((Bp, CONV_W - 1, 2 * D_FF), F32)
        hp, *rest_p = _layer(hp, p_prompt[i], None, s0_p, c0_p, wts)
        hs, *rest_s = _layer(hs, p_sample[i], (cache_fox_k[i], cache_fox_v[i], cache_fox_logf[i]),
                             state_gla[i], cache_ffn_conv[i], wts)
        for dst, src in ((outs_p, rest_p), (outs_s, rest_s)):
            for lst, val in zip(dst, src):
                lst.append(val)
    return (hp, hs, *[jnp.stack(l) for l in outs_p], *[jnp.stack(l) for l in outs_s])
```

```python
import functools
import math

import jax
import jax.numpy as jnp
from jax import lax
from jax.experimental import pallas as pl
from jax.experimental.pallas import tpu as pltpu

F32 = jnp.float32
BF16 = jnp.bfloat16

D_MODEL = 1024
H_A, DH_A = 8, 64
FOX_W = H_A * DH_A
H_B, DK_B, DV_B = 4, 128, 256
GLA_K = H_B * DK_B
GLA_V = H_B * DV_B
GLA_RANK = 16
GLA_TAU = 16
D_FF = 2816
CONV_W = 3
PLE_DIM = 256
LN_EPS = 1e-5
DEPTH = 1
ALPHA = (2 * DEPTH) ** 0.25
GLA_CHUNK = 64
GLA_SUB = 16

LANES = 128
LOG2E = math.log2(math.e)
V7X_VMEM_LIMIT_BYTES = 56 * 1024 * 1024

_SEG = {}
_off = 0
for _name, _n in (("qa", FOX_W), ("ka", FOX_W), ("va", FOX_W), ("qb", GLA_K), ("kb", GLA_K),
                  ("vb", GLA_V), ("rb", GLA_V), ("ga", D_MODEL), ("gb", D_MODEL),
                  ("fa", LANES), ("a1", LANES)):
    _SEG[_name] = (_off, _n)
    _off += _n
W_CAT = _off


def _params(*sem):
    return pltpu.CompilerParams(dimension_semantics=sem, vmem_limit_bytes=V7X_VMEM_LIMIT_BYTES)


def _const_spec(shape):
    nd = len(shape)
    return pl.BlockSpec(shape, lambda *_: (0,) * nd, pipeline_mode=pl.Buffered(1))


def _log_sigmoid(z):
    return jnp.minimum(z, 0.0) - jnp.log(1.0 + jnp.exp(-jnp.abs(z)))


def _sigmoid(z):
    return 1.0 / (1.0 + jnp.exp(-z))


def _split3(a):
    hi = a.astype(BF16)
    r = a - hi.astype(F32)
    mid = r.astype(BF16)
    lo = (r - mid.astype(F32)).astype(BF16)
    return hi, mid, lo


def _layer_norm(z, g, b):
    mu = jnp.mean(z, axis=-1, keepdims=True)
    zc = z - mu
    var = jnp.mean(zc * zc, axis=-1, keepdims=True)
    return zc * lax.rsqrt(var + LN_EPS) * g + b


def _dot(a, b):
    return jnp.dot(a, b, preferred_element_type=F32)


def _dot_nt(a, b):
    return lax.dot_general(a, b, (((1,), (1,)), ((), ())), preferred_element_type=F32)


def _dot_tn(a, b):
    return lax.dot_general(a, b, (((0,), (0,)), ((), ())), preferred_element_type=F32)


def _proj_kernel(x_ref, w_ref, bf_ref, wa2h_ref, wa2l_ref, ba2_ref,
                 qa_ref, kaf_ref, kab_ref, vaf_ref, vab_ref, logf_ref, logfp_ref,
                 qb_ref, kb_ref, vb_ref, rb_ref, la_ref, ga_ref, gb_ref, *, values_transposed):
    xb = x_ref[...].astype(BF16)

    def seg(name):
        lo, n = _SEG[name]
        return _dot(xb, w_ref[:, lo:lo + n])

    qa_ref[...] = (seg("qa") * (DH_A ** -0.5 * LOG2E)).astype(BF16)
    ka = seg("ka")
    kaf_ref[...] = ka
    kab_ref[...] = ka.astype(BF16)
    va = seg("va")
    vaf_ref[...] = va
    if values_transposed:
        for p in range(H_A // 2):
            vab_ref[p] = va[:, p * LANES:(p + 1) * LANES].T.astype(BF16)
    else:
        vab_ref[...] = va.astype(BF16)
    qb_ref[...] = seg("qb") * (DK_B ** -0.5)
    kb_ref[...] = seg("kb")
    vb_ref[...] = seg("vb").astype(BF16)
    rb_ref[...] = seg("rb")
    ga_ref[...] = seg("ga").astype(BF16)
    gb_ref[...] = seg("gb").astype(BF16)
    logf = _log_sigmoid(seg("fa") + bf_ref[...])
    logf_ref[...] = logf.T[:H_A, :]
    logfp_ref[...] = logf
    a1 = seg("a1")
    a1h = a1.astype(BF16)
    a1l = (a1 - a1h.astype(F32)).astype(BF16)
    z = _dot(a1h, wa2h_ref[...]) + _dot(a1l, wa2h_ref[...]) + _dot(a1h, wa2l_ref[...]) + ba2_ref[...]
    la_ref[...] = _log_sigmoid(z) * (1.0 / GLA_TAU)


def _proj(x, wcat, bf, wa2h, wa2l, ba2, tm, vt_block=None):
    B, T, D = x.shape
    grid = (B, T // tm)

    def row(n, dt):
        return pl.BlockSpec((None, tm, n), lambda b, t: (b, t, 0)), jax.ShapeDtypeStruct((B, T, n), dt)

    outs = [row(FOX_W, BF16), row(FOX_W, F32), row(FOX_W, BF16), row(FOX_W, F32), row(FOX_W, BF16),
            row(H_A, F32), row(LANES, F32), row(GLA_K, F32), row(GLA_K, F32), row(GLA_V, BF16), row(GLA_V, F32),
            row(GLA_K, F32), row(D_MODEL, BF16), row(D_MODEL, BF16)]
    outs[5] = (pl.BlockSpec((None, H_A, tm), lambda b, t: (b, 0, t)), jax.ShapeDtypeStruct((B, H_A, T), F32))
    if vt_block is not None:
        per = vt_block // tm
        outs[4] = (pl.BlockSpec((None, H_A // 2, None, LANES, tm), lambda b, t: (b, 0, t // per, 0, t % per)),
                   jax.ShapeDtypeStruct((B, H_A // 2, T // vt_block, LANES, vt_block), BF16))
    return pl.pallas_call(
        functools.partial(_proj_kernel, values_transposed=vt_block is not None),
        grid=grid,
        in_specs=[pl.BlockSpec((None, tm, D), lambda b, t: (b, t, 0)),
                  _const_spec(wcat.shape), _const_spec(bf.shape), _const_spec(wa2h.shape),
                  _const_spec(wa2l.shape), _const_spec(ba2.shape)],
        out_specs=[o[0] for o in outs],
        out_shape=[o[1] for o in outs],
        compiler_params=_params("parallel", "parallel"),
        name="proj",
    )(x, wcat, bf, wa2h, wa2l, ba2)


def _cumsum_kernel(lf_ref, nc_ref, carry_ref):
    @pl.when(pl.program_id(1) == 0)
    def _():
        carry_ref[...] = jnp.zeros_like(carry_ref)

    x = lf_ref[...]
    tl = x.shape[1]
    r = lax.broadcasted_iota(jnp.int32, (tl, tl), 0)
    c = lax.broadcasted_iota(jnp.int32, (tl, tl), 1)
    tri = jnp.where(r <= c, 1.0, 0.0).astype(BF16)
    hi, mid, lo = _split3(x)
    cs = _dot(hi, tri) + _dot(mid, tri) + _dot(lo, tri) + carry_ref[...]
    nc_ref[...] = -cs
    carry_ref[...] = carry_ref[...] + jnp.sum(x, axis=1, keepdims=True)


def _neg_cumsum(lf_t, tl):
    B, H, L = lf_t.shape
    return pl.pallas_call(
        _cumsum_kernel,
        grid=(B, L // tl),
        in_specs=[pl.BlockSpec((None, H, tl), lambda b, t: (b, 0, t))],
        out_specs=pl.BlockSpec((None, H, tl), lambda b, t: (b, 0, t)),
        out_shape=jax.ShapeDtypeStruct((B, H, L), F32),
        scratch_shapes=[pltpu.VMEM((H, 1), F32)],
        compiler_params=_params("parallel", "arbitrary"),
        name="cumsum",
    )(lf_t)


FOX_KEY_PAD = 512
FOX_BIAS_PARTS = 3
FOX_UNROLL = 4
FOX_SUM_ROWS = 16


def _fox_bias_sel():
    sel = [[[0.0] * FOX_W for _ in range(LANES)] for _ in range(FOX_BIAS_PARTS)]
    for h in range(H_A):
        for j in range(FOX_BIAS_PARTS):
            sel[j][h][(h // 2) * LANES + FOX_BIAS_PARTS * (h % 2) + j] = 1.0
    return jnp.asarray(sel, dtype=BF16)


def _fox_bias_kernel(lf_ref, sel_ref, kb_ref, carry_ref):
    @pl.when(pl.program_id(1) == 0)
    def _():
        carry_ref[...] = jnp.zeros_like(carry_ref)

    x = lf_ref[...]
    tl = x.shape[0]
    r = lax.broadcasted_iota(jnp.int32, (tl, tl), 0)
    c = lax.broadcasted_iota(jnp.int32, (tl, tl), 1)
    tri = jnp.where(c <= r, 1.0, 0.0).astype(BF16)
    sums = _dot(tri, jnp.concatenate(_split3(x), axis=1))
    cs = sums[:, :LANES] + sums[:, LANES:2 * LANES] + sums[:, 2 * LANES:] + carry_ref[...]
    carry_ref[...] = cs[tl - 1:tl, :]
    parts = jnp.concatenate(_split3(cs * (-LOG2E)), axis=1)
    kb_ref[...] = _dot(parts, sel_ref[...]).astype(kb_ref.dtype)


def _fox_bias(logf_pad, tl):
    B, T, _ = logf_pad.shape
    sel = _fox_bias_sel().reshape(FOX_BIAS_PARTS * LANES, FOX_W)
    return pl.pallas_call(
        _fox_bias_kernel,
        grid=(B, T // tl),
        in_specs=[pl.BlockSpec((None, tl, LANES), lambda b, t: (b, t, 0)), _const_spec(sel.shape)],
        out_specs=pl.BlockSpec((None, tl, FOX_W), lambda b, t: (b, t, 0)),
        out_shape=jax.ShapeDtypeStruct((B, T, FOX_W), BF16),
        scratch_shapes=[pltpu.VMEM((1, LANES), F32)],
        compiler_params=_params("parallel", "arbitrary"),
        name="fox_bias",
    )(logf_pad, sel)


def _fox_prompt_kernel(q_ref, k_ref, kb_ref, vt_ref, o_ref, m_ref, acc_ref,
                       sa_ref, sb_ref, bma_ref, bmb_ref, *, tq, tk):
    i = pl.program_id(2)
    lane = lax.broadcasted_iota(jnp.int32, (1, LANES), 1)
    q = q_ref[...]
    zero = jnp.zeros_like(q)
    qa = []
    for h in range(2):
        own = (lane < DH_A) if h == 0 else (lane >= DH_A)
        ones = jnp.logical_and(lane >= FOX_BIAS_PARTS * h, lane < FOX_BIAS_PARTS * (h + 1))
        ones = jnp.broadcast_to(jnp.where(ones, 1.0, 0.0).astype(q.dtype), q.shape)
        qa.append(jnp.concatenate([jnp.where(own, q, zero), ones], axis=1))
    m_ref[...] = jnp.full_like(m_ref, -jnp.inf)
    acc_ref[...] = jnp.zeros_like(acc_ref)
    sum_rows = jnp.ones((FOX_SUM_ROWS, tk), BF16)
    per = tq // tk

    def scores(j, s_ref, bm_ref):
        start = pl.multiple_of(j * tk, tk)
        ka = jnp.concatenate([k_ref[pl.ds(start, tk), :], kb_ref[pl.ds(start, tk), :]], axis=1)
        for h in range(2):
            st = _dot_nt(ka, qa[h])
            s_ref[h] = st
            bm_ref[h] = jnp.max(st, axis=0, keepdims=True)

    def soft_pv(j, s_ref, bm_ref, diag):
        for h in range(2):
            st = s_ref[h]
            if diag is not None:
                r = lax.broadcasted_iota(jnp.int32, (tk, tq), 0)
                c = lax.broadcasted_iota(jnp.int32, (tk, tq), 1)
                st = jnp.where(r + diag * tk <= c, st, -jnp.inf)
                bm = jnp.max(st, axis=0, keepdims=True)
            else:
                bm = bm_ref[h]
            m_old = m_ref[h]
            m_new = jnp.maximum(m_old, bm)
            p = jnp.exp2(st - m_new)
            a = jnp.exp2(m_old - m_new)
            vt = jnp.concatenate([vt_ref[j, h * DH_A:(h + 1) * DH_A, :], sum_rows], axis=0)
            acc_ref[h] = a * acc_ref[h] + _dot(vt, p.astype(BF16))
            m_ref[h] = m_new

    slots = ((sa_ref, bma_ref), (sb_ref, bmb_ref))

    def run(j0, diags, prefetch_after):
        for u, diag in enumerate(diags):
            if u + 1 < len(diags) or prefetch_after:
                scores(j0 + u + 1, *slots[(u + 1) % 2])
            soft_pv(j0 + u, *slots[u % 2], diag)

    scores(0, *slots[0])

    def body(t, carry):
        run(FOX_UNROLL * t, [None] * FOX_UNROLL, True)
        return carry

    full = i * per
    log_unroll = FOX_UNROLL.bit_length() - 1
    lax.fori_loop(0, lax.shift_right_logical(full, log_unroll), body, 0)
    rest = jnp.bitwise_and(full, FOX_UNROLL - 1)
    for n in range(0, FOX_UNROLL, per):
        @pl.when(rest == n)
        def _(n=n):
            run(full - n, [None] * n + list(range(per)), False)

    ot = jnp.concatenate([acc_ref[h, :DH_A, :] / acc_ref[h, DH_A:DH_A + 1, :] for h in range(2)],
                         axis=0)
    o_ref[...] = ot.T.astype(o_ref.dtype)


def _fox_prompt(qa, ka, kbias, vt, tq, tk):
    B, T, _ = qa.shape
    nk = T // tk
    return pl.pallas_call(
        functools.partial(_fox_prompt_kernel, tq=tq, tk=tk),
        grid=(B, H_A // 2, T // tq),
        in_specs=[pl.BlockSpec((None, tq, LANES), lambda b, p, i: (b, i, p)),
                  pl.BlockSpec((None, T, LANES), lambda b, p, i: (b, 0, p)),
                  pl.BlockSpec((None, T, LANES), lambda b, p, i: (b, 0, p)),
                  pl.BlockSpec((None, None, nk, LANES, tk), lambda b, p, i: (b, p, 0, 0, 0))],
        out_specs=pl.BlockSpec((None, tq, LANES), lambda b, p, i: (b, i, p)),
        out_shape=jax.ShapeDtypeStruct((B, T, FOX_W), BF16),
        scratch_shapes=[pltpu.VMEM((2, 1, tq), F32),
                        pltpu.VMEM((2, DH_A + FOX_SUM_ROWS, tq), F32),
                        pltpu.VMEM((2, tk, tq), F32), pltpu.VMEM((2, tk, tq), F32),
                        pltpu.VMEM((2, 1, tq), F32), pltpu.VMEM((2, 1, tq), F32)],
        compiler_params=_params("parallel", "parallel", "arbitrary"),
        name="fox_prompt",
    )(qa, ka, kbias, vt)


def _fox_sample_kernel(q_ref, k_ref, v_ref, nc_ref, o_ref, *, past):
    first = lax.broadcasted_iota(jnp.int32, (1, LANES), 1) < DH_A
    q = q_ref[...]
    zero = jnp.zeros_like(q)
    tq = q.shape[0]
    kb = k_ref[...]
    vb = v_ref[...]
    lk = kb.shape[0]
    r = lax.broadcasted_iota(jnp.int32, (tq, lk), 0)
    c = lax.broadcasted_iota(jnp.int32, (tq, lk), 1)
    visible = c <= r + past
    outs = []
    for h in range(2):
        qh = jnp.where(first, q, zero) if h == 0 else jnp.where(first, zero, q)
        s = _dot_nt(qh, kb) + nc_ref[h:h + 1, :] * LOG2E
        s = jnp.where(visible, s, -jnp.inf)
        m = jnp.max(s, axis=1, keepdims=True)
        p = jnp.exp2(s - m)
        l = jnp.sum(p, axis=1, keepdims=True)
        outs.append(_dot(p.astype(BF16), vb) / l)
    o_ref[...] = jnp.where(first, outs[0], outs[1]).astype(o_ref.dtype)


def _fox_sample(qa, k_all, v_all, nc, past):
    B, Tq, _ = qa.shape
    Lk = k_all.shape[1]
    return pl.pallas_call(
        functools.partial(_fox_sample_kernel, past=past),
        grid=(B, H_A // 2),
        in_specs=[pl.BlockSpec((None, Tq, LANES), lambda b, p: (b, 0, p)),
                  pl.BlockSpec((None, Lk, LANES), lambda b, p: (b, 0, p)),
                  pl.BlockSpec((None, Lk, LANES), lambda b, p: (b, 0, p)),
                  pl.BlockSpec((None, None, 2, Lk), lambda b, p: (b, p, 0, 0))],
        out_specs=pl.BlockSpec((None, Tq, LANES), lambda b, p: (b, 0, p)),
        out_shape=jax.ShapeDtypeStruct((B, Tq, FOX_W), BF16),
        compiler_params=_params("parallel", "parallel"),
        name="fox_sample",
    )(qa, k_all, v_all, nc)


def _gla_kernel(q_ref, k_ref, v_ref, la_ref, rb_ref, g_ref, s0_ref, hb_ref, sfin_ref, st_ref,
                *, chunk, sub):
    t = pl.program_id(1)

    @pl.when(t == 0)
    def _():
        for h in range(H_B):
            st_ref[h] = s0_ref[h].T

    tc = q_ref.shape[0]
    nsub = chunk // sub
    nchunk = tc // chunk
    r = lax.broadcasted_iota(jnp.int32, (tc, tc), 0)
    c = lax.broadcasted_iota(jnp.int32, (tc, tc), 1)
    log_chunk = chunk.bit_length() - 1
    log_sub = sub.bit_length() - 1
    same_chunk = jnp.right_shift(r, log_chunk) == jnp.right_shift(c, log_chunk)
    near = jnp.logical_and(c <= r, same_chunk)
    far = jnp.right_shift(c, log_chunk) < jnp.right_shift(r, log_chunk)
    tri = jnp.where(c <= r, 1.0, 0.0).astype(BF16)
    hi, mid, lo = _split3(la_ref[...])
    b_all = (_dot(tri, hi) + _dot(tri, mid) + _dot(tri, lo)) * LOG2E

    row = lax.broadcasted_iota(jnp.int32, (tc, 1), 0)
    row_sub = jnp.bitwise_and(jnp.right_shift(row, log_sub), nsub - 1)
    row_chunk = jnp.right_shift(row, log_chunk)

    def rows_of(refs, n):
        return jnp.concatenate([jnp.broadcast_to(rf, (n, DK_B)) for rf in refs], axis=0)

    def zeros(n):
        return jnp.zeros((n, DK_B), F32)

    for h in range(H_B):
        kc = slice(h * DK_B, (h + 1) * DK_B)
        vc = slice(h * DV_B, (h + 1) * DV_B)
        b = b_all[:, kc]
        q = q_ref[:, kc]
        k = k_ref[:, kc]
        v = v_ref[:, vc]
        b_last = b[tc - 1:tc, :]
        st = st_ref[h]
        sub_refs = [b[i * sub:i * sub + 1, :] for i in range(tc // sub)]
        chunk_refs = [b[ci * chunk:ci * chunk + 1, :] for ci in range(nchunk)]

        qt = q * jnp.exp2(b - rows_of(sub_refs, sub))
        qx = [jnp.where(row_sub == i, qt, 0.0) for i in range(nsub)]
        kx = []
        for i in range(nsub):
            pieces = []
            for ci in range(nchunk):
                lo_row, n = ci * chunk, (i + 1) * sub
                pieces.append(k[lo_row:lo_row + n] * jnp.exp2(sub_refs[ci * nsub + i] - b[lo_row:lo_row + n]))
                if n < chunk:
                    pieces.append(zeros(chunk - n))
            kx.append(jnp.concatenate(pieces, axis=0) if len(pieces) > 1 else pieces[0])
        a = jnp.where(near, _dot_nt(jnp.concatenate(qx, axis=1).astype(BF16),
                                    jnp.concatenate(kx, axis=1).astype(BF16)), 0.0)
        if nchunk > 1:
            qc = q * jnp.exp2(b - rows_of(chunk_refs, chunk))
            qx = [jnp.where(row_chunk == ci, qc, 0.0) for ci in range(1, nchunk)]
            kx = []
            for ci in range(1, nchunk):
                n = ci * chunk
                kx.append(jnp.concatenate([k[:n] * jnp.exp2(chunk_refs[ci] - b[:n]), zeros(tc - n)], axis=0))
            a = jnp.where(far, _dot_nt(jnp.concatenate(qx, axis=1).astype(BF16),
                                       jnp.concatenate(kx, axis=1).astype(BF16)), a)
        o = _dot(a.astype(BF16), v) + _dot_nt((q * jnp.exp2(b)).astype(BF16), st.astype(BF16))
        kd = (k * jnp.exp2(b_last - b)).astype(BF16)
        st_ref[h] = st * jnp.exp2(b_last) + _dot_tn(v, kd)
        ms = jnp.mean(o * o, axis=-1, keepdims=True)
        on = o * lax.rsqrt(ms + LN_EPS) * g_ref[:, vc]
        rbv = rb_ref[:, vc]
        hb_ref[:, vc] = (on * (rbv * _sigmoid(rbv))).astype(hb_ref.dtype)

    @pl.when(t == pl.num_programs(1) - 1)
    def _():
        for h in range(H_B):
            sfin_ref[h] = st_ref[h].T


def _gla(qb, kb, vb, la, rb, g, s0_t, tc, chunk):
    B, T, _ = qb.shape
    sub = min(GLA_SUB, chunk)

    def row(n):
        return pl.BlockSpec((None, tc, n), lambda b, t: (b, t, 0))

    st_spec = pl.BlockSpec((None, H_B, DK_B, DV_B), lambda b, t: (b, 0, 0, 0))
    return pl.pallas_call(
        functools.partial(_gla_kernel, chunk=chunk, sub=sub),
        grid=(B, T // tc),
        in_specs=[row(GLA_K), row(GLA_K), row(GLA_V), row(GLA_K), row(GLA_V),
                  _const_spec(g.shape), st_spec],
        out_specs=[row(GLA_V), st_spec],
        out_shape=[jax.ShapeDtypeStruct((B, T, GLA_V), BF16),
                   jax.ShapeDtypeStruct((B, H_B, DK_B, DV_B), F32)],
        scratch_shapes=[pltpu.VMEM((H_B, DV_B, DK_B), F32)],
        compiler_params=_params("parallel", "arbitrary"),
        name="gla",
    )(qb, kb, vb, la, rb, g, s0_t)


def _merge_kernel(oa_ref, hb_ref, ga_ref, gb_ref, x_ref, wa_ref, wb_ref, wo_ref, g_ref, b_ref, x1_ref):
    ya = _dot(oa_ref[...], wa_ref[...])
    yb = _dot(hb_ref[...], wb_ref[...])
    merged = _sigmoid(ga_ref[...].astype(F32)) * ya + _sigmoid(gb_ref[...].astype(F32)) * yb
    z = ALPHA * x_ref[...] + _dot(merged.astype(BF16), wo_ref[...])
    x1_ref[...] = _layer_norm(z, g_ref[...], b_ref[...])


def _merge(oa, hb, ga, gb, x, wa, wb, wo, g, b, tm):
    B, T, D = x.shape

    def row(n):
        return pl.BlockSpec((None, tm, n), lambda bb, t: (bb, t, 0))

    return pl.pallas_call(
        _merge_kernel,
        grid=(B, T // tm),
        in_specs=[row(FOX_W), row(GLA_V), row(D), row(D), row(D),
                  _const_spec(wa.shape), _const_spec(wb.shape), _const_spec(wo.shape),
                  _const_spec(g.shape), _const_spec(b.shape)],
        out_specs=row(D),
        out_shape=jax.ShapeDtypeStruct((B, T, D), F32),
        compiler_params=_params("parallel", "parallel"),
        name="merge",
    )(oa, hb, ga, gb, x, wa, wb, wo, g, b)


FFN_GROUP = 256


def _gelu_tanh(x):
    return 0.5 * x * (1.0 + jnp.tanh(math.sqrt(2.0 / math.pi) * (x + 0.044715 * (x * x * x))))


def _ffn_pack(a):
    return jnp.concatenate([a[..., k * D_FF + j * FFN_GROUP:k * D_FF + (j + 1) * FFN_GROUP]
                            for j in range(D_FF // FFN_GROUP) for k in range(2)], axis=-1)


def _ffn_unpack(a):
    lead = a.shape[:-1]
    return a.reshape(*lead, D_FF // FFN_GROUP, 2, FFN_GROUP).swapaxes(-3, -2).reshape(*lead, 2 * D_FF)


def _ffn_kernel(x1_ref, p_ref, cprev_ref, wup_ref, cw_ref, cb_ref, wdn_ref, g2_ref, b2_ref,
                wpl_ref, wplg_ref, g3_ref, b3_ref, y_ref, cnew_ref, carry_ref):
    @pl.when(pl.program_id(1) == 0)
    def _():
        carry_ref[...] = cprev_ref[...]

    x1 = x1_ref[...]
    tm = x1.shape[0]
    up = _dot(x1.astype(BF16), wup_ref[...])
    prev = carry_ref[...]
    row = lax.broadcasted_iota(jnp.int32, (tm, 1), 0)
    up_m1 = jnp.where(row == 0, prev[1:2, :], pltpu.roll(up, 1, 0))
    up_m2 = jnp.where(row == 0, prev[0:1, :], jnp.where(row == 1, prev[1:2, :], pltpu.roll(up, 2, 0)))
    conv = cb_ref[...] + cw_ref[0:1, :] * up_m2 + cw_ref[1:2, :] * up_m1 + cw_ref[2:3, :] * up
    last2 = up[tm - 2:tm, :]
    carry_ref[...] = last2
    cnew_ref[...] = last2
    hgl = jnp.concatenate(
        [conv[:, 2 * j * FFN_GROUP:(2 * j + 1) * FFN_GROUP]
         * _gelu_tanh(conv[:, (2 * j + 1) * FFN_GROUP:(2 * j + 2) * FFN_GROUP])
         for j in range(D_FF // FFN_GROUP)], axis=1).astype(BF16)
    z2 = ALPHA * x1 + _dot(hgl, wdn_ref[...])
    x2 = _layer_norm(z2, g2_ref[...], b2_ref[...])
    e = _dot(p_ref[...].astype(BF16), wpl_ref[...]) * _sigmoid(_dot(x2.astype(BF16), wplg_ref[...]))
    y_ref[...] = _layer_norm(ALPHA * x2 + e, g3_ref[...], b3_ref[...])


def _ffn(x1, p, cprev, wup, cw, cb, wdn, g2, b2, wpl, wplg, g3, b3, tm):
    B, T, D = x1.shape

    def row(n):
        return pl.BlockSpec((None, tm, n), lambda bb, t: (bb, t, 0))

    cspec = pl.BlockSpec((None, CONV_W - 1, 2 * D_FF), lambda bb, t: (bb, 0, 0))
    consts = (wup, cw, cb, wdn, g2, b2, wpl, wplg, g3, b3)
    return pl.pallas_call(
        _ffn_kernel,
        grid=(B, T // tm),
        in_specs=[row(D), row(PLE_DIM), cspec] + [_const_spec(c.shape) for c in consts],
        out_specs=[row(D), cspec],
        out_shape=[jax.ShapeDtypeStruct((B, T, D), F32),
                   jax.ShapeDtypeStruct((B, CONV_W - 1, 2 * D_FF), F32)],
        scratch_shapes=[pltpu.VMEM((CONV_W - 1, 2 * D_FF), F32)],
        compiler_params=_params("parallel", "arbitrary"),
        name="ffn",
    )(x1, p, cprev, *consts)


def _pick(T, want):
    t = min(T, want)
    while T % t:
        t //= 2
    return t


def _pack_w_in(w_in):
    pts, acc = {}, 0
    for name, n in (("qa", FOX_W), ("ka", FOX_W), ("va", FOX_W), ("fa", H_A), ("qb", GLA_K), ("kb", GLA_K),
                    ("vb", GLA_V), ("rb", GLA_V), ("a1", GLA_RANK), ("ga", D_MODEL), ("gb", D_MODEL)):
        pts[name] = (acc, n)
        acc += n
    cols = []
    for name in _SEG:
        lo, n = pts[name]
        piece = w_in[:, lo:lo + n]
        if n < _SEG[name][1]:
            piece = jnp.pad(piece, ((0, 0), (0, _SEG[name][1] - n)))
        cols.append(piece)
    return jnp.concatenate(cols, axis=1).astype(BF16)


def _layer(x, p, fox_past, s0, conv_prev, wts):
    (w_in, b_fgate, w_a2, b_a2, g_gla, w_a_out, w_b_out, w_o, ln1_g, ln1_b, w_up, conv_w, conv_b,
     w_down, ln2_g, ln2_b, w_pl, w_plg, ln3_g, ln3_b) = wts
    B, T, D = x.shape
    row2 = lambda a: a.reshape(1, -1).astype(F32)

    wcat = _pack_w_in(w_in)
    bf = jnp.pad(row2(b_fgate), ((0, 0), (0, LANES - H_A)))
    wa2 = jnp.pad(w_a2.astype(F32), ((0, LANES - GLA_RANK), (0, 0)))
    wa2h = wa2.astype(BF16)
    wa2l = (wa2 - wa2h.astype(F32)).astype(BF16)

    tq, tk = _pick(T, 512), _pick(T, 512)
    fold = fox_past is not None and T < LANES
    rows = (lambda a: a.reshape(1, B * T, a.shape[-1])) if fold else (lambda a: a)
    unrows = (lambda a: a.reshape(B, T, a.shape[-1])) if fold else (lambda a: a)
    proj_out = _proj(rows(x), wcat, bf, wa2h, wa2l, row2(b_a2), _pick(B * T if fold else T, 512),
                     vt_block=tk if fox_past is None else None)
    proj_out = list(proj_out)
    proj_out[5] = jnp.transpose(proj_out[5], (0, 2, 1))
    if fold:
        proj_out = [unrows(o) for o in proj_out]
    (qa, ka_f, ka_b, va_f, va_b, logf, logf_pad, qb, kb, vb, rb, la, ga, gb) = proj_out

    if fox_past is None:
        kbias = _fox_bias(logf_pad, _pick(T, 512))
        oa = _fox_prompt(qa, ka_b, kbias, va_b, tq, tk)
    else:
        past_k, past_v, past_logf = fox_past
        P = past_k.shape[1]
        lk = -(-(P + T) // FOX_KEY_PAD) * FOX_KEY_PAD
        padk = lambda a: jnp.pad(a, ((0, 0), (0, lk - P - T), (0, 0)))
        k_all = padk(jnp.concatenate([past_k.reshape(B, P, FOX_W).astype(BF16), ka_b], axis=1))
        v_all = padk(jnp.concatenate([past_v.reshape(B, P, FOX_W).astype(BF16), va_b], axis=1))
        lf_all = padk(jnp.concatenate([past_logf.astype(F32), logf], axis=1))
        nc = _neg_cumsum(jnp.transpose(lf_all, (0, 2, 1)), _pick(lk, 512))
        oa = _fox_sample(qa, k_all, v_all, nc.reshape(B, H_A // 2, 2, lk), P)

    chunk = min(GLA_CHUNK, T)
    tc = _pick(T, 4 * chunk)
    hb, s_fin = _gla(qb, kb, vb, la, rb, row2(g_gla), s0.astype(F32), tc, chunk)

    x1 = unrows(_merge(rows(oa), rows(hb), rows(ga), rows(gb), rows(x), w_a_out.astype(BF16),
                       w_b_out.astype(BF16), w_o.astype(BF16), row2(ln1_g), row2(ln1_b),
                       _pick(B * T if fold else T, 512)))
    y, conv_new = _ffn(x1, p, _ffn_pack(conv_prev.astype(F32)), _ffn_pack(w_up.astype(BF16)),
                       _ffn_pack(conv_w.astype(F32)), _ffn_pack(row2(conv_b)),
                       w_down.astype(BF16), row2(ln2_g), row2(ln2_b), w_pl.astype(BF16),
                       w_plg.astype(BF16), row2(ln3_g), row2(ln3_b), _pick(T, 512))
    conv_new = _ffn_unpack(conv_new)
    k_out = ka_f.reshape(B, T, H_A, DH_A)
    v_out = va_f.reshape(B, T, H_A, DH_A)
    return y, k_out, v_out, logf, s_fin, conv_new


def kernel(x_prompt, x_sample, cache_fox_k, cache_fox_v, cache_fox_logf, state_gla, cache_ffn_conv, p_prompt, p_sample, w_in, b_fgate, w_a2, b_a2, g_gla, w_a_out, w_b_out, w_o, ln1_g, ln1_b, w_up, conv_w, conv_b, w_down, ln2_g, ln2_b, w_pl, w_plg, ln3_g, ln3_b):
    hp, hs = x_prompt, x_sample
    outs_p = [[] for _ in range(5)]
    outs_s = [[] for _ in range(5)]
    for i in range(DEPTH):
        wts = (w_in[i], b_fgate[i], w_a2[i], b_a2[i], g_gla[i], w_a_out[i], w_b_out[i], w_o[i],
               ln1_g[i], ln1_b[i], w_up[i], conv_w[i], conv_b[i], w_down[i], ln2_g[i], ln2_b[i],
               w_pl[i], w_plg[i], ln3_g[i], ln3_b[i])
        Bp = hp.shape[0]
        s0_p = jnp.zeros((Bp, H_B, DK_B, DV_B), F32)
        c0_p = jnp.zeros((Bp, CONV_W - 1, 2 * D_FF), F32)
        hp, *rest_p = _layer(hp, p_prompt[i], None, s0_p, c0_p, wts)
        hs, *rest_s = _layer(hs, p_sample[i], (cache_fox_k[i], cache_fox_v[i], cache_fox_logf[i]),
                             state_gla[i], cache_ffn_conv[i], wts)
        for dst, src in ((outs_p, rest_p), (outs_s, rest_s)):
            for lst, val in zip(dst, src):
                lst.append(val)
    return (hp, hs, *[jnp.stack(l) for l in outs_p], *[jnp.stack(l) for l in outs_s])
```

```python
import functools
import math

import jax
import jax.numpy as jnp
from jax import lax
from jax.experimental import pallas as pl
from jax.experimental.pallas import tpu as pltpu

F32 = jnp.float32
BF16 = jnp.bfloat16

D_MODEL = 1024
H_A, DH_A = 8, 64
FOX_W = H_A * DH_A
H_B, DK_B, DV_B = 4, 128, 256
GLA_K = H_B * DK_B
GLA_V = H_B * DV_B
GLA_RANK = 16
GLA_TAU = 16
D_FF = 2816
CONV_W = 3
PLE_DIM = 256
LN_EPS = 1e-5
DEPTH = 1
ALPHA = (2 * DEPTH) ** 0.25
GLA_CHUNK = 64
GLA_SUB = 16

LANES = 128
LOG2E = math.log2(math.e)
V7X_VMEM_LIMIT_BYTES = 56 * 1024 * 1024

_SEG = {}
_off = 0
for _name, _n in (("qa", FOX_W), ("ka", FOX_W), ("va", FOX_W), ("qb", GLA_K), ("kb", GLA_K),
                  ("vb", GLA_V), ("rb", GLA_V), ("ga", D_MODEL), ("gb", D_MODEL),
                  ("fa", LANES), ("a1", LANES)):
    _SEG[_name] = (_off, _n)
    _off += _n
W_CAT = _off


def _params(*sem):
    return pltpu.CompilerParams(dimension_semantics=sem, vmem_limit_bytes=V7X_VMEM_LIMIT_BYTES)


def _const_spec(shape):
    nd = len(shape)
    return pl.BlockSpec(shape, lambda *_: (0,) * nd, pipeline_mode=pl.Buffered(1))


def _log_sigmoid(z):
    return jnp.minimum(z, 0.0) - jnp.log(1.0 + jnp.exp(-jnp.abs(z)))


def _sigmoid(z):
    return 1.0 / (1.0 + jnp.exp(-z))


def _split3(a):
    hi = a.astype(BF16)
    r = a - hi.astype(F32)
    mid = r.astype(BF16)
    lo = (r - mid.astype(F32)).astype(BF16)
    return hi, mid, lo


def _layer_norm(z, g, b):
    mu = jnp.mean(z, axis=-1, keepdims=True)
    zc = z - mu
    var = jnp.mean(zc * zc, axis=-1, keepdims=True)
    return zc * lax.rsqrt(var + LN_EPS) * g + b


def _dot(a, b):
    return jnp.dot(a, b, preferred_element_type=F32)


def _dot_nt(a, b):
    return lax.dot_general(a, b, (((1,), (1,)), ((), ())), preferred_element_type=F32)


def _dot_tn(a, b):
    return lax.dot_general(a, b, (((0,), (0,)), ((), ())), preferred_element_type=F32)


def _proj_kernel(x_ref, w_ref, bf_ref, wa2h_ref, wa2l_ref, ba2_ref,
                 qa_ref, kaf_ref, kab_ref, vaf_ref, vab_ref, logf_ref, logfp_ref,
                 qb_ref, kb_ref, vb_ref, rb_ref, la_ref, ga_ref, gb_ref, *, values_transposed):
    xb = x_ref[...].astype(BF16)

    def seg(name):
        lo, n = _SEG[name]
        return _dot(xb, w_ref[:, lo:lo + n])

    qa_ref[...] = (seg("qa") * (DH_A ** -0.5 * LOG2E)).astype(BF16)
    ka = seg("ka")
    kaf_ref[...] = ka
    kab_ref[...] = ka.astype(BF16)
    va = seg("va")
    vaf_ref[...] = va
    if values_transposed:
        for p in range(H_A // 2):
            vab_ref[p] = va[:, p * LANES:(p + 1) * LANES].T.astype(BF16)
    else:
        vab_ref[...] = va.astype(BF16)
    qb_ref[...] = seg("qb") * (DK_B ** -0.5)
    kb_ref[...] = seg("kb")
    vb_ref[...] = seg("vb").astype(BF16)
    rb_ref[...] = seg("rb")
    ga_ref[...] = seg("ga").astype(BF16)
    gb_ref[...] = seg("gb").astype(BF16)
    logf = _log_sigmoid(seg("fa") + bf_ref[...])
    logf_ref[...] = logf.T[:H_A, :]
    logfp_ref[...] = logf
    a1 = seg("a1")
    a1h = a1.astype(BF16)
    a1l = (a1 - a1h.astype(F32)).astype(BF16)
    z = _dot(a1h, wa2h_ref[...]) + _dot(a1l, wa2h_ref[...]) + _dot(a1h, wa2l_ref[...]) + ba2_ref[...]
    la_ref[...] = _log_sigmoid(z) * (1.0 / GLA_TAU)


def _proj(x, wcat, bf, wa2h, wa2l, ba2, tm, vt_block=None):
    B, T, D = x.shape
    grid = (B, T // tm)

    def row(n, dt):
        return pl.BlockSpec((None, tm, n), lambda b, t: (b, t, 0)), jax.ShapeDtypeStruct((B, T, n), dt)

    outs = [row(FOX_W, BF16), row(FOX_W, F32), row(FOX_W, BF16), row(FOX_W, F32), row(FOX_W, BF16),
            row(H_A, F32), row(LANES, F32), row(GLA_K, F32), row(GLA_K, F32), row(GLA_V, BF16), row(GLA_V, F32),
            row(GLA_K, F32), row(D_MODEL, BF16), row(D_MODEL, BF16)]
    outs[5] = (pl.BlockSpec((None, H_A, tm), lambda b, t: (b, 0, t)), jax.ShapeDtypeStruct((B, H_A, T), F32))
    if vt_block is not None:
        per = vt_block // tm
        outs[4] = (pl.BlockSpec((None, H_A // 2, None, LANES, tm), lambda b, t: (b, 0, t // per, 0, t % per)),
                   jax.ShapeDtypeStruct((B, H_A // 2, T // vt_block, LANES, vt_block), BF16))
    return pl.pallas_call(
        functools.partial(_proj_kernel, values_transposed=vt_block is not None),
        grid=grid,
        in_specs=[pl.BlockSpec((None, tm, D), lambda b, t: (b, t, 0)),
                  _const_spec(wcat.shape), _const_spec(bf.shape), _const_spec(wa2h.shape),
                  _const_spec(wa2l.shape), _const_spec(ba2.shape)],
        out_specs=[o[0] for o in outs],
        out_shape=[o[1] for o in outs],
        compiler_params=_params("parallel", "parallel"),
        name="proj",
    )(x, wcat, bf, wa2h, wa2l, ba2)


def _cumsum_kernel(lf_ref, nc_ref, carry_ref):
    @pl.when(pl.program_id(1) == 0)
    def _():
        carry_ref[...] = jnp.zeros_like(carry_ref)

    x = lf_ref[...]
    tl = x.shape[1]
    r = lax.broadcasted_iota(jnp.int32, (tl, tl), 0)
    c = lax.broadcasted_iota(jnp.int32, (tl, tl), 1)
    tri = jnp.where(r <= c, 1.0, 0.0).astype(BF16)
    hi, mid, lo = _split3(x)
    cs = _dot(hi, tri) + _dot(mid, tri) + _dot(lo, tri) + carry_ref[...]
    nc_ref[...] = -cs
    carry_ref[...] = carry_ref[...] + jnp.sum(x, axis=1, keepdims=True)


def _neg_cumsum(lf_t, tl):
    B, H, L = lf_t.shape
    return pl.pallas_call(
        _cumsum_kernel,
        grid=(B, L // tl),
        in_specs=[pl.BlockSpec((None, H, tl), lambda b, t: (b, 0, t))],
        out_specs=pl.BlockSpec((None, H, tl), lambda b, t: (b, 0, t)),
        out_shape=jax.ShapeDtypeStruct((B, H, L), F32),
        scratch_shapes=[pltpu.VMEM((H, 1), F32)],
        compiler_params=_params("parallel", "arbitrary"),
        name="cumsum",
    )(lf_t)


FOX_KEY_PAD = 512
FOX_BIAS_PARTS = 3
FOX_UNROLL = 4
FOX_ROW_SKEW = LANES
FOX_SUM_ROWS = 16


def _fox_bias_sel():
    sel = [[[0.0] * FOX_W for _ in range(LANES)] for _ in range(FOX_BIAS_PARTS)]
    for h in range(H_A):
        for j in range(FOX_BIAS_PARTS):
            sel[j][h][(h // 2) * LANES + FOX_BIAS_PARTS * (h % 2) + j] = 1.0
    return jnp.asarray(sel, dtype=BF16)


def _fox_bias_kernel(lf_ref, sel_ref, kb_ref, carry_ref):
    @pl.when(pl.program_id(1) == 0)
    def _():
        carry_ref[...] = jnp.zeros_like(carry_ref)

    x = lf_ref[...]
    tl = x.shape[0]
    r = lax.broadcasted_iota(jnp.int32, (tl, tl), 0)
    c = lax.broadcasted_iota(jnp.int32, (tl, tl), 1)
    tri = jnp.where(c <= r, 1.0, 0.0).astype(BF16)
    hi, mid, lo = _split3(x)
    cs = _dot(tri, hi) + _dot(tri, mid) + _dot(tri, lo) + carry_ref[...]
    carry_ref[...] = cs[tl - 1:tl, :]
    parts = _split3(cs * (-LOG2E))
    kb_ref[...] = (_dot(parts[0], sel_ref[0]) + _dot(parts[1], sel_ref[1])
                   + _dot(parts[2], sel_ref[2])).astype(kb_ref.dtype)


def _fox_bias(logf_pad, tl):
    B, T, _ = logf_pad.shape
    sel = _fox_bias_sel()
    return pl.pallas_call(
        _fox_bias_kernel,
        grid=(B, T // tl),
        in_specs=[pl.BlockSpec((None, tl, LANES), lambda b, t: (b, t, 0)), _const_spec(sel.shape)],
        out_specs=pl.BlockSpec((None, tl, FOX_W), lambda b, t: (b, t, 0)),
        out_shape=jax.ShapeDtypeStruct((B, T, FOX_W), BF16),
        scratch_shapes=[pltpu.VMEM((1, LANES), F32)],
        compiler_params=_params("parallel", "arbitrary"),
        name="fox_bias",
    )(logf_pad, sel)


def _fox_prompt_kernel(q_ref, k_ref, kb_ref, vt_ref, o_ref, m_ref, acc_ref,
                       sa_ref, sb_ref, bma_ref, bmb_ref, *, tq, tk):
    i = pl.program_id(2)
    lane = lax.broadcasted_iota(jnp.int32, (1, LANES), 1)
    q = q_ref[...]
    zero = jnp.zeros_like(q)
    qa = []
    for h in range(2):
        own = (lane < DH_A) if h == 0 else (lane >= DH_A)
        ones = jnp.logical_and(lane >= FOX_BIAS_PARTS * h, lane < FOX_BIAS_PARTS * (h + 1))
        ones = jnp.broadcast_to(jnp.where(ones, 1.0, 0.0).astype(q.dtype), q.shape)
        qa.append(jnp.concatenate([jnp.where(own, q, zero), ones], axis=1))
    m_ref[...] = jnp.full_like(m_ref, -jnp.inf)
    acc_ref[...] = jnp.zeros_like(acc_ref)
    sum_rows = jnp.ones((FOX_SUM_ROWS, tk), BF16)
    per = tq // tk

    def scores(j, s_ref, bm_ref):
        start = pl.multiple_of(j * tk, tk)
        ka = jnp.concatenate([k_ref[pl.ds(start, tk), :], kb_ref[pl.ds(start, tk), :]], axis=1)
        for h in range(2):
            st = _dot_nt(ka, qa[h])
            s_ref[h, :, :tq] = st
            bm_ref[h] = jnp.max(st, axis=0, keepdims=True)

    def soft_pv(j, s_ref, bm_ref, diag):
        for h in range(2):
            st = s_ref[h, :, :tq]
            if diag is not None:
                r = lax.broadcasted_iota(jnp.int32, (tk, tq), 0)
                c = lax.broadcasted_iota(jnp.int32, (tk, tq), 1)
                st = jnp.where(r + diag * tk <= c, st, -jnp.inf)
                bm = jnp.max(st, axis=0, keepdims=True)
            else:
                bm = bm_ref[h]
            m_old = m_ref[h]
            m_new = jnp.maximum(m_old, bm)
            p = jnp.exp2(st - m_new)
            a = jnp.exp2(m_old - m_new)
            vt = jnp.concatenate([vt_ref[j, h * DH_A:(h + 1) * DH_A, :], sum_rows], axis=0)
            acc_ref[h] = a * acc_ref[h] + _dot(vt, p.astype(BF16))
            m_ref[h] = m_new

    slots = ((sa_ref, bma_ref), (sb_ref, bmb_ref))

    def run(j0, diags, prefetch_after):
        for u, diag in enumerate(diags):
            if u + 1 < len(diags) or prefetch_after:
                scores(j0 + u + 1, *slots[(u + 1) % 2])
            soft_pv(j0 + u, *slots[u % 2], diag)

    scores(0, *slots[0])

    def body(t, carry):
        run(FOX_UNROLL * t, [None] * FOX_UNROLL, True)
        return carry

    full = i * per
    log_unroll = FOX_UNROLL.bit_length() - 1
    lax.fori_loop(0, lax.shift_right_logical(full, log_unroll), body, 0)
    rest = jnp.bitwise_and(full, FOX_UNROLL - 1)
    for n in range(0, FOX_UNROLL, per):
        @pl.when(rest == n)
        def _(n=n):
            run(full - n, [None] * n + list(range(per)), False)

    ot = jnp.concatenate([acc_ref[h, :DH_A, :] / acc_ref[h, DH_A:DH_A + 1, :] for h in range(2)],
                         axis=0)
    o_ref[...] = ot.T.astype(o_ref.dtype)


def _fox_prompt(qa, ka, kbias, vt, tq, tk):
    B, T, _ = qa.shape
    nk = T // tk
    return pl.pallas_call(
        functools.partial(_fox_prompt_kernel, tq=tq, tk=tk),
        grid=(B, H_A // 2, T // tq),
        in_specs=[pl.BlockSpec((None, tq, LANES), lambda b, p, i: (b, i, p)),
                  pl.BlockSpec((None, T, LANES), lambda b, p, i: (b, 0, p)),
                  pl.BlockSpec((None, T, LANES), lambda b, p, i: (b, 0, p)),
                  pl.BlockSpec((None, None, nk, LANES, tk), lambda b, p, i: (b, p, 0, 0, 0))],
        out_specs=pl.BlockSpec((None, tq, LANES), lambda b, p, i: (b, i, p)),
        out_shape=jax.ShapeDtypeStruct((B, T, FOX_W), BF16),
        scratch_shapes=[pltpu.VMEM((2, 1, tq), F32),
                        pltpu.VMEM((2, DH_A + FOX_SUM_ROWS, tq), F32),
                        pltpu.VMEM((2, tk, tq + FOX_ROW_SKEW), F32),
                        pltpu.VMEM((2, tk, tq + FOX_ROW_SKEW), F32),
                        pltpu.VMEM((2, 1, tq), F32), pltpu.VMEM((2, 1, tq), F32)],
        compiler_params=_params("parallel", "parallel", "arbitrary"),
        name="fox_prompt",
    )(qa, ka, kbias, vt)


def _fox_sample_kernel(q_ref, k_ref, v_ref, nc_ref, o_ref, *, past):
    first = lax.broadcasted_iota(jnp.int32, (1, LANES), 1) < DH_A
    q = q_ref[...]
    zero = jnp.zeros_like(q)
    tq = q.shape[0]
    kb = k_ref[...]
    vb = v_ref[...]
    lk = kb.shape[0]
    r = lax.broadcasted_iota(jnp.int32, (tq, lk), 0)
    c = lax.broadcasted_iota(jnp.int32, (tq, lk), 1)
    visible = c <= r + past
    outs = []
    for h in range(2):
        qh = jnp.where(first, q, zero) if h == 0 else jnp.where(first, zero, q)
        s = _dot_nt(qh, kb) + nc_ref[h:h + 1, :] * LOG2E
        s = jnp.where(visible, s, -jnp.inf)
        m = jnp.max(s, axis=1, keepdims=True)
        p = jnp.exp2(s - m)
        l = jnp.sum(p, axis=1, keepdims=True)
        outs.append(_dot(p.astype(BF16), vb) / l)
    o_ref[...] = jnp.where(first, outs[0], outs[1]).astype(o_ref.dtype)


def _fox_sample(qa, k_all, v_all, nc, past):
    B, Tq, _ = qa.shape
    Lk = k_all.shape[1]
    return pl.pallas_call(
        functools.partial(_fox_sample_kernel, past=past),
        grid=(B, H_A // 2),
        in_specs=[pl.BlockSpec((None, Tq, LANES), lambda b, p: (b, 0, p)),
                  pl.BlockSpec((None, Lk, LANES), lambda b, p: (b, 0, p)),
                  pl.BlockSpec((None, Lk, LANES), lambda b, p: (b, 0, p)),
                  pl.BlockSpec((None, None, 2, Lk), lambda b, p: (b, p, 0, 0))],
        out_specs=pl.BlockSpec((None, Tq, LANES), lambda b, p: (b, 0, p)),
        out_shape=jax.ShapeDtypeStruct((B, Tq, FOX_W), BF16),
        compiler_params=_params("parallel", "parallel"),
        name="fox_sample",
    )(qa, k_all, v_all, nc)


def _gla_kernel(q_ref, k_ref, v_ref, la_ref, rb_ref, g_ref, s0_ref, hb_ref, sfin_ref, st_ref,
                *, chunk, sub):
    t = pl.program_id(1)

    @pl.when(t == 0)
    def _():
        for h in range(H_B):
            st_ref[h] = s0_ref[h].T

    tc = q_ref.shape[0]
    nsub = chunk // sub
    nchunk = tc // chunk
    r = lax.broadcasted_iota(jnp.int32, (tc, tc), 0)
    c = lax.broadcasted_iota(jnp.int32, (tc, tc), 1)
    log_chunk = chunk.bit_length() - 1
    log_sub = sub.bit_length() - 1
    same_chunk = jnp.right_shift(r, log_chunk) == jnp.right_shift(c, log_chunk)
    near = jnp.logical_and(c <= r, same_chunk)
    far = jnp.right_shift(c, log_chunk) < jnp.right_shift(r, log_chunk)
    tri = jnp.where(c <= r, 1.0, 0.0).astype(BF16)
    hi, mid, lo = _split3(la_ref[...])
    b_all = (_dot(tri, hi) + _dot(tri, mid) + _dot(tri, lo)) * LOG2E

    row = lax.broadcasted_iota(jnp.int32, (tc, 1), 0)
    row_sub = jnp.bitwise_and(jnp.right_shift(row, log_sub), nsub - 1)
    row_chunk = jnp.right_shift(row, log_chunk)

    def rows_of(refs, n):
        return jnp.concatenate([jnp.broadcast_to(rf, (n, DK_B)) for rf in refs], axis=0)

    def zeros(n):
        return jnp.zeros((n, DK_B), F32)

    for h in range(H_B):
        kc = slice(h * DK_B, (h + 1) * DK_B)
        vc = slice(h * DV_B, (h + 1) * DV_B)
        b = b_all[:, kc]
        q = q_ref[:, kc]
        k = k_ref[:, kc]
        v = v_ref[:, vc]
        b_last = b[tc - 1:tc, :]
        st = st_ref[h]
        sub_refs = [b[i * sub:i * sub + 1, :] for i in range(tc // sub)]
        chunk_refs = [b[ci * chunk:ci * chunk + 1, :] for ci in range(nchunk)]

        qt = q * jnp.exp2(b - rows_of(sub_refs, sub))
        qx = [jnp.where(row_sub == i, qt, 0.0) for i in range(nsub)]
        kx = []
        for i in range(nsub):
            pieces = []
            for ci in range(nchunk):
                lo_row, n = ci * chunk, (i + 1) * sub
                pieces.append(k[lo_row:lo_row + n] * jnp.exp2(sub_refs[ci * nsub + i] - b[lo_row:lo_row + n]))
                if n < chunk:
                    pieces.append(zeros(chunk - n))
            kx.append(jnp.concatenate(pieces, axis=0) if len(pieces) > 1 else pieces[0])
        a = jnp.where(near, _dot_nt(jnp.concatenate(qx, axis=1).astype(BF16),
                                    jnp.concatenate(kx, axis=1).astype(BF16)), 0.0)
        if nchunk > 1:
            qc = q * jnp.exp2(b - rows_of(chunk_refs, chunk))
            qx = [jnp.where(row_chunk == ci, qc, 0.0) for ci in range(1, nchunk)]
            kx = []
            for ci in range(1, nchunk):
                n = ci * chunk
                kx.append(jnp.concatenate([k[:n] * jnp.exp2(chunk_refs[ci] - b[:n]), zeros(tc - n)], axis=0))
            a = jnp.where(far, _dot_nt(jnp.concatenate(qx, axis=1).astype(BF16),
                                       jnp.concatenate(kx, axis=1).astype(BF16)), a)
        o = _dot(a.astype(BF16), v) + _dot_nt((q * jnp.exp2(b)).astype(BF16), st.astype(BF16))
        kd = (k * jnp.exp2(b_last - b)).astype(BF16)
        st_ref[h] = st * jnp.exp2(b_last) + _dot_tn(v, kd)
        ms = jnp.mean(o * o, axis=-1, keepdims=True)
        on = o * lax.rsqrt(ms + LN_EPS) * g_ref[:, vc]
        rbv = rb_ref[:, vc]
        hb_ref[:, vc] = (on * (rbv * _sigmoid(rbv))).astype(hb_ref.dtype)

    @pl.when(t == pl.num_programs(1) - 1)
    def _():
        for h in range(H_B):
            sfin_ref[h] = st_ref[h].T


def _gla(qb, kb, vb, la, rb, g, s0_t, tc, chunk):
    B, T, _ = qb.shape
    sub = min(GLA_SUB, chunk)

    def row(n):
        return pl.BlockSpec((None, tc, n), lambda b, t: (b, t, 0))

    st_spec = pl.BlockSpec((None, H_B, DK_B, DV_B), lambda b, t: (b, 0, 0, 0))
    return pl.pallas_call(
        functools.partial(_gla_kernel, chunk=chunk, sub=sub),
        grid=(B, T // tc),
        in_specs=[row(GLA_K), row(GLA_K), row(GLA_V), row(GLA_K), row(GLA_V),
                  _const_spec(g.shape), st_spec],
        out_specs=[row(GLA_V), st_spec],
        out_shape=[jax.ShapeDtypeStruct((B, T, GLA_V), BF16),
                   jax.ShapeDtypeStruct((B, H_B, DK_B, DV_B), F32)],
        scratch_shapes=[pltpu.VMEM((H_B, DV_B, DK_B), F32)],
        compiler_params=_params("parallel", "arbitrary"),
        name="gla",
    )(qb, kb, vb, la, rb, g, s0_t)


def _merge_kernel(oa_ref, hb_ref, ga_ref, gb_ref, x_ref, wa_ref, wb_ref, wo_ref, g_ref, b_ref, x1_ref):
    ya = _dot(oa_ref[...], wa_ref[...])
    yb = _dot(hb_ref[...], wb_ref[...])
    merged = _sigmoid(ga_ref[...].astype(F32)) * ya + _sigmoid(gb_ref[...].astype(F32)) * yb
    z = ALPHA * x_ref[...] + _dot(merged.astype(BF16), wo_ref[...])
    x1_ref[...] = _layer_norm(z, g_ref[...], b_ref[...])


def _merge(oa, hb, ga, gb, x, wa, wb, wo, g, b, tm):
    B, T, D = x.shape

    def row(n):
        return pl.BlockSpec((None, tm, n), lambda bb, t: (bb, t, 0))

    return pl.pallas_call(
        _merge_kernel,
        grid=(B, T // tm),
        in_specs=[row(FOX_W), row(GLA_V), row(D), row(D), row(D),
                  _const_spec(wa.shape), _const_spec(wb.shape), _const_spec(wo.shape),
                  _const_spec(g.shape), _const_spec(b.shape)],
        out_specs=row(D),
        out_shape=jax.ShapeDtypeStruct((B, T, D), F32),
        compiler_params=_params("parallel", "parallel"),
        name="merge",
    )(oa, hb, ga, gb, x, wa, wb, wo, g, b)


FFN_GROUP = 256


def _gelu_tanh(x):
    return 0.5 * x * (1.0 + jnp.tanh(math.sqrt(2.0 / math.pi) * (x + 0.044715 * (x * x * x))))


def _ffn_pack(a):
    return jnp.concatenate([a[..., k * D_FF + j * FFN_GROUP:k * D_FF + (j + 1) * FFN_GROUP]
                            for j in range(D_FF // FFN_GROUP) for k in range(2)], axis=-1)


def _ffn_unpack(a):
    lead = a.shape[:-1]
    return a.reshape(*lead, D_FF // FFN_GROUP, 2, FFN_GROUP).swapaxes(-3, -2).reshape(*lead, 2 * D_FF)


def _ffn_kernel(x1_ref, p_ref, cprev_ref, wup_ref, cw_ref, cb_ref, wdn_ref, g2_ref, b2_ref,
                wpl_ref, wplg_ref, g3_ref, b3_ref, y_ref, cnew_ref, carry_ref):
    @pl.when(pl.program_id(1) == 0)
    def _():
        carry_ref[...] = cprev_ref[...]

    x1 = x1_ref[...]
    tm = x1.shape[0]
    up = _dot(x1.astype(BF16), wup_ref[...])
    prev = carry_ref[...]
    row = lax.broadcasted_iota(jnp.int32, (tm, 1), 0)
    up_m1 = jnp.where(row == 0, prev[1:2, :], pltpu.roll(up, 1, 0))
    up_m2 = jnp.where(row == 0, prev[0:1, :], jnp.where(row == 1, prev[1:2, :], pltpu.roll(up, 2, 0)))
    conv = cb_ref[...] + cw_ref[0:1, :] * up_m2 + cw_ref[1:2, :] * up_m1 + cw_ref[2:3, :] * up
    last2 = up[tm - 2:tm, :]
    carry_ref[...] = last2
    cnew_ref[...] = last2
    hgl = jnp.concatenate(
        [conv[:, 2 * j * FFN_GROUP:(2 * j + 1) * FFN_GROUP]
         * _gelu_tanh(conv[:, (2 * j + 1) * FFN_GROUP:(2 * j + 2) * FFN_GROUP])
         for j in range(D_FF // FFN_GROUP)], axis=1).astype(BF16)
    z2 = ALPHA * x1 + _dot(hgl, wdn_ref[...])
    x2 = _layer_norm(z2, g2_ref[...], b2_ref[...])
    e = _dot(p_ref[...].astype(BF16), wpl_ref[...]) * _sigmoid(_dot(x2.astype(BF16), wplg_ref[...]))
    y_ref[...] = _layer_norm(ALPHA * x2 + e, g3_ref[...], b3_ref[...])


def _ffn(x1, p, cprev, wup, cw, cb, wdn, g2, b2, wpl, wplg, g3, b3, tm):
    B, T, D = x1.shape

    def row(n):
        return pl.BlockSpec((None, tm, n), lambda bb, t: (bb, t, 0))

    cspec = pl.BlockSpec((None, CONV_W - 1, 2 * D_FF), lambda bb, t: (bb, 0, 0))
    consts = (wup, cw, cb, wdn, g2, b2, wpl, wplg, g3, b3)
    return pl.pallas_call(
        _ffn_kernel,
        grid=(B, T // tm),
        in_specs=[row(D), row(PLE_DIM), cspec] + [_const_spec(c.shape) for c in consts],
        out_specs=[row(D), cspec],
        out_shape=[jax.ShapeDtypeStruct((B, T, D), F32),
                   jax.ShapeDtypeStruct((B, CONV_W - 1, 2 * D_FF), F32)],
        scratch_shapes=[pltpu.VMEM((CONV_W - 1, 2 * D_FF), F32)],
        compiler_params=_params("parallel", "arbitrary"),
        name="ffn",
    )(x1, p, cprev, *consts)


def _pick(T, want):
    t = min(T, want)
    while T % t:
        t //= 2
    return t


def _pack_w_in(w_in):
    pts, acc = {}, 0
    for name, n in (("qa", FOX_W), ("ka", FOX_W), ("va", FOX_W), ("fa", H_A), ("qb", GLA_K), ("kb", GLA_K),
                    ("vb", GLA_V), ("rb", GLA_V), ("a1", GLA_RANK), ("ga", D_MODEL), ("gb", D_MODEL)):
        pts[name] = (acc, n)
        acc += n
    cols = []
    for name in _SEG:
        lo, n = pts[name]
        piece = w_in[:, lo:lo + n]
        if n < _SEG[name][1]:
            piece = jnp.pad(piece, ((0, 0), (0, _SEG[name][1] - n)))
        cols.append(piece)
    return jnp.concatenate(cols, axis=1).astype(BF16)


def _layer(x, p, fox_past, s0, conv_prev, wts):
    (w_in, b_fgate, w_a2, b_a2, g_gla, w_a_out, w_b_out, w_o, ln1_g, ln1_b, w_up, conv_w, conv_b,
     w_down, ln2_g, ln2_b, w_pl, w_plg, ln3_g, ln3_b) = wts
    B, T, D = x.shape
    row2 = lambda a: a.reshape(1, -1).astype(F32)

    wcat = _pack_w_in(w_in)
    bf = jnp.pad(row2(b_fgate), ((0, 0), (0, LANES - H_A)))
    wa2 = jnp.pad(w_a2.astype(F32), ((0, LANES - GLA_RANK), (0, 0)))
    wa2h = wa2.astype(BF16)
    wa2l = (wa2 - wa2h.astype(F32)).astype(BF16)

    tq, tk = _pick(T, 512), _pick(T, 512)
    fold = fox_past is not None and T < LANES
    rows = (lambda a: a.reshape(1, B * T, a.shape[-1])) if fold else (lambda a: a)
    unrows = (lambda a: a.reshape(B, T, a.shape[-1])) if fold else (lambda a: a)
    proj_out = _proj(rows(x), wcat, bf, wa2h, wa2l, row2(b_a2), _pick(B * T if fold else T, 512),
                     vt_block=tk if fox_past is None else None)
    proj_out = list(proj_out)
    proj_out[5] = jnp.transpose(proj_out[5], (0, 2, 1))
    if fold:
        proj_out = [unrows(o) for o in proj_out]
    (qa, ka_f, ka_b, va_f, va_b, logf, logf_pad, qb, kb, vb, rb, la, ga, gb) = proj_out

    if fox_past is None:
        kbias = _fox_bias(logf_pad, _pick(T, 512))
        oa = _fox_prompt(qa, ka_b, kbias, va_b, tq, tk)
    else:
        past_k, past_v, past_logf = fox_past
        P = past_k.shape[1]
        lk = -(-(P + T) // FOX_KEY_PAD) * FOX_KEY_PAD
        padk = lambda a: jnp.pad(a, ((0, 0), (0, lk - P - T), (0, 0)))
        k_all = padk(jnp.concatenate([past_k.reshape(B, P, FOX_W).astype(BF16), ka_b], axis=1))
        v_all = padk(jnp.concatenate([past_v.reshape(B, P, FOX_W).astype(BF16), va_b], axis=1))
        lf_all = padk(jnp.concatenate([past_logf.astype(F32), logf], axis=1))
        nc = _neg_cumsum(jnp.transpose(lf_all, (0, 2, 1)), _pick(lk, 512))
        oa = _fox_sample(qa, k_all, v_all, nc.reshape(B, H_A // 2, 2, lk), P)

    chunk = min(GLA_CHUNK, T)
    tc = _pick(T, 4 * chunk)
    hb, s_fin = _gla(qb, kb, vb, la, rb, row2(g_gla), s0.astype(F32), tc, chunk)

    x1 = unrows(_merge(rows(oa), rows(hb), rows(ga), rows(gb), rows(x), w_a_out.astype(BF16),
                       w_b_out.astype(BF16), w_o.astype(BF16), row2(ln1_g), row2(ln1_b),
                       _pick(B * T if fold else T, 1024)))
    y, conv_new = _ffn(x1, p, _ffn_pack(conv_prev.astype(F32)), _ffn_pack(w_up.astype(BF16)),
                       _ffn_pack(conv_w.astype(F32)), _ffn_pack(row2(conv_b)),
                       w_down.astype(BF16), row2(ln2_g), row2(ln2_b), w_pl.astype(BF16),
                       w_plg.astype(BF16), row2(ln3_g), row2(ln3_b), _pick(T, 512))
    conv_new = _ffn_unpack(conv_new)
    k_out = ka_f.reshape(B, T, H_A, DH_A)
    v_out = va_f.reshape(B, T, H_A, DH_A)
    return y, k_out, v_out, logf, s_fin, conv_new


def kernel(x_prompt, x_sample, cache_fox_k, cache_fox_v, cache_fox_logf, state_gla, cache_ffn_conv, p_prompt, p_sample, w_in, b_fgate, w_a2, b_a2, g_gla, w_a_out, w_b_out, w_o, ln1_g, ln1_b, w_up, conv_w, conv_b, w_down, ln2_g, ln2_b, w_pl, w_plg, ln3_g, ln3_b):
    hp, hs = x_prompt, x_sample
    outs_p = [[] for _ in range(5)]
    outs_s = [[] for _ in range(5)]
    for i in range(DEPTH):
        wts = (w_in[i], b_fgate[i], w_a2[i], b_a2[i], g_gla[i], w_a_out[i], w_b_out[i], w_o[i],
               ln1_g[i], ln1_b[i], w_up[i], conv_w[i], conv_b[i], w_down[i], ln2_g[i], ln2_b[i],
               w_pl[i], w_plg[i], ln3_g[i], ln3_b[i])
        Bp = hp.shape[0]
        s0_p = jnp.zeros((Bp, H_B, DK_B, DV_B), F32)
        c0_p = jnp.zeros((Bp, CONV_W - 1, 2 * D_FF), F32)
        hp, *rest_p = _layer(hp, p_prompt[i], None, s0_p, c0_p, wts)
        hs, *rest_s = _layer(hs, p_sample[i], (cache_fox_k[i], cache_fox_v[i], cache_fox_logf[i]),
                             state_gla[i], cache_ffn_conv[i], wts)
        for dst, src in ((outs_p, rest_p), (outs_s, rest_s)):
            for lst, val in zip(dst, src):
                lst.append(val)
    return (hp, hs, *[jnp.stack(l) for l in outs_p], *[jnp.stack(l) for l in outs_s])
```

```python
import functools
import math

import jax
import jax.numpy as jnp
from jax import lax
from jax.experimental import pallas as pl
from jax.experimental.pallas import tpu as pltpu

F32 = jnp.float32
BF16 = jnp.bfloat16

D_MODEL = 1024
H_A, DH_A = 8, 64
FOX_W = H_A * DH_A
H_B, DK_B, DV_B = 4, 128, 256
GLA_K = H_B * DK_B
GLA_V = H_B * DV_B
GLA_RANK = 16
GLA_TAU = 16
D_FF = 2816
CONV_W = 3
PLE_DIM = 256
LN_EPS = 1e-5
DEPTH = 1
ALPHA = (2 * DEPTH) ** 0.25
GLA_CHUNK = 64
GLA_SUB = 16

LANES = 128
LOG2E = math.log2(math.e)
V7X_VMEM_LIMIT_BYTES = 56 * 1024 * 1024

_SEG = {}
_off = 0
for _name, _n in (("qa", FOX_W), ("ka", FOX_W), ("va", FOX_W), ("qb", GLA_K), ("kb", GLA_K),
                  ("vb", GLA_V), ("rb", GLA_V), ("ga", D_MODEL), ("gb", D_MODEL),
                  ("fa", LANES), ("a1", LANES)):
    _SEG[_name] = (_off, _n)
    _off += _n
W_CAT = _off


def _params(*sem):
    return pltpu.CompilerParams(dimension_semantics=sem, vmem_limit_bytes=V7X_VMEM_LIMIT_BYTES)


def _const_spec(shape):
    nd = len(shape)
    return pl.BlockSpec(shape, lambda *_: (0,) * nd, pipeline_mode=pl.Buffered(1))


def _log_sigmoid(z):
    return jnp.minimum(z, 0.0) - jnp.log(1.0 + jnp.exp(-jnp.abs(z)))


def _sigmoid(z):
    return 1.0 / (1.0 + jnp.exp(-z))


def _split3(a):
    hi = a.astype(BF16)
    r = a - hi.astype(F32)
    mid = r.astype(BF16)
    lo = (r - mid.astype(F32)).astype(BF16)
    return hi, mid, lo


def _layer_norm(z, g, b):
    mu = jnp.mean(z, axis=-1, keepdims=True)
    zc = z - mu
    var = jnp.mean(zc * zc, axis=-1, keepdims=True)
    return zc * lax.rsqrt(var + LN_EPS) * g + b


def _dot(a, b):
    return jnp.dot(a, b, preferred_element_type=F32)


def _dot_nt(a, b):
    return lax.dot_general(a, b, (((1,), (1,)), ((), ())), preferred_element_type=F32)


def _dot_tn(a, b):
    return lax.dot_general(a, b, (((0,), (0,)), ((), ())), preferred_element_type=F32)


def _proj_kernel(x_ref, w_ref, bf_ref, wa2h_ref, wa2l_ref, ba2_ref,
                 qa_ref, kaf_ref, kab_ref, vaf_ref, vab_ref, logf_ref, logfp_ref,
                 qb_ref, kb_ref, vb_ref, rb_ref, la_ref, ga_ref, gb_ref, *, values_transposed):
    xb = x_ref[...].astype(BF16)

    def seg(name):
        lo, n = _SEG[name]
        return _dot(xb, w_ref[:, lo:lo + n])

    qa_ref[...] = (seg("qa") * (DH_A ** -0.5 * LOG2E)).astype(BF16)
    ka = seg("ka")
    kaf_ref[...] = ka
    kab_ref[...] = ka.astype(BF16)
    va = seg("va")
    vaf_ref[...] = va
    if values_transposed:
        for p in range(H_A // 2):
            vab_ref[p] = va[:, p * LANES:(p + 1) * LANES].T.astype(BF16)
    else:
        vab_ref[...] = va.astype(BF16)
    qb_ref[...] = seg("qb") * (DK_B ** -0.5)
    kb_ref[...] = seg("kb")
    vb_ref[...] = seg("vb").astype(BF16)
    rb_ref[...] = seg("rb")
    ga_ref[...] = seg("ga").astype(BF16)
    gb_ref[...] = seg("gb").astype(BF16)
    logf = _log_sigmoid(seg("fa") + bf_ref[...])
    logf_ref[...] = logf.T[:H_A, :]
    logfp_ref[...] = logf
    a1 = seg("a1")
    a1h = a1.astype(BF16)
    a1l = (a1 - a1h.astype(F32)).astype(BF16)
    z = _dot(a1h, wa2h_ref[...]) + _dot(a1l, wa2h_ref[...]) + _dot(a1h, wa2l_ref[...]) + ba2_ref[...]
    la_ref[...] = _log_sigmoid(z) * (1.0 / GLA_TAU)


def _proj(x, wcat, bf, wa2h, wa2l, ba2, tm, vt_block=None):
    B, T, D = x.shape
    grid = (B, T // tm)

    def row(n, dt):
        return pl.BlockSpec((None, tm, n), lambda b, t: (b, t, 0)), jax.ShapeDtypeStruct((B, T, n), dt)

    outs = [row(FOX_W, BF16), row(FOX_W, F32), row(FOX_W, BF16), row(FOX_W, F32), row(FOX_W, BF16),
            row(H_A, F32), row(LANES, F32), row(GLA_K, F32), row(GLA_K, F32), row(GLA_V, BF16), row(GLA_V, F32),
            row(GLA_K, F32), row(D_MODEL, BF16), row(D_MODEL, BF16)]
    outs[5] = (pl.BlockSpec((None, H_A, tm), lambda b, t: (b, 0, t)), jax.ShapeDtypeStruct((B, H_A, T), F32))
    if vt_block is not None:
        per = vt_block // tm
        outs[4] = (pl.BlockSpec((None, H_A // 2, None, LANES, tm), lambda b, t: (b, 0, t // per, 0, t % per)),
                   jax.ShapeDtypeStruct((B, H_A // 2, T // vt_block, LANES, vt_block), BF16))
    return pl.pallas_call(
        functools.partial(_proj_kernel, values_transposed=vt_block is not None),
        grid=grid,
        in_specs=[pl.BlockSpec((None, tm, D), lambda b, t: (b, t, 0)),
                  _const_spec(wcat.shape), _const_spec(bf.shape), _const_spec(wa2h.shape),
                  _const_spec(wa2l.shape), _const_spec(ba2.shape)],
        out_specs=[o[0] for o in outs],
        out_shape=[o[1] for o in outs],
        compiler_params=_params("parallel", "parallel"),
        name="proj",
    )(x, wcat, bf, wa2h, wa2l, ba2)


def _cumsum_kernel(lf_ref, nc_ref, carry_ref):
    @pl.when(pl.program_id(1) == 0)
    def _():
        carry_ref[...] = jnp.zeros_like(carry_ref)

    x = lf_ref[...]
    tl = x.shape[1]
    r = lax.broadcasted_iota(jnp.int32, (tl, tl), 0)
    c = lax.broadcasted_iota(jnp.int32, (tl, tl), 1)
    tri = jnp.where(r <= c, 1.0, 0.0).astype(BF16)
    hi, mid, lo = _split3(x)
    cs = _dot(hi, tri) + _dot(mid, tri) + _dot(lo, tri) + carry_ref[...]
    nc_ref[...] = -cs
    carry_ref[...] = carry_ref[...] + jnp.sum(x, axis=1, keepdims=True)


def _neg_cumsum(lf_t, tl):
    B, H, L = lf_t.shape
    return pl.pallas_call(
        _cumsum_kernel,
        grid=(B, L // tl),
        in_specs=[pl.BlockSpec((None, H, tl), lambda b, t: (b, 0, t))],
        out_specs=pl.BlockSpec((None, H, tl), lambda b, t: (b, 0, t)),
        out_shape=jax.ShapeDtypeStruct((B, H, L), F32),
        scratch_shapes=[pltpu.VMEM((H, 1), F32)],
        compiler_params=_params("parallel", "arbitrary"),
        name="cumsum",
    )(lf_t)


FOX_KEY_PAD = 512
FOX_BIAS_PARTS = 3
FOX_UNROLL = 4
FOX_SUM_ROWS = 16


def _fox_bias_kernel(lf_ref, kb_ref, carry_ref):
    @pl.when(pl.program_id(1) == 0)
    def _():
        carry_ref[...] = jnp.zeros_like(carry_ref)

    x = lf_ref[...]
    tl = x.shape[0]
    r = lax.broadcasted_iota(jnp.int32, (tl, tl), 0)
    c = lax.broadcasted_iota(jnp.int32, (tl, tl), 1)
    tri = jnp.where(c <= r, 1.0, 0.0).astype(BF16)
    hi, mid, lo = _split3(x)
    cs = _dot(tri, hi) + _dot(tri, mid) + _dot(tri, lo) + carry_ref[...]
    carry_ref[...] = cs[tl - 1:tl, :]
    parts = [p.astype(F32) for p in _split3(cs * (-LOG2E))]
    lane = lax.broadcasted_iota(jnp.int32, (1, LANES), 1)
    out = jnp.zeros_like(x)
    for j in reversed(range(FOX_BIAS_PARTS)):
        moved = parts[j] if j == 0 else pltpu.roll(parts[j], j * H_A, 1)
        out = jnp.where(jnp.logical_and(lane >= j * H_A, lane < (j + 1) * H_A), moved, out)
    kb_ref[...] = out.astype(kb_ref.dtype)


def _fox_bias(logf_pad, tl):
    B, T, _ = logf_pad.shape
    return pl.pallas_call(
        _fox_bias_kernel,
        grid=(B, T // tl),
        in_specs=[pl.BlockSpec((None, tl, LANES), lambda b, t: (b, t, 0))],
        out_specs=pl.BlockSpec((None, tl, LANES), lambda b, t: (b, t, 0)),
        out_shape=jax.ShapeDtypeStruct((B, T, LANES), BF16),
        scratch_shapes=[pltpu.VMEM((1, LANES), F32)],
        compiler_params=_params("parallel", "arbitrary"),
        name="fox_bias",
    )(logf_pad)


def _fox_prompt_kernel(q_ref, k_ref, kb_ref, vt_ref, o_ref, m_ref, acc_ref,
                       sa_ref, sb_ref, bma_ref, bmb_ref, *, tq, tk):
    i = pl.program_id(2)
    lane = lax.broadcasted_iota(jnp.int32, (1, LANES), 1)
    q = q_ref[...]
    zero = jnp.zeros_like(q)
    qa = []
    for h in range(2):
        own = (lane < DH_A) if h == 0 else (lane >= DH_A)
        head = 2 * pl.program_id(1) + h
        ones = jnp.logical_and(jnp.bitwise_and(lane, H_A - 1) == head, lane < FOX_BIAS_PARTS * H_A)
        ones = jnp.broadcast_to(jnp.where(ones, 1.0, 0.0).astype(q.dtype), q.shape)
        qa.append(jnp.concatenate([jnp.where(own, q, zero), ones], axis=1))
    m_ref[...] = jnp.full_like(m_ref, -jnp.inf)
    acc_ref[...] = jnp.zeros_like(acc_ref)
    sum_rows = jnp.ones((FOX_SUM_ROWS, tk), BF16)
    per = tq // tk

    def scores(j, s_ref, bm_ref):
        start = pl.multiple_of(j * tk, tk)
        ka = jnp.concatenate([k_ref[pl.ds(start, tk), :], kb_ref[pl.ds(start, tk), :]], axis=1)
        for h in range(2):
            st = _dot_nt(ka, qa[h])
            s_ref[h] = st
            bm_ref[h] = jnp.max(st, axis=0, keepdims=True)

    def soft_pv(j, s_ref, bm_ref, diag):
        for h in range(2):
            st = s_ref[h]
            if diag is not None:
                r = lax.broadcasted_iota(jnp.int32, (tk, tq), 0)
                c = lax.broadcasted_iota(jnp.int32, (tk, tq), 1)
                st = jnp.where(r + diag * tk <= c, st, -jnp.inf)
                bm = jnp.max(st, axis=0, keepdims=True)
            else:
                bm = bm_ref[h]
            m_old = m_ref[h]
            m_new = jnp.maximum(m_old, bm)
            p = jnp.exp2(st - m_new)
            a = jnp.exp2(m_old - m_new)
            vt = jnp.concatenate([vt_ref[j, h * DH_A:(h + 1) * DH_A, :], sum_rows], axis=0)
            acc_ref[h] = a * acc_ref[h] + _dot(vt, p.astype(BF16))
            m_ref[h] = m_new

    slots = ((sa_ref, bma_ref), (sb_ref, bmb_ref))

    def run(j0, diags, prefetch_after):
        for u, diag in enumerate(diags):
            if u + 1 < len(diags) or prefetch_after:
                scores(j0 + u + 1, *slots[(u + 1) % 2])
            soft_pv(j0 + u, *slots[u % 2], diag)

    scores(0, *slots[0])

    def body(t, carry):
        run(FOX_UNROLL * t, [None] * FOX_UNROLL, True)
        return carry

    full = i * per
    log_unroll = FOX_UNROLL.bit_length() - 1
    lax.fori_loop(0, lax.shift_right_logical(full, log_unroll), body, 0)
    rest = jnp.bitwise_and(full, FOX_UNROLL - 1)
    for n in range(0, FOX_UNROLL, per):
        @pl.when(rest == n)
        def _(n=n):
            run(full - n, [None] * n + list(range(per)), False)

    ot = jnp.concatenate([acc_ref[h, :DH_A, :] / acc_ref[h, DH_A:DH_A + 1, :] for h in range(2)],
                         axis=0)
    o_ref[...] = ot.T.astype(o_ref.dtype)


def _fox_prompt(qa, ka, kbias, vt, tq, tk):
    B, T, _ = qa.shape
    nk = T // tk
    return pl.pallas_call(
        functools.partial(_fox_prompt_kernel, tq=tq, tk=tk),
        grid=(B, H_A // 2, T // tq),
        in_specs=[pl.BlockSpec((None, tq, LANES), lambda b, p, i: (b, i, p)),
                  pl.BlockSpec((None, T, LANES), lambda b, p, i: (b, 0, p)),
                  pl.BlockSpec((None, T, LANES), lambda b, p, i: (b, 0, 0)),
                  pl.BlockSpec((None, None, nk, LANES, tk), lambda b, p, i: (b, p, 0, 0, 0))],
        out_specs=pl.BlockSpec((None, tq, LANES), lambda b, p, i: (b, i, p)),
        out_shape=jax.ShapeDtypeStruct((B, T, FOX_W), BF16),
        scratch_shapes=[pltpu.VMEM((2, 1, tq), F32),
                        pltpu.VMEM((2, DH_A + FOX_SUM_ROWS, tq), F32),
                        pltpu.VMEM((2, tk, tq), F32), pltpu.VMEM((2, tk, tq), F32),
                        pltpu.VMEM((2, 1, tq), F32), pltpu.VMEM((2, 1, tq), F32)],
        compiler_params=_params("parallel", "parallel", "arbitrary"),
        name="fox_prompt",
    )(qa, ka, kbias, vt)


def _fox_sample_kernel(q_ref, k_ref, v_ref, nc_ref, o_ref, *, past):
    first = lax.broadcasted_iota(jnp.int32, (1, LANES), 1) < DH_A
    q = q_ref[...]
    zero = jnp.zeros_like(q)
    tq = q.shape[0]
    kb = k_ref[...]
    vb = v_ref[...]
    lk = kb.shape[0]
    r = lax.broadcasted_iota(jnp.int32, (tq, lk), 0)
    c = lax.broadcasted_iota(jnp.int32, (tq, lk), 1)
    visible = c <= r + past
    outs = []
    for h in range(2):
        qh = jnp.where(first, q, zero) if h == 0 else jnp.where(first, zero, q)
        s = _dot_nt(qh, kb) + nc_ref[h:h + 1, :] * LOG2E
        s = jnp.where(visible, s, -jnp.inf)
        m = jnp.max(s, axis=1, keepdims=True)
        p = jnp.exp2(s - m)
        l = jnp.sum(p, axis=1, keepdims=True)
        outs.append(_dot(p.astype(BF16), vb) / l)
    o_ref[...] = jnp.where(first, outs[0], outs[1]).astype(o_ref.dtype)


def _fox_sample(qa, k_all, v_all, nc, past):
    B, Tq, _ = qa.shape
    Lk = k_all.shape[1]
    return pl.pallas_call(
        functools.partial(_fox_sample_kernel, past=past),
        grid=(B, H_A // 2),
        in_specs=[pl.BlockSpec((None, Tq, LANES), lambda b, p: (b, 0, p)),
                  pl.BlockSpec((None, Lk, LANES), lambda b, p: (b, 0, p)),
                  pl.BlockSpec((None, Lk, LANES), lambda b, p: (b, 0, p)),
                  pl.BlockSpec((None, None, 2, Lk), lambda b, p: (b, p, 0, 0))],
        out_specs=pl.BlockSpec((None, Tq, LANES), lambda b, p: (b, 0, p)),
        out_shape=jax.ShapeDtypeStruct((B, Tq, FOX_W), BF16),
        compiler_params=_params("parallel", "parallel"),
        name="fox_sample",
    )(qa, k_all, v_all, nc)


def _gla_kernel(q_ref, k_ref, v_ref, la_ref, rb_ref, g_ref, s0_ref, hb_ref, sfin_ref, st_ref,
                *, chunk, sub):
    t = pl.program_id(1)

    @pl.when(t == 0)
    def _():
        for h in range(H_B):
            st_ref[h] = s0_ref[h].T

    tc = q_ref.shape[0]
    nsub = chunk // sub
    nchunk = tc // chunk
    r = lax.broadcasted_iota(jnp.int32, (tc, tc), 0)
    c = lax.broadcasted_iota(jnp.int32, (tc, tc), 1)
    log_chunk = chunk.bit_length() - 1
    log_sub = sub.bit_length() - 1
    same_chunk = jnp.right_shift(r, log_chunk) == jnp.right_shift(c, log_chunk)
    near = jnp.logical_and(c <= r, same_chunk)
    far = jnp.right_shift(c, log_chunk) < jnp.right_shift(r, log_chunk)
    tri = jnp.where(c <= r, 1.0, 0.0).astype(BF16)
    hi, mid, lo = _split3(la_ref[...])
    b_all = (_dot(tri, hi) + _dot(tri, mid) + _dot(tri, lo)) * LOG2E

    row = lax.broadcasted_iota(jnp.int32, (tc, 1), 0)
    row_sub = jnp.bitwise_and(jnp.right_shift(row, log_sub), nsub - 1)
    row_chunk = jnp.right_shift(row, log_chunk)

    def rows_of(refs, n):
        return jnp.concatenate([jnp.broadcast_to(rf, (n, DK_B)) for rf in refs], axis=0)

    def zeros(n):
        return jnp.zeros((n, DK_B), F32)

    for h in range(H_B):
        kc = slice(h * DK_B, (h + 1) * DK_B)
        vc = slice(h * DV_B, (h + 1) * DV_B)
        b = b_all[:, kc]
        q = q_ref[:, kc]
        k = k_ref[:, kc]
        v = v_ref[:, vc]
        b_last = b[tc - 1:tc, :]
        st = st_ref[h]
        sub_refs = [b[i * sub:i * sub + 1, :] for i in range(tc // sub)]
        chunk_refs = [b[ci * chunk:ci * chunk + 1, :] for ci in range(nchunk)]

        qt = q * jnp.exp2(b - rows_of(sub_refs, sub))
        qx = [jnp.where(row_sub == i, qt, 0.0) for i in range(nsub)]
        kx = []
        for i in range(nsub):
            pieces = []
            for ci in range(nchunk):
                lo_row, n = ci * chunk, (i + 1) * sub
                pieces.append(k[lo_row:lo_row + n] * jnp.exp2(sub_refs[ci * nsub + i] - b[lo_row:lo_row + n]))
                if n < chunk:
                    pieces.append(zeros(chunk - n))
            kx.append(jnp.concatenate(pieces, axis=0) if len(pieces) > 1 else pieces[0])
        a = jnp.where(near, _dot_nt(jnp.concatenate(qx, axis=1).astype(BF16),
                                    jnp.concatenate(kx, axis=1).astype(BF16)), 0.0)
        if nchunk > 1:
            qc = q * jnp.exp2(b - rows_of(chunk_refs, chunk))
            qx = [jnp.where(row_chunk == ci, qc, 0.0) for ci in range(1, nchunk)]
            kx = []
            for ci in range(1, nchunk):
                n = ci * chunk
                kx.append(jnp.concatenate([k[:n] * jnp.exp2(chunk_refs[ci] - b[:n]), zeros(tc - n)], axis=0))
            a = jnp.where(far, _dot_nt(jnp.concatenate(qx, axis=1).astype(BF16),
                                       jnp.concatenate(kx, axis=1).astype(BF16)), a)
        o = _dot(a.astype(BF16), v) + _dot_nt((q * jnp.exp2(b)).astype(BF16), st.astype(BF16))
        kd = (k * jnp.exp2(b_last - b)).astype(BF16)
        st_ref[h] = st * jnp.exp2(b_last) + _dot_tn(v, kd)
        ms = jnp.mean(o * o, axis=-1, keepdims=True)
        on = o * lax.rsqrt(ms + LN_EPS) * g_ref[:, vc]
        rbv = rb_ref[:, vc]
        hb_ref[:, vc] = (on * (rbv * _sigmoid(rbv))).astype(hb_ref.dtype)

    @pl.when(t == pl.num_programs(1) - 1)
    def _():
        for h in range(H_B):
            sfin_ref[h] = st_ref[h].T


def _gla(qb, kb, vb, la, rb, g, s0_t, tc, chunk):
    B, T, _ = qb.shape
    sub = min(GLA_SUB, chunk)

    def row(n):
        return pl.BlockSpec((None, tc, n), lambda b, t: (b, t, 0))

    st_spec = pl.BlockSpec((None, H_B, DK_B, DV_B), lambda b, t: (b, 0, 0, 0))
    return pl.pallas_call(
        functools.partial(_gla_kernel, chunk=chunk, sub=sub),
        grid=(B, T // tc),
        in_specs=[row(GLA_K), row(GLA_K), row(GLA_V), row(GLA_K), row(GLA_V),
                  _const_spec(g.shape), st_spec],
        out_specs=[row(GLA_V), st_spec],
        out_shape=[jax.ShapeDtypeStruct((B, T, GLA_V), BF16),
                   jax.ShapeDtypeStruct((B, H_B, DK_B, DV_B), F32)],
        scratch_shapes=[pltpu.VMEM((H_B, DV_B, DK_B), F32)],
        compiler_params=_params("parallel", "arbitrary"),
        name="gla",
    )(qb, kb, vb, la, rb, g, s0_t)


def _merge_kernel(oa_ref, hb_ref, ga_ref, gb_ref, x_ref, wa_ref, wb_ref, wo_ref, g_ref, b_ref, x1_ref):
    ya = _dot(oa_ref[...], wa_ref[...])
    yb = _dot(hb_ref[...], wb_ref[...])
    merged = _sigmoid(ga_ref[...].astype(F32)) * ya + _sigmoid(gb_ref[...].astype(F32)) * yb
    z = ALPHA * x_ref[...] + _dot(merged.astype(BF16), wo_ref[...])
    x1_ref[...] = _layer_norm(z, g_ref[...], b_ref[...])


def _merge(oa, hb, ga, gb, x, wa, wb, wo, g, b, tm):
    B, T, D = x.shape

    def row(n):
        return pl.BlockSpec((None, tm, n), lambda bb, t: (bb, t, 0))

    return pl.pallas_call(
        _merge_kernel,
        grid=(B, T // tm),
        in_specs=[row(FOX_W), row(GLA_V), row(D), row(D), row(D),
                  _const_spec(wa.shape), _const_spec(wb.shape), _const_spec(wo.shape),
                  _const_spec(g.shape), _const_spec(b.shape)],
        out_specs=row(D),
        out_shape=jax.ShapeDtypeStruct((B, T, D), F32),
        compiler_params=_params("parallel", "parallel"),
        name="merge",
    )(oa, hb, ga, gb, x, wa, wb, wo, g, b)


FFN_GROUP = 256


def _gelu_tanh(x):
    return 0.5 * x * (1.0 + jnp.tanh(math.sqrt(2.0 / math.pi) * (x + 0.044715 * (x * x * x))))


def _ffn_pack(a):
    return jnp.concatenate([a[..., k * D_FF + j * FFN_GROUP:k * D_FF + (j + 1) * FFN_GROUP]
                            for j in range(D_FF // FFN_GROUP) for k in range(2)], axis=-1)


def _ffn_unpack(a):
    lead = a.shape[:-1]
    return a.reshape(*lead, D_FF // FFN_GROUP, 2, FFN_GROUP).swapaxes(-3, -2).reshape(*lead, 2 * D_FF)


def _ffn_kernel(x1_ref, p_ref, cprev_ref, wup_ref, cw_ref, cb_ref, wdn_ref, g2_ref, b2_ref,
                wpl_ref, wplg_ref, g3_ref, b3_ref, y_ref, cnew_ref, carry_ref):
    @pl.when(pl.program_id(1) == 0)
    def _():
        carry_ref[...] = cprev_ref[...]

    x1 = x1_ref[...]
    tm = x1.shape[0]
    up = _dot(x1.astype(BF16), wup_ref[...])
    prev = carry_ref[...]
    row = lax.broadcasted_iota(jnp.int32, (tm, 1), 0)
    up_m1 = jnp.where(row == 0, prev[1:2, :], pltpu.roll(up, 1, 0))
    up_m2 = jnp.where(row == 0, prev[0:1, :], jnp.where(row == 1, prev[1:2, :], pltpu.roll(up, 2, 0)))
    conv = cb_ref[...] + cw_ref[0:1, :] * up_m2 + cw_ref[1:2, :] * up_m1 + cw_ref[2:3, :] * up
    last2 = up[tm - 2:tm, :]
    carry_ref[...] = last2
    cnew_ref[...] = last2
    hgl = jnp.concatenate(
        [conv[:, 2 * j * FFN_GROUP:(2 * j + 1) * FFN_GROUP]
         * _gelu_tanh(conv[:, (2 * j + 1) * FFN_GROUP:(2 * j + 2) * FFN_GROUP])
         for j in range(D_FF // FFN_GROUP)], axis=1).astype(BF16)
    z2 = ALPHA * x1 + _dot(hgl, wdn_ref[...])
    x2 = _layer_norm(z2, g2_ref[...], b2_ref[...])
    e = _dot(p_ref[...].astype(BF16), wpl_ref[...]) * _sigmoid(_dot(x2.astype(BF16), wplg_ref[...]))
    y_ref[...] = _layer_norm(ALPHA * x2 + e, g3_ref[...], b3_ref[...])


def _ffn(x1, p, cprev, wup, cw, cb, wdn, g2, b2, wpl, wplg, g3, b3, tm):
    B, T, D = x1.shape

    def row(n):
        return pl.BlockSpec((None, tm, n), lambda bb, t: (bb, t, 0))

    cspec = pl.BlockSpec((None, CONV_W - 1, 2 * D_FF), lambda bb, t: (bb, 0, 0))
    consts = (wup, cw, cb, wdn, g2, b2, wpl, wplg, g3, b3)
    return pl.pallas_call(
        _ffn_kernel,
        grid=(B, T // tm),
        in_specs=[row(D), row(PLE_DIM), cspec] + [_const_spec(c.shape) for c in consts],
        out_specs=[row(D), cspec],
        out_shape=[jax.ShapeDtypeStruct((B, T, D), F32),
                   jax.ShapeDtypeStruct((B, CONV_W - 1, 2 * D_FF), F32)],
        scratch_shapes=[pltpu.VMEM((CONV_W - 1, 2 * D_FF), F32)],
        compiler_params=_params("parallel", "arbitrary"),
        name="ffn",
    )(x1, p, cprev, *consts)


def _pick(T, want):
    t = min(T, want)
    while T % t:
        t //= 2
    return t


def _pack_w_in(w_in):
    pts, acc = {}, 0
    for name, n in (("qa", FOX_W), ("ka", FOX_W), ("va", FOX_W), ("fa", H_A), ("qb", GLA_K), ("kb", GLA_K),
                    ("vb", GLA_V), ("rb", GLA_V), ("a1", GLA_RANK), ("ga", D_MODEL), ("gb", D_MODEL)):
        pts[name] = (acc, n)
        acc += n
    cols = []
    for name in _SEG:
        lo, n = pts[name]
        piece = w_in[:, lo:lo + n]
        if n < _SEG[name][1]:
            piece = jnp.pad(piece, ((0, 0), (0, _SEG[name][1] - n)))
        cols.append(piece)
    return jnp.concatenate(cols, axis=1).astype(BF16)


def _layer(x, p, fox_past, s0, conv_prev, wts):
    (w_in, b_fgate, w_a2, b_a2, g_gla, w_a_out, w_b_out, w_o, ln1_g, ln1_b, w_up, conv_w, conv_b,
     w_down, ln2_g, ln2_b, w_pl, w_plg, ln3_g, ln3_b) = wts
    B, T, D = x.shape
    row2 = lambda a: a.reshape(1, -1).astype(F32)

    wcat = _pack_w_in(w_in)
    bf = jnp.pad(row2(b_fgate), ((0, 0), (0, LANES - H_A)))
    wa2 = jnp.pad(w_a2.astype(F32), ((0, LANES - GLA_RANK), (0, 0)))
    wa2h = wa2.astype(BF16)
    wa2l = (wa2 - wa2h.astype(F32)).astype(BF16)

    tq, tk = _pick(T, 512), _pick(T, 512)
    fold = fox_past is not None and T < LANES
    rows = (lambda a: a.reshape(1, B * T, a.shape[-1])) if fold else (lambda a: a)
    unrows = (lambda a: a.reshape(B, T, a.shape[-1])) if fold else (lambda a: a)
    proj_out = _proj(rows(x), wcat, bf, wa2h, wa2l, row2(b_a2), _pick(B * T if fold else T, 512),
                     vt_block=tk if fox_past is None else None)
    proj_out = list(proj_out)
    proj_out[5] = jnp.transpose(proj_out[5], (0, 2, 1))
    if fold:
        proj_out = [unrows(o) for o in proj_out]
    (qa, ka_f, ka_b, va_f, va_b, logf, logf_pad, qb, kb, vb, rb, la, ga, gb) = proj_out

    if fox_past is None:
        kbias = _fox_bias(logf_pad, _pick(T, 512))
        oa = _fox_prompt(qa, ka_b, kbias, va_b, tq, tk)
    else:
        past_k, past_v, past_logf = fox_past
        P = past_k.shape[1]
        lk = -(-(P + T) // FOX_KEY_PAD) * FOX_KEY_PAD
        padk = lambda a: jnp.pad(a, ((0, 0), (0, lk - P - T), (0, 0)))
        k_all = padk(jnp.concatenate([past_k.reshape(B, P, FOX_W).astype(BF16), ka_b], axis=1))
        v_all = padk(jnp.concatenate([past_v.reshape(B, P, FOX_W).astype(BF16), va_b], axis=1))
        lf_all = padk(jnp.concatenate([past_logf.astype(F32), logf], axis=1))
        nc = _neg_cumsum(jnp.transpose(lf_all, (0, 2, 1)), _pick(lk, 512))
        oa = _fox_sample(qa, k_all, v_all, nc.reshape(B, H_A // 2, 2, lk), P)

    chunk = min(GLA_CHUNK, T)
    tc = _pick(T, 4 * chunk)
    hb, s_fin = _gla(qb, kb, vb, la, rb, row2(g_gla), s0.astype(F32), tc, chunk)

    x1 = unrows(_merge(rows(oa), rows(hb), rows(ga), rows(gb), rows(x), w_a_out.astype(BF16),
                       w_b_out.astype(BF16), w_o.astype(BF16), row2(ln1_g), row2(ln1_b),
                       _pick(B * T if fold else T, 1024)))
    y, conv_new = _ffn(x1, p, _ffn_pack(conv_prev.astype(F32)), _ffn_pack(w_up.astype(BF16)),
                       _ffn_pack(conv_w.astype(F32)), _ffn_pack(row2(conv_b)),
                       w_down.astype(BF16), row2(ln2_g), row2(ln2_b), w_pl.astype(BF16),
                       w_plg.astype(BF16), row2(ln3_g), row2(ln3_b), _pick(T, 512))
    conv_new = _ffn_unpack(conv_new)
    k_out = ka_f.reshape(B, T, H_A, DH_A)
    v_out = va_f.reshape(B, T, H_A, DH_A)
    return y, k_out, v_out, logf, s_fin, conv_new


def kernel(x_prompt, x_sample, cache_fox_k, cache_fox_v, cache_fox_logf, state_gla, cache_ffn_conv, p_prompt, p_sample, w_in, b_fgate, w_a2, b_a2, g_gla, w_a_out, w_b_out, w_o, ln1_g, ln1_b, w_up, conv_w, conv_b, w_down, ln2_g, ln2_b, w_pl, w_plg, ln3_g, ln3_b):
    hp, hs = x_prompt, x_sample
    outs_p = [[] for _ in range(5)]
    outs_s = [[] for _ in range(5)]
    for i in range(DEPTH):
        wts = (w_in[i], b_fgate[i], w_a2[i], b_a2[i], g_gla[i], w_a_out[i], w_b_out[i], w_o[i],
               ln1_g[i], ln1_b[i], w_up[i], conv_w[i], conv_b[i], w_down[i], ln2_g[i], ln2_b[i],
               w_pl[i], w_plg[i], ln3_g[i], ln3_b[i])
        Bp = hp.shape[0]
        s0_p = jnp.zeros((Bp, H_B, DK_B, DV_B), F32)
        c0_p = jnp.zeros((Bp, CONV_W - 1, 2 * D_FF), F32)
        hp, *rest_p = _layer(hp, p_prompt[i], None, s0_p, c0_p, wts)
        hs, *rest_s = _layer(hs, p_sample[i], (cache_fox_k[i], cache_fox_v[i], cache_fox_logf[i]),
                             state_gla[i], cache_ffn_conv[i], wts)
        for dst, src in ((outs_p, rest_p), (outs_s, rest_s)):
            for lst, val in zip(dst, src):
                lst.append(val)
    return (hp, hs, *[jnp.stack(l) for l in outs_p], *[jnp.stack(l) for l in outs_s])
```

```python
import functools
import math

import jax
import jax.numpy as jnp
from jax import lax
from jax.experimental import pallas as pl
from jax.experimental.pallas import tpu as pltpu

F32 = jnp.float32
BF16 = jnp.bfloat16

D_MODEL = 1024
H_A, DH_A = 8, 64
FOX_W = H_A * DH_A
H_B, DK_B, DV_B = 4, 128, 256
GLA_K = H_B * DK_B
GLA_V = H_B * DV_B
GLA_RANK = 16
GLA_TAU = 16
D_FF = 2816
CONV_W = 3
PLE_DIM = 256
LN_EPS = 1e-5
DEPTH = 1
ALPHA = (2 * DEPTH) ** 0.25
GLA_CHUNK = 64
GLA_SUB = 16

LANES = 128
LOG2E = math.log2(math.e)
V7X_VMEM_LIMIT_BYTES = 56 * 1024 * 1024

_SEG = {}
_off = 0
for _name, _n in (("qa", FOX_W), ("ka", FOX_W), ("va", FOX_W), ("qb", GLA_K), ("kb", GLA_K),
                  ("vb", GLA_V), ("rb", GLA_V), ("ga", D_MODEL), ("gb", D_MODEL),
                  ("fa", LANES), ("a1", LANES)):
    _SEG[_name] = (_off, _n)
    _off += _n
W_CAT = _off


def _params(*sem):
    return pltpu.CompilerParams(dimension_semantics=sem, vmem_limit_bytes=V7X_VMEM_LIMIT_BYTES)


def _const_spec(shape):
    nd = len(shape)
    return pl.BlockSpec(shape, lambda *_: (0,) * nd, pipeline_mode=pl.Buffered(1))


def _log_sigmoid(z):
    return jnp.minimum(z, 0.0) - jnp.log(1.0 + jnp.exp(-jnp.abs(z)))


def _sigmoid(z):
    return 1.0 / (1.0 + jnp.exp(-z))


def _split3(a):
    hi = a.astype(BF16)
    r = a - hi.astype(F32)
    mid = r.astype(BF16)
    lo = (r - mid.astype(F32)).astype(BF16)
    return hi, mid, lo


def _layer_norm(z, g, b):
    mu = jnp.mean(z, axis=-1, keepdims=True)
    zc = z - mu
    var = jnp.mean(zc * zc, axis=-1, keepdims=True)
    return zc * lax.rsqrt(var + LN_EPS) * g + b


def _dot(a, b):
    return jnp.dot(a, b, preferred_element_type=F32)


def _dot_nt(a, b):
    return lax.dot_general(a, b, (((1,), (1,)), ((), ())), preferred_element_type=F32)


def _dot_tn(a, b):
    return lax.dot_general(a, b, (((0,), (0,)), ((), ())), preferred_element_type=F32)


def _proj_kernel(x_ref, w_ref, bf_ref, wa2h_ref, wa2l_ref, ba2_ref,
                 qa_ref, kaf_ref, kab_ref, vaf_ref, vab_ref, logf_ref, logfp_ref,
                 qb_ref, kb_ref, vb_ref, rb_ref, la_ref, ga_ref, gb_ref, carry_ref, *, values_transposed):
    xb = x_ref[...].astype(BF16)

    def seg(name):
        lo, n = _SEG[name]
        return _dot(xb, w_ref[:, lo:lo + n])

    logf = _log_sigmoid(seg("fa") + bf_ref[...])
    logf_ref[...] = logf.T[:H_A, :]
    if values_transposed:
        logfp_ref[...] = _fox_bias_lanes(logf, carry_ref, pl.program_id(1) == 0).astype(logfp_ref.dtype)
    else:
        logfp_ref[...] = logf
    qa_ref[...] = (seg("qa") * (DH_A ** -0.5 * LOG2E)).astype(BF16)
    ka = seg("ka")
    kaf_ref[...] = ka
    kab_ref[...] = ka.astype(BF16)
    va = seg("va")
    vaf_ref[...] = va
    if values_transposed:
        for p in range(H_A // 2):
            vab_ref[p] = va[:, p * LANES:(p + 1) * LANES].T.astype(BF16)
    else:
        vab_ref[...] = va.astype(BF16)
    qb_ref[...] = seg("qb") * (DK_B ** -0.5)
    kb_ref[...] = seg("kb")
    vb_ref[...] = seg("vb").astype(BF16)
    rb_ref[...] = seg("rb")
    ga_ref[...] = seg("ga").astype(BF16)
    gb_ref[...] = seg("gb").astype(BF16)
    a1 = seg("a1")
    a1h = a1.astype(BF16)
    a1l = (a1 - a1h.astype(F32)).astype(BF16)
    z = _dot(a1h, wa2h_ref[...]) + _dot(a1l, wa2h_ref[...]) + _dot(a1h, wa2l_ref[...]) + ba2_ref[...]
    la_ref[...] = _log_sigmoid(z) * (1.0 / GLA_TAU)


def _proj(x, wcat, bf, wa2h, wa2l, ba2, tm, vt_block=None):
    B, T, D = x.shape
    grid = (B, T // tm)

    def row(n, dt):
        return pl.BlockSpec((None, tm, n), lambda b, t: (b, t, 0)), jax.ShapeDtypeStruct((B, T, n), dt)

    outs = [row(FOX_W, BF16), row(FOX_W, F32), row(FOX_W, BF16), row(FOX_W, F32), row(FOX_W, BF16),
            row(H_A, F32), row(LANES, F32), row(GLA_K, F32), row(GLA_K, F32), row(GLA_V, BF16), row(GLA_V, F32),
            row(GLA_K, F32), row(D_MODEL, BF16), row(D_MODEL, BF16)]
    outs[5] = (pl.BlockSpec((None, H_A, tm), lambda b, t: (b, 0, t)), jax.ShapeDtypeStruct((B, H_A, T), F32))
    if vt_block is not None:
        per = vt_block // tm
        outs[4] = (pl.BlockSpec((None, H_A // 2, None, LANES, tm), lambda b, t: (b, 0, t // per, 0, t % per)),
                   jax.ShapeDtypeStruct((B, H_A // 2, T // vt_block, LANES, vt_block), BF16))
        outs[6] = row(LANES, BF16)
    return pl.pallas_call(
        functools.partial(_proj_kernel, values_transposed=vt_block is not None),
        grid=grid,
        in_specs=[pl.BlockSpec((None, tm, D), lambda b, t: (b, t, 0)),
                  _const_spec(wcat.shape), _const_spec(bf.shape), _const_spec(wa2h.shape),
                  _const_spec(wa2l.shape), _const_spec(ba2.shape)],
        out_specs=[o[0] for o in outs],
        out_shape=[o[1] for o in outs],
        scratch_shapes=[pltpu.VMEM((1, LANES), F32)],
        compiler_params=_params("parallel", "arbitrary"),
        name="proj",
    )(x, wcat, bf, wa2h, wa2l, ba2)


def _cumsum_kernel(lf_ref, nc_ref, carry_ref):
    @pl.when(pl.program_id(1) == 0)
    def _():
        carry_ref[...] = jnp.zeros_like(carry_ref)

    x = lf_ref[...]
    tl = x.shape[1]
    r = lax.broadcasted_iota(jnp.int32, (tl, tl), 0)
    c = lax.broadcasted_iota(jnp.int32, (tl, tl), 1)
    tri = jnp.where(r <= c, 1.0, 0.0).astype(BF16)
    hi, mid, lo = _split3(x)
    cs = _dot(hi, tri) + _dot(mid, tri) + _dot(lo, tri) + carry_ref[...]
    nc_ref[...] = -cs
    carry_ref[...] = carry_ref[...] + jnp.sum(x, axis=1, keepdims=True)


def _neg_cumsum(lf_t, tl):
    B, H, L = lf_t.shape
    return pl.pallas_call(
        _cumsum_kernel,
        grid=(B, L // tl),
        in_specs=[pl.BlockSpec((None, H, tl), lambda b, t: (b, 0, t))],
        out_specs=pl.BlockSpec((None, H, tl), lambda b, t: (b, 0, t)),
        out_shape=jax.ShapeDtypeStruct((B, H, L), F32),
        scratch_shapes=[pltpu.VMEM((H, 1), F32)],
        compiler_params=_params("parallel", "arbitrary"),
        name="cumsum",
    )(lf_t)


FOX_KEY_PAD = 512
FOX_BIAS_PARTS = 3
FOX_UNROLL = 4
FOX_SUM_ROWS = 16


def _fox_bias_lanes(x, carry_ref, first):
    @pl.when(first)
    def _():
        carry_ref[...] = jnp.zeros_like(carry_ref)

    tl = x.shape[0]
    blk = min(tl, LANES)
    r = lax.broadcasted_iota(jnp.int32, (blk, blk), 0)
    c = lax.broadcasted_iota(jnp.int32, (blk, blk), 1)
    tri = jnp.where(c <= r, 1.0, 0.0).astype(BF16)
    hi, mid, lo = _split3(x)
    run = carry_ref[...]
    pieces = []
    for k in range(tl // blk):
        rows = slice(k * blk, (k + 1) * blk)
        pieces.append(_dot(tri, hi[rows]) + _dot(tri, mid[rows]) + _dot(tri, lo[rows]) + run)
        run = pieces[-1][blk - 1:blk, :]
    carry_ref[...] = run
    cs = jnp.concatenate(pieces, axis=0) if len(pieces) > 1 else pieces[0]
    parts = [p.astype(F32) for p in _split3(cs * (-LOG2E))]
    lane = lax.broadcasted_iota(jnp.int32, (1, LANES), 1)
    out = jnp.zeros_like(x)
    for j in reversed(range(FOX_BIAS_PARTS)):
        moved = parts[j] if j == 0 else pltpu.roll(parts[j], j * H_A, 1)
        out = jnp.where(jnp.logical_and(lane >= j * H_A, lane < (j + 1) * H_A), moved, out)
    return out


def _fox_prompt_kernel(q_ref, k_ref, kb_ref, vt_ref, o_ref, m_ref, acc_ref,
                       sa_ref, sb_ref, bma_ref, bmb_ref, *, tq, tk):
    i = pl.program_id(2)
    lane = lax.broadcasted_iota(jnp.int32, (1, LANES), 1)
    q = q_ref[...]
    zero = jnp.zeros_like(q)
    qa = []
    for h in range(2):
        own = (lane < DH_A) if h == 0 else (lane >= DH_A)
        head = 2 * pl.program_id(1) + h
        ones = jnp.logical_and(jnp.bitwise_and(lane, H_A - 1) == head, lane < FOX_BIAS_PARTS * H_A)
        ones = jnp.broadcast_to(jnp.where(ones, 1.0, 0.0).astype(q.dtype), q.shape)
        qa.append(jnp.concatenate([jnp.where(own, q, zero), ones], axis=1))
    m_ref[...] = jnp.full_like(m_ref, -jnp.inf)
    acc_ref[...] = jnp.zeros_like(acc_ref)
    sum_rows = jnp.ones((FOX_SUM_ROWS, tk), BF16)
    per = tq // tk

    def scores(j, s_ref, bm_ref):
        start = pl.multiple_of(j * tk, tk)
        ka = jnp.concatenate([k_ref[pl.ds(start, tk), :], kb_ref[pl.ds(start, tk), :]], axis=1)
        for h in range(2):
            st = _dot_nt(ka, qa[h])
            s_ref[h] = st
            bm_ref[h] = jnp.max(st, axis=0, keepdims=True)

    def soft_pv(j, s_ref, bm_ref, diag):
        for h in range(2):
            st = s_ref[h]
            if diag is not None:
                r = lax.broadcasted_iota(jnp.int32, (tk, tq), 0)
                c = lax.broadcasted_iota(jnp.int32, (tk, tq), 1)
                st = jnp.where(r + diag * tk <= c, st, -jnp.inf)
                bm = jnp.max(st, axis=0, keepdims=True)
            else:
                bm = bm_ref[h]
            m_old = m_ref[h]
            m_new = jnp.maximum(m_old, bm)
            p = jnp.exp2(st - m_new)
            a = jnp.exp2(m_old - m_new)
            vt = jnp.concatenate([vt_ref[j, h * DH_A:(h + 1) * DH_A, :], sum_rows], axis=0)
            acc_ref[h] = a * acc_ref[h] + _dot(vt, p.astype(BF16))
            m_ref[h] = m_new

    slots = ((sa_ref, bma_ref), (sb_ref, bmb_ref))

    def run(j0, diags, prefetch_after):
        for u, diag in enumerate(diags):
            if u + 1 < len(diags) or prefetch_after:
                scores(j0 + u + 1, *slots[(u + 1) % 2])
            soft_pv(j0 + u, *slots[u % 2], diag)

    scores(0, *slots[0])

    def body(t, carry):
        run(FOX_UNROLL * t, [None] * FOX_UNROLL, True)
        return carry

    full = i * per
    log_unroll = FOX_UNROLL.bit_length() - 1
    lax.fori_loop(0, lax.shift_right_logical(full, log_unroll), body, 0)
    rest = jnp.bitwise_and(full, FOX_UNROLL - 1)
    for n in range(0, FOX_UNROLL, per):
        @pl.when(rest == n)
        def _(n=n):
            run(full - n, [None] * n + list(range(per)), False)

    ot = jnp.concatenate([acc_ref[h, :DH_A, :] / acc_ref[h, DH_A:DH_A + 1, :] for h in range(2)],
                         axis=0)
    o_ref[...] = ot.T.astype(o_ref.dtype)


def _fox_prompt(qa, ka, kbias, vt, tq, tk):
    B, T, _ = qa.shape
    nk = T // tk
    return pl.pallas_call(
        functools.partial(_fox_prompt_kernel, tq=tq, tk=tk),
        grid=(B, H_A // 2, T // tq),
        in_specs=[pl.BlockSpec((None, tq, LANES), lambda b, p, i: (b, i, p)),
                  pl.BlockSpec((None, T, LANES), lambda b, p, i: (b, 0, p)),
                  pl.BlockSpec((None, T, LANES), lambda b, p, i: (b, 0, 0)),
                  pl.BlockSpec((None, None, nk, LANES, tk), lambda b, p, i: (b, p, 0, 0, 0))],
        out_specs=pl.BlockSpec((None, tq, LANES), lambda b, p, i: (b, i, p)),
        out_shape=jax.ShapeDtypeStruct((B, T, FOX_W), BF16),
        scratch_shapes=[pltpu.VMEM((2, 1, tq), F32),
                        pltpu.VMEM((2, DH_A + FOX_SUM_ROWS, tq), F32),
                        pltpu.VMEM((2, tk, tq), F32), pltpu.VMEM((2, tk, tq), F32),
                        pltpu.VMEM((2, 1, tq), F32), pltpu.VMEM((2, 1, tq), F32)],
        compiler_params=_params("parallel", "parallel", "arbitrary"),
        name="fox_prompt",
    )(qa, ka, kbias, vt)


def _fox_sample_kernel(q_ref, k_ref, v_ref, nc_ref, o_ref, *, past):
    first = lax.broadcasted_iota(jnp.int32, (1, LANES), 1) < DH_A
    q = q_ref[...]
    zero = jnp.zeros_like(q)
    tq = q.shape[0]
    kb = k_ref[...]
    vb = v_ref[...]
    lk = kb.shape[0]
    r = lax.broadcasted_iota(jnp.int32, (tq, lk), 0)
    c = lax.broadcasted_iota(jnp.int32, (tq, lk), 1)
    visible = c <= r + past
    outs = []
    for h in range(2):
        qh = jnp.where(first, q, zero) if h == 0 else jnp.where(first, zero, q)
        s = _dot_nt(qh, kb) + nc_ref[h:h + 1, :] * LOG2E
        s = jnp.where(visible, s, -jnp.inf)
        m = jnp.max(s, axis=1, keepdims=True)
        p = jnp.exp2(s - m)
        l = jnp.sum(p, axis=1, keepdims=True)
        outs.append(_dot(p.astype(BF16), vb) / l)
    o_ref[...] = jnp.where(first, outs[0], outs[1]).astype(o_ref.dtype)


def _fox_sample(qa, k_all, v_all, nc, past):
    B, Tq, _ = qa.shape
    Lk = k_all.shape[1]
    return pl.pallas_call(
        functools.partial(_fox_sample_kernel, past=past),
        grid=(B, H_A // 2),
        in_specs=[pl.BlockSpec((None, Tq, LANES), lambda b, p: (b, 0, p)),
                  pl.BlockSpec((None, Lk, LANES), lambda b, p: (b, 0, p)),
                  pl.BlockSpec((None, Lk, LANES), lambda b, p: (b, 0, p)),
                  pl.BlockSpec((None, None, 2, Lk), lambda b, p: (b, p, 0, 0))],
        out_specs=pl.BlockSpec((None, Tq, LANES), lambda b, p: (b, 0, p)),
        out_shape=jax.ShapeDtypeStruct((B, Tq, FOX_W), BF16),
        compiler_params=_params("parallel", "parallel"),
        name="fox_sample",
    )(qa, k_all, v_all, nc)


def _gla_kernel(q_ref, k_ref, v_ref, la_ref, rb_ref, g_ref, s0_ref, hb_ref, sfin_ref, st_ref,
                *, chunk, sub):
    t = pl.program_id(1)

    @pl.when(t == 0)
    def _():
        for h in range(H_B):
            st_ref[h] = s0_ref[h].T

    tc = q_ref.shape[0]
    nsub = chunk // sub
    nchunk = tc // chunk
    r = lax.broadcasted_iota(jnp.int32, (tc, tc), 0)
    c = lax.broadcasted_iota(jnp.int32, (tc, tc), 1)
    log_chunk = chunk.bit_length() - 1
    log_sub = sub.bit_length() - 1
    same_chunk = jnp.right_shift(r, log_chunk) == jnp.right_shift(c, log_chunk)
    near = jnp.logical_and(c <= r, same_chunk)
    far = jnp.right_shift(c, log_chunk) < jnp.right_shift(r, log_chunk)
    tri = jnp.where(c <= r, 1.0, 0.0).astype(BF16)
    hi, mid, lo = _split3(la_ref[...])
    b_all = (_dot(tri, hi) + _dot(tri, mid) + _dot(tri, lo)) * LOG2E

    row = lax.broadcasted_iota(jnp.int32, (tc, 1), 0)
    row_sub = jnp.bitwise_and(jnp.right_shift(row, log_sub), nsub - 1)
    row_chunk = jnp.right_shift(row, log_chunk)

    def rows_of(refs, n):
        return jnp.concatenate([jnp.broadcast_to(rf, (n, DK_B)) for rf in refs], axis=0)

    def zeros(n):
        return jnp.zeros((n, DK_B), F32)

    for h in range(H_B):
        kc = slice(h * DK_B, (h + 1) * DK_B)
        vc = slice(h * DV_B, (h + 1) * DV_B)
        b = b_all[:, kc]
        q = q_ref[:, kc]
        k = k_ref[:, kc]
        v = v_ref[:, vc]
        b_last = b[tc - 1:tc, :]
        st = st_ref[h]
        sub_refs = [b[i * sub:i * sub + 1, :] for i in range(tc // sub)]
        chunk_refs = [b[ci * chunk:ci * chunk + 1, :] for ci in range(nchunk)]

        qt = q * jnp.exp2(b - rows_of(sub_refs, sub))
        qx = [jnp.where(row_sub == i, qt, 0.0) for i in range(nsub)]
        kx = []
        for i in range(nsub):
            pieces = []
            for ci in range(nchunk):
                lo_row, n = ci * chunk, (i + 1) * sub
                pieces.append(k[lo_row:lo_row + n] * jnp.exp2(sub_refs[ci * nsub + i] - b[lo_row:lo_row + n]))
                if n < chunk:
                    pieces.append(zeros(chunk - n))
            kx.append(jnp.concatenate(pieces, axis=0) if len(pieces) > 1 else pieces[0])
        a = jnp.where(near, _dot_nt(jnp.concatenate(qx, axis=1).astype(BF16),
                                    jnp.concatenate(kx, axis=1).astype(BF16)), 0.0)
        if nchunk > 1:
            qc = q * jnp.exp2(b - rows_of(chunk_refs, chunk))
            qx = [jnp.where(row_chunk == ci, qc, 0.0) for ci in range(1, nchunk)]
            kx = []
            for ci in range(1, nchunk):
                n = ci * chunk
                kx.append(jnp.concatenate([k[:n] * jnp.exp2(chunk_refs[ci] - b[:n]), zeros(tc - n)], axis=0))
            a = jnp.where(far, _dot_nt(jnp.concatenate(qx, axis=1).astype(BF16),
                                       jnp.concatenate(kx, axis=1).astype(BF16)), a)
        o = _dot(a.astype(BF16), v) + _dot_nt((q * jnp.exp2(b)).astype(BF16), st.astype(BF16))
        kd = (k * jnp.exp2(b_last - b)).astype(BF16)
        st_ref[h] = st * jnp.exp2(b_last) + _dot_tn(v, kd)
        ms = jnp.mean(o * o, axis=-1, keepdims=True)
        on = o * lax.rsqrt(ms + LN_EPS) * g_ref[:, vc]
        rbv = rb_ref[:, vc]
        hb_ref[:, vc] = (on * (rbv * _sigmoid(rbv))).astype(hb_ref.dtype)

    @pl.when(t == pl.num_programs(1) - 1)
    def _():
        for h in range(H_B):
            sfin_ref[h] = st_ref[h].T


def _gla(qb, kb, vb, la, rb, g, s0_t, tc, chunk):
    B, T, _ = qb.shape
    sub = min(GLA_SUB, chunk)

    def row(n):
        return pl.BlockSpec((None, tc, n), lambda b, t: (b, t, 0))

    st_spec = pl.BlockSpec((None, H_B, DK_B, DV_B), lambda b, t: (b, 0, 0, 0))
    return pl.pallas_call(
        functools.partial(_gla_kernel, chunk=chunk, sub=sub),
        grid=(B, T // tc),
        in_specs=[row(GLA_K), row(GLA_K), row(GLA_V), row(GLA_K), row(GLA_V),
                  _const_spec(g.shape), st_spec],
        out_specs=[row(GLA_V), st_spec],
        out_shape=[jax.ShapeDtypeStruct((B, T, GLA_V), BF16),
                   jax.ShapeDtypeStruct((B, H_B, DK_B, DV_B), F32)],
        scratch_shapes=[pltpu.VMEM((H_B, DV_B, DK_B), F32)],
        compiler_params=_params("parallel", "arbitrary"),
        name="gla",
    )(qb, kb, vb, la, rb, g, s0_t)


def _merge_kernel(oa_ref, hb_ref, ga_ref, gb_ref, x_ref, wa_ref, wb_ref, wo_ref, g_ref, b_ref, x1_ref):
    ya = _dot(oa_ref[...], wa_ref[...])
    yb = _dot(hb_ref[...], wb_ref[...])
    merged = _sigmoid(ga_ref[...].astype(F32)) * ya + _sigmoid(gb_ref[...].astype(F32)) * yb
    z = ALPHA * x_ref[...] + _dot(merged.astype(BF16), wo_ref[...])
    x1_ref[...] = _layer_norm(z, g_ref[...], b_ref[...])


def _merge(oa, hb, ga, gb, x, wa, wb, wo, g, b, tm):
    B, T, D = x.shape

    def row(n):
        return pl.BlockSpec((None, tm, n), lambda bb, t: (bb, t, 0))

    return pl.pallas_call(
        _merge_kernel,
        grid=(B, T // tm),
        in_specs=[row(FOX_W), row(GLA_V), row(D), row(D), row(D),
                  _const_spec(wa.shape), _const_spec(wb.shape), _const_spec(wo.shape),
                  _const_spec(g.shape), _const_spec(b.shape)],
        out_specs=row(D),
        out_shape=jax.ShapeDtypeStruct((B, T, D), F32),
        compiler_params=_params("parallel", "parallel"),
        name="merge",
    )(oa, hb, ga, gb, x, wa, wb, wo, g, b)


FFN_GROUP = 256


def _gelu_tanh(x):
    return 0.5 * x * (1.0 + jnp.tanh(math.sqrt(2.0 / math.pi) * (x + 0.044715 * (x * x * x))))


def _ffn_pack(a):
    return jnp.concatenate([a[..., k * D_FF + j * FFN_GROUP:k * D_FF + (j + 1) * FFN_GROUP]
                            for j in range(D_FF // FFN_GROUP) for k in range(2)], axis=-1)


def _ffn_unpack(a):
    lead = a.shape[:-1]
    return a.reshape(*lead, D_FF // FFN_GROUP, 2, FFN_GROUP).swapaxes(-3, -2).reshape(*lead, 2 * D_FF)


def _ffn_kernel(x1_ref, p_ref, cprev_ref, wup_ref, cw_ref, cb_ref, wdn_ref, g2_ref, b2_ref,
                wpl_ref, wplg_ref, g3_ref, b3_ref, y_ref, cnew_ref, carry_ref):
    @pl.when(pl.program_id(1) == 0)
    def _():
        carry_ref[...] = cprev_ref[...]

    x1 = x1_ref[...]
    tm = x1.shape[0]
    up = _dot(x1.astype(BF16), wup_ref[...])
    prev = carry_ref[...]
    row = lax.broadcasted_iota(jnp.int32, (tm, 1), 0)
    up_m1 = jnp.where(row == 0, prev[1:2, :], pltpu.roll(up, 1, 0))
    up_m2 = jnp.where(row == 0, prev[0:1, :], jnp.where(row == 1, prev[1:2, :], pltpu.roll(up, 2, 0)))
    conv = cb_ref[...] + cw_ref[0:1, :] * up_m2 + cw_ref[1:2, :] * up_m1 + cw_ref[2:3, :] * up
    last2 = up[tm - 2:tm, :]
    carry_ref[...] = last2
    cnew_ref[...] = last2
    hgl = jnp.concatenate(
        [conv[:, 2 * j * FFN_GROUP:(2 * j + 1) * FFN_GROUP]
         * _gelu_tanh(conv[:, (2 * j + 1) * FFN_GROUP:(2 * j + 2) * FFN_GROUP])
         for j in range(D_FF // FFN_GROUP)], axis=1).astype(BF16)
    z2 = ALPHA * x1 + _dot(hgl, wdn_ref[...])
    x2 = _layer_norm(z2, g2_ref[...], b2_ref[...])
    e = _dot(p_ref[...].astype(BF16), wpl_ref[...]) * _sigmoid(_dot(x2.astype(BF16), wplg_ref[...]))
    y_ref[...] = _layer_norm(ALPHA * x2 + e, g3_ref[...], b3_ref[...])


def _ffn(x1, p, cprev, wup, cw, cb, wdn, g2, b2, wpl, wplg, g3, b3, tm):
    B, T, D = x1.shape

    def row(n):
        return pl.BlockSpec((None, tm, n), lambda bb, t: (bb, t, 0))

    cspec = pl.BlockSpec((None, CONV_W - 1, 2 * D_FF), lambda bb, t: (bb, 0, 0))
    consts = (wup, cw, cb, wdn, g2, b2, wpl, wplg, g3, b3)
    return pl.pallas_call(
        _ffn_kernel,
        grid=(B, T // tm),
        in_specs=[row(D), row(PLE_DIM), cspec] + [_const_spec(c.shape) for c in consts],
        out_specs=[row(D), cspec],
        out_shape=[jax.ShapeDtypeStruct((B, T, D), F32),
                   jax.ShapeDtypeStruct((B, CONV_W - 1, 2 * D_FF), F32)],
        scratch_shapes=[pltpu.VMEM((CONV_W - 1, 2 * D_FF), F32)],
        compiler_params=_params("parallel", "arbitrary"),
        name="ffn",
    )(x1, p, cprev, *consts)


def _pick(T, want):
    t = min(T, want)
    while T % t:
        t //= 2
    return t


def _pack_w_in(w_in):
    pts, acc = {}, 0
    for name, n in (("qa", FOX_W), ("ka", FOX_W), ("va", FOX_W), ("fa", H_A), ("qb", GLA_K), ("kb", GLA_K),
                    ("vb", GLA_V), ("rb", GLA_V), ("a1", GLA_RANK), ("ga", D_MODEL), ("gb", D_MODEL)):
        pts[name] = (acc, n)
        acc += n
    cols = []
    for name in _SEG:
        lo, n = pts[name]
        piece = w_in[:, lo:lo + n]
        if n < _SEG[name][1]:
            piece = jnp.pad(piece, ((0, 0), (0, _SEG[name][1] - n)))
        cols.append(piece)
    return jnp.concatenate(cols, axis=1).astype(BF16)


def _layer(x, p, fox_past, s0, conv_prev, wts):
    (w_in, b_fgate, w_a2, b_a2, g_gla, w_a_out, w_b_out, w_o, ln1_g, ln1_b, w_up, conv_w, conv_b,
     w_down, ln2_g, ln2_b, w_pl, w_plg, ln3_g, ln3_b) = wts
    B, T, D = x.shape
    row2 = lambda a: a.reshape(1, -1).astype(F32)

    wcat = _pack_w_in(w_in)
    bf = jnp.pad(row2(b_fgate), ((0, 0), (0, LANES - H_A)))
    wa2 = jnp.pad(w_a2.astype(F32), ((0, LANES - GLA_RANK), (0, 0)))
    wa2h = wa2.astype(BF16)
    wa2l = (wa2 - wa2h.astype(F32)).astype(BF16)

    tq, tk = _pick(T, 512), _pick(T, 512)
    fold = fox_past is not None and T < LANES
    rows = (lambda a: a.reshape(1, B * T, a.shape[-1])) if fold else (lambda a: a)
    unrows = (lambda a: a.reshape(B, T, a.shape[-1])) if fold else (lambda a: a)
    proj_out = _proj(rows(x), wcat, bf, wa2h, wa2l, row2(b_a2), _pick(B * T if fold else T, 512),
                     vt_block=tk if fox_past is None else None)
    proj_out = list(proj_out)
    proj_out[5] = jnp.transpose(proj_out[5], (0, 2, 1))
    if fold:
        proj_out = [unrows(o) for o in proj_out]
    (qa, ka_f, ka_b, va_f, va_b, logf, logf_pad, qb, kb, vb, rb, la, ga, gb) = proj_out

    if fox_past is None:
        oa = _fox_prompt(qa, ka_b, logf_pad, va_b, tq, tk)
    else:
        past_k, past_v, past_logf = fox_past
        P = past_k.shape[1]
        lk = -(-(P + T) // FOX_KEY_PAD) * FOX_KEY_PAD
        padk = lambda a: jnp.pad(a, ((0, 0), (0, lk - P - T), (0, 0)))
        k_all = padk(jnp.concatenate([past_k.reshape(B, P, FOX_W).astype(BF16), ka_b], axis=1))
        v_all = padk(jnp.concatenate([past_v.reshape(B, P, FOX_W).astype(BF16), va_b], axis=1))
        lf_all = padk(jnp.concatenate([past_logf.astype(F32), logf], axis=1))
        nc = _neg_cumsum(jnp.transpose(lf_all, (0, 2, 1)), _pick(lk, 512))
        oa = _fox_sample(qa, k_all, v_all, nc.reshape(B, H_A // 2, 2, lk), P)

    chunk = min(GLA_CHUNK, T)
    tc = _pick(T, 4 * chunk)
    hb, s_fin = _gla(qb, kb, vb, la, rb, row2(g_gla), s0.astype(F32), tc, chunk)

    x1 = unrows(_merge(rows(oa), rows(hb), rows(ga), rows(gb), rows(x), w_a_out.astype(BF16),
                       w_b_out.astype(BF16), w_o.astype(BF16), row2(ln1_g), row2(ln1_b),
                       _pick(B * T if fold else T, 1024)))
    y, conv_new = _ffn(x1, p, _ffn_pack(conv_prev.astype(F32)), _ffn_pack(w_up.astype(BF16)),
                       _ffn_pack(conv_w.astype(F32)), _ffn_pack(row2(conv_b)),
                       w_down.astype(BF16), row2(ln2_g), row2(ln2_b), w_pl.astype(BF16),
                       w_plg.astype(BF16), row2(ln3_g), row2(ln3_b), _pick(T, 512))
    conv_new = _ffn_unpack(conv_new)
    k_out = ka_f.reshape(B, T, H_A, DH_A)
    v_out = va_f.reshape(B, T, H_A, DH_A)
    return y, k_out, v_out, logf, s_fin, conv_new


def kernel(x_prompt, x_sample, cache_fox_k, cache_fox_v, cache_fox_logf, state_gla, cache_ffn_conv, p_prompt, p_sample, w_in, b_fgate, w_a2, b_a2, g_gla, w_a_out, w_b_out, w_o, ln1_g, ln1_b, w_up, conv_w, conv_b, w_down, ln2_g, ln2_b, w_pl, w_plg, ln3_g, ln3_b):
    hp, hs = x_prompt, x_sample
    outs_p = [[] for _ in range(5)]
    outs_s = [[] for _ in range(5)]
    for i in range(DEPTH):
        wts = (w_in[i], b_fgate[i], w_a2[i], b_a2[i], g_gla[i], w_a_out[i], w_b_out[i], w_o[i],
               ln1_g[i], ln1_b[i], w_up[i], conv_w[i], conv_b[i], w_down[i], ln2_g[i], ln2_b[i],
               w_pl[i], w_plg[i], ln3_g[i], ln3_b[i])
        Bp = hp.shape[0]
        s0_p = jnp.zeros((Bp, H_B, DK_B, DV_B), F32)
        c0_p = jnp.zeros((Bp, CONV_W - 1, 2 * D_FF), F32)
        hp, *rest_p = _layer(hp, p_prompt[i], None, s0_p, c0_p, wts)
        hs, *rest_s = _layer(hs, p_sample[i], (cache_fox_k[i], cache_fox_v[i], cache_fox_logf[i]),
                             state_gla[i], cache_ffn_conv[i], wts)
        for dst, src in ((outs_p, rest_p), (outs_s, rest_s)):
            for lst, val in zip(dst, src):
                lst.append(val)
    return (hp, hs, *[jnp.stack(l) for l in outs_p], *[jnp.stack(l) for l in outs_s])
```

```python
import functools
import math

import jax
import jax.numpy as jnp
from jax import lax
from jax.experimental import pallas as pl
from jax.experimental.pallas import tpu as pltpu

F32 = jnp.float32
BF16 = jnp.bfloat16

D_MODEL = 1024
H_A, DH_A = 8, 64
FOX_W = H_A * DH_A
H_B, DK_B, DV_B = 4, 128, 256
GLA_K = H_B * DK_B
GLA_V = H_B * DV_B
GLA_RANK = 16
GLA_TAU = 16
D_FF = 2816
CONV_W = 3
PLE_DIM = 256
LN_EPS = 1e-5
DEPTH = 1
ALPHA = (2 * DEPTH) ** 0.25
GLA_CHUNK = 64
GLA_SUB = 16

LANES = 128
LOG2E = math.log2(math.e)
V7X_VMEM_LIMIT_BYTES = 56 * 1024 * 1024

_SEG = {}
_off = 0
for _name, _n in (("qa", FOX_W), ("ka", FOX_W), ("va", FOX_W), ("qb", GLA_K), ("kb", GLA_K),
                  ("vb", GLA_V), ("rb", GLA_V), ("ga", D_MODEL), ("gb", D_MODEL),
                  ("fa", LANES), ("a1", LANES)):
    _SEG[_name] = (_off, _n)
    _off += _n
W_CAT = _off


def _params(*sem):
    return pltpu.CompilerParams(dimension_semantics=sem, vmem_limit_bytes=V7X_VMEM_LIMIT_BYTES)


def _const_spec(shape):
    nd = len(shape)
    return pl.BlockSpec(shape, lambda *_: (0,) * nd, pipeline_mode=pl.Buffered(1))


def _log_sigmoid(z):
    return jnp.minimum(z, 0.0) - jnp.log(1.0 + jnp.exp(-jnp.abs(z)))


def _sigmoid(z):
    return 1.0 / (1.0 + jnp.exp(-z))


def _split3(a):
    hi = a.astype(BF16)
    r = a - hi.astype(F32)
    mid = r.astype(BF16)
    lo = (r - mid.astype(F32)).astype(BF16)
    return hi, mid, lo


def _layer_norm(z, g, b):
    mu = jnp.mean(z, axis=-1, keepdims=True)
    zc = z - mu
    var = jnp.mean(zc * zc, axis=-1, keepdims=True)
    return zc * lax.rsqrt(var + LN_EPS) * g + b


def _dot(a, b):
    return jnp.dot(a, b, preferred_element_type=F32)


def _dot_nt(a, b):
    return lax.dot_general(a, b, (((1,), (1,)), ((), ())), preferred_element_type=F32)


def _dot_tn(a, b):
    return lax.dot_general(a, b, (((0,), (0,)), ((), ())), preferred_element_type=F32)


def _proj_kernel(x_ref, w_ref, bf_ref, wa2h_ref, wa2l_ref, ba2_ref,
                 qa_ref, kaf_ref, kab_ref, vaf_ref, vab_ref, logf_ref, logfp_ref,
                 qb_ref, kb_ref, vb_ref, rb_ref, la_ref, ga_ref, gb_ref, carry_ref, *, values_transposed):
    xb = x_ref[...].astype(BF16)

    def seg(name):
        lo, n = _SEG[name]
        return _dot(xb, w_ref[:, lo:lo + n])

    logf = _log_sigmoid(seg("fa") + bf_ref[...])
    qa_ref[...] = (seg("qa") * (DH_A ** -0.5 * LOG2E)).astype(BF16)
    ka = seg("ka")
    kaf_ref[...] = ka
    kab_ref[...] = ka.astype(BF16)
    va = seg("va")
    vaf_ref[...] = va
    if values_transposed:
        for p in range(H_A // 2):
            vab_ref[p] = va[:, p * LANES:(p + 1) * LANES].T.astype(BF16)
    else:
        vab_ref[...] = va.astype(BF16)
    qb_ref[...] = seg("qb") * (DK_B ** -0.5)
    kb_ref[...] = seg("kb")
    vb_ref[...] = seg("vb").astype(BF16)
    rb_ref[...] = seg("rb")
    ga_ref[...] = seg("ga").astype(BF16)
    gb_ref[...] = seg("gb").astype(BF16)
    a1 = seg("a1")
    a1h = a1.astype(BF16)
    a1l = (a1 - a1h.astype(F32)).astype(BF16)
    z = _dot(a1h, wa2h_ref[...]) + _dot(a1l, wa2h_ref[...]) + _dot(a1h, wa2l_ref[...]) + ba2_ref[...]
    la_ref[...] = _log_sigmoid(z) * (1.0 / GLA_TAU)
    logf_ref[...] = logf.T[:H_A, :]
    if values_transposed:
        logfp_ref[...] = _fox_bias_lanes(logf, carry_ref, pl.program_id(1) == 0).astype(logfp_ref.dtype)
    else:
        logfp_ref[...] = logf


def _proj(x, wcat, bf, wa2h, wa2l, ba2, tm, vt_block=None):
    B, T, D = x.shape
    grid = (B, T // tm)

    def row(n, dt):
        return pl.BlockSpec((None, tm, n), lambda b, t: (b, t, 0)), jax.ShapeDtypeStruct((B, T, n), dt)

    outs = [row(FOX_W, BF16), row(FOX_W, F32), row(FOX_W, BF16), row(FOX_W, F32), row(FOX_W, BF16),
            row(H_A, F32), row(LANES, F32), row(GLA_K, F32), row(GLA_K, F32), row(GLA_V, BF16), row(GLA_V, F32),
            row(GLA_K, F32), row(D_MODEL, BF16), row(D_MODEL, BF16)]
    outs[5] = (pl.BlockSpec((None, H_A, tm), lambda b, t: (b, 0, t)), jax.ShapeDtypeStruct((B, H_A, T), F32))
    if vt_block is not None:
        per = vt_block // tm
        outs[4] = (pl.BlockSpec((None, H_A // 2, None, LANES, tm), lambda b, t: (b, 0, t // per, 0, t % per)),
                   jax.ShapeDtypeStruct((B, H_A // 2, T // vt_block, LANES, vt_block), BF16))
        outs[6] = row(LANES, BF16)
    return pl.pallas_call(
        functools.partial(_proj_kernel, values_transposed=vt_block is not None),
        grid=grid,
        in_specs=[pl.BlockSpec((None, tm, D), lambda b, t: (b, t, 0)),
                  _const_spec(wcat.shape), _const_spec(bf.shape), _const_spec(wa2h.shape),
                  _const_spec(wa2l.shape), _const_spec(ba2.shape)],
        out_specs=[o[0] for o in outs],
        out_shape=[o[1] for o in outs],
        scratch_shapes=[pltpu.VMEM((1, LANES), F32)],
        compiler_params=_params("parallel", "arbitrary"),
        name="proj",
    )(x, wcat, bf, wa2h, wa2l, ba2)


def _cumsum_kernel(lf_ref, nc_ref, carry_ref):
    @pl.when(pl.program_id(1) == 0)
    def _():
        carry_ref[...] = jnp.zeros_like(carry_ref)

    x = lf_ref[...]
    tl = x.shape[1]
    r = lax.broadcasted_iota(jnp.int32, (tl, tl), 0)
    c = lax.broadcasted_iota(jnp.int32, (tl, tl), 1)
    tri = jnp.where(r <= c, 1.0, 0.0).astype(BF16)
    hi, mid, lo = _split3(x)
    cs = _dot(hi, tri) + _dot(mid, tri) + _dot(lo, tri) + carry_ref[...]
    nc_ref[...] = -cs
    carry_ref[...] = carry_ref[...] + jnp.sum(x, axis=1, keepdims=True)


def _neg_cumsum(lf_t, tl):
    B, H, L = lf_t.shape
    return pl.pallas_call(
        _cumsum_kernel,
        grid=(B, L // tl),
        in_specs=[pl.BlockSpec((None, H, tl), lambda b, t: (b, 0, t))],
        out_specs=pl.BlockSpec((None, H, tl), lambda b, t: (b, 0, t)),
        out_shape=jax.ShapeDtypeStruct((B, H, L), F32),
        scratch_shapes=[pltpu.VMEM((H, 1), F32)],
        compiler_params=_params("parallel", "arbitrary"),
        name="cumsum",
    )(lf_t)


FOX_KEY_PAD = 512
FOX_BIAS_PARTS = 3
FOX_UNROLL = 4
FOX_SUM_ROWS = 16


def _fox_bias_lanes(x, carry_ref, first):
    @pl.when(first)
    def _():
        carry_ref[...] = jnp.zeros_like(carry_ref)

    tl = x.shape[0]
    row = lax.broadcasted_iota(jnp.int32, (tl, 1), 0)
    cs = x
    shift = 1
    while shift < tl:
        cs = cs + jnp.where(row >= shift, pltpu.roll(cs, shift, 0), 0.0)
        shift *= 2
    cs = cs + carry_ref[...]
    carry_ref[...] = cs[tl - 1:tl, :]
    parts =[p.astype(F32) for p in _split3(cs * (-LOG2E))]
    lane = lax.broadcasted_iota(jnp.int32, (1, LANES), 1)
    out = jnp.zeros_like(x)
    for j in reversed(range(FOX_BIAS_PARTS)):
        moved = parts[j] if j == 0 else pltpu.roll(parts[j], j * H_A, 1)
        out = jnp.where(jnp.logical_and(lane >= j * H_A, lane < (j + 1) * H_A), moved, out)
    return out


def _fox_prompt_kernel(q_ref, k_ref, kb_ref, vt_ref, o_ref, m_ref, acc_ref,
                       sa_ref, sb_ref, bma_ref, bmb_ref, *, tq, tk):
    i = pl.program_id(2)
    lane = lax.broadcasted_iota(jnp.int32, (1, LANES), 1)
    q = q_ref[...]
    zero = jnp.zeros_like(q)
    qa = []
    for h in range(2):
        own = (lane < DH_A) if h == 0 else (lane >= DH_A)
        head = 2 * pl.program_id(1) + h
        ones = jnp.logical_and(jnp.bitwise_and(lane, H_A - 1) == head, lane < FOX_BIAS_PARTS * H_A)
        ones = jnp.broadcast_to(jnp.where(ones, 1.0, 0.0).astype(q.dtype), q.shape)
        qa.append(jnp.concatenate([jnp.where(own, q, zero), ones], axis=1))
    m_ref[...] = jnp.full_like(m_ref, -jnp.inf)
    acc_ref[...] = jnp.zeros_like(acc_ref)
    sum_rows = jnp.ones((FOX_SUM_ROWS, tk), BF16)
    per = tq // tk

    def scores(j, s_ref, bm_ref):
        start = pl.multiple_of(j * tk, tk)
        ka = jnp.concatenate([k_ref[pl.ds(start, tk), :], kb_ref[pl.ds(start, tk), :]], axis=1)
        for h in range(2):
            st = _dot_nt(ka, qa[h])
            s_ref[h] = st
            bm_ref[h] = jnp.max(st, axis=0, keepdims=True)

    def soft_pv(j, s_ref, bm_ref, diag):
        for h in range(2):
            st = s_ref[h]
            if diag is not None:
                r = lax.broadcasted_iota(jnp.int32, (tk, tq), 0)
                c = lax.broadcasted_iota(jnp.int32, (tk, tq), 1)
                st = jnp.where(r + diag * tk <= c, st, -jnp.inf)
                bm = jnp.max(st, axis=0, keepdims=True)
            else:
                bm = bm_ref[h]
            m_old = m_ref[h]
            m_new = jnp.maximum(m_old, bm)
            p = jnp.exp2(st - m_new)
            a = jnp.exp2(m_old - m_new)
            vt = jnp.concatenate([vt_ref[j, h * DH_A:(h + 1) * DH_A, :], sum_rows], axis=0)
            acc_ref[h] = a * acc_ref[h] + _dot(vt, p.astype(BF16))
            m_ref[h] = m_new

    slots = ((sa_ref, bma_ref), (sb_ref, bmb_ref))

    def run(j0, diags, prefetch_after):
        for u, diag in enumerate(diags):
            if u + 1 < len(diags) or prefetch_after:
                scores(j0 + u + 1, *slots[(u + 1) % 2])
            soft_pv(j0 + u, *slots[u % 2], diag)

    scores(0, *slots[0])

    def body(t, carry):
        run(FOX_UNROLL * t, [None] * FOX_UNROLL, True)
        return carry

    full = i * per
    log_unroll = FOX_UNROLL.bit_length() - 1
    lax.fori_loop(0, lax.shift_right_logical(full, log_unroll), body, 0)
    rest = jnp.bitwise_and(full, FOX_UNROLL - 1)
    for n in range(0, FOX_UNROLL, per):
        @pl.when(rest == n)
        def _(n=n):
            run(full - n, [None] * n + list(range(per)), False)

    ot = jnp.concatenate([acc_ref[h, :DH_A, :] / acc_ref[h, DH_A:DH_A + 1, :] for h in range(2)],
                         axis=0)
    o_ref[...] = ot.T.astype(o_ref.dtype)


def _fox_prompt(qa, ka, kbias, vt, tq, tk):
    B, T, _ = qa.shape
    nk = T // tk
    return pl.pallas_call(
        functools.partial(_fox_prompt_kernel, tq=tq, tk=tk),
        grid=(B, H_A // 2, T // tq),
        in_specs=[pl.BlockSpec((None, tq, LANES), lambda b, p, i: (b, i, p)),
                  pl.BlockSpec((None, T, LANES), lambda b, p, i: (b, 0, p)),
                  pl.BlockSpec((None, T, LANES), lambda b, p, i: (b, 0, 0)),
                  pl.BlockSpec((None, None, nk, LANES, tk), lambda b, p, i: (b, p, 0, 0, 0))],
        out_specs=pl.BlockSpec((None, tq, LANES), lambda b, p, i: (b, i, p)),
        out_shape=jax.ShapeDtypeStruct((B, T, FOX_W), BF16),
        scratch_shapes=[pltpu.VMEM((2, 1, tq), F32),
                        pltpu.VMEM((2, DH_A + FOX_SUM_ROWS, tq), F32),
                        pltpu.VMEM((2, tk, tq), F32), pltpu.VMEM((2, tk, tq), F32),
                        pltpu.VMEM((2, 1, tq), F32), pltpu.VMEM((2, 1, tq), F32)],
        compiler_params=_params("parallel", "parallel", "arbitrary"),
        name="fox_prompt",
    )(qa, ka, kbias, vt)


def _fox_sample_kernel(q_ref, k_ref, v_ref, nc_ref, o_ref, *, past):
    first = lax.broadcasted_iota(jnp.int32, (1, LANES), 1) < DH_A
    q = q_ref[...]
    zero = jnp.zeros_like(q)
    tq = q.shape[0]
    kb = k_ref[...]
    vb = v_ref[...]
    lk = kb.shape[0]
    r = lax.broadcasted_iota(jnp.int32, (tq, lk), 0)
    c = lax.broadcasted_iota(jnp.int32, (tq, lk), 1)
    visible = c <= r + past
    outs = []
    for h in range(2):
        qh = jnp.where(first, q, zero) if h == 0 else jnp.where(first, zero, q)
        s = _dot_nt(qh, kb) + nc_ref[h:h + 1, :] * LOG2E
        s = jnp.where(visible, s, -jnp.inf)
        m = jnp.max(s, axis=1, keepdims=True)
        p = jnp.exp2(s - m)
        l = jnp.sum(p, axis=1, keepdims=True)
        outs.append(_dot(p.astype(BF16), vb) / l)
    o_ref[...] = jnp.where(first, outs[0], outs[1]).astype(o_ref.dtype)


def _fox_sample(qa, k_all, v_all, nc, past):
    B, Tq, _ = qa.shape
    Lk = k_all.shape[1]
    return pl.pallas_call(
        functools.partial(_fox_sample_kernel, past=past),
        grid=(B, H_A // 2),
        in_specs=[pl.BlockSpec((None, Tq, LANES), lambda b, p: (b, 0, p)),
                  pl.BlockSpec((None, Lk, LANES), lambda b, p: (b, 0, p)),
                  pl.BlockSpec((None, Lk, LANES), lambda b, p: (b, 0, p)),
                  pl.BlockSpec((None, None, 2, Lk), lambda b, p: (b, p, 0, 0))],
        out_specs=pl.BlockSpec((None, Tq, LANES), lambda b, p: (b, 0, p)),
        out_shape=jax.ShapeDtypeStruct((B, Tq, FOX_W), BF16),
        compiler_params=_params("parallel", "parallel"),
        name="fox_sample",
    )(qa, k_all, v_all, nc)


def _gla_kernel(q_ref, k_ref, v_ref, la_ref, rb_ref, g_ref, s0_ref, hb_ref, sfin_ref, st_ref,
                *, chunk, sub):
    t = pl.program_id(1)

    @pl.when(t == 0)
    def _():
        for h in range(H_B):
            st_ref[h] = s0_ref[h].T

    tc = q_ref.shape[0]
    nsub = chunk // sub
    nchunk = tc // chunk
    r = lax.broadcasted_iota(jnp.int32, (tc, tc), 0)
    c = lax.broadcasted_iota(jnp.int32, (tc, tc), 1)
    log_chunk = chunk.bit_length() - 1
    log_sub = sub.bit_length() - 1
    same_chunk = jnp.right_shift(r, log_chunk) == jnp.right_shift(c, log_chunk)
    near = jnp.logical_and(c <= r, same_chunk)
    far = jnp.right_shift(c, log_chunk) < jnp.right_shift(r, log_chunk)
    tri = jnp.where(c <= r, 1.0, 0.0).astype(BF16)
    hi, mid, lo = _split3(la_ref[...])
    b_all = (_dot(tri, hi) + _dot(tri, mid) + _dot(tri, lo)) * LOG2E

    row = lax.broadcasted_iota(jnp.int32, (tc, 1), 0)
    row_sub = jnp.bitwise_and(jnp.right_shift(row, log_sub), nsub - 1)
    row_chunk = jnp.right_shift(row, log_chunk)

    def rows_of(refs, n):
        return jnp.concatenate([jnp.broadcast_to(rf, (n, DK_B)) for rf in refs], axis=0)

    def zeros(n):
        return jnp.zeros((n, DK_B), F32)

    for h in range(H_B):
        kc = slice(h * DK_B, (h + 1) * DK_B)
        vc = slice(h * DV_B, (h + 1) * DV_B)
        b = b_all[:, kc]
        q = q_ref[:, kc]
        k = k_ref[:, kc]
        v = v_ref[:, vc]
        b_last = b[tc - 1:tc, :]
        st = st_ref[h]
        sub_refs = [b[i * sub:i * sub + 1, :] for i in range(tc // sub)]
        chunk_refs = [b[ci * chunk:ci * chunk + 1, :] for ci in range(nchunk)]

        qt = q * jnp.exp2(b - rows_of(sub_refs, sub))
        qx = [jnp.where(row_sub == i, qt, 0.0) for i in range(nsub)]
        kx = []
        for i in range(nsub):
            pieces = []
            for ci in range(nchunk):
                lo_row, n = ci * chunk, (i + 1) * sub
                pieces.append(k[lo_row:lo_row + n] * jnp.exp2(sub_refs[ci * nsub + i] - b[lo_row:lo_row + n]))
                if n < chunk:
                    pieces.append(zeros(chunk - n))
            kx.append(jnp.concatenate(pieces, axis=0) if len(pieces) > 1 else pieces[0])
        a = jnp.where(near, _dot_nt(jnp.concatenate(qx, axis=1).astype(BF16),
                                    jnp.concatenate(kx, axis=1).astype(BF16)), 0.0)
        if nchunk > 1:
            qc = q * jnp.exp2(b - rows_of(chunk_refs, chunk))
            qx = [jnp.where(row_chunk == ci, qc, 0.0) for ci in range(1, nchunk)]
            kx = []
            for ci in range(1, nchunk):
                n = ci * chunk
                kx.append(jnp.concatenate([k[:n] * jnp.exp2(chunk_refs[ci] - b[:n]), zeros(tc - n)], axis=0))
            a = jnp.where(far, _dot_nt(jnp.concatenate(qx, axis=1).astype(BF16),
                                       jnp.concatenate(kx, axis=1).astype(BF16)), a)
        o = _dot(a.astype(BF16), v) + _dot_nt((q * jnp.exp2(b)).astype(BF16), st.astype(BF16))
        kd = (k * jnp.exp2(b_last - b)).astype(BF16)
        st_ref[h] = st * jnp.exp2(b_last) + _dot_tn(v, kd)
        ms = jnp.mean(o * o, axis=-1, keepdims=True)
        on = o * lax.rsqrt(ms + LN_EPS) * g_ref[:, vc]
        rbv = rb_ref[:, vc]
        hb_ref[:, vc] = (on * (rbv * _sigmoid(rbv))).astype(hb_ref.dtype)

    @pl.when(t == pl.num_programs(1) - 1)
    def _():
        for h in range(H_B):
            sfin_ref[h] = st_ref[h].T


def _gla(qb, kb, vb, la, rb, g, s0_t, tc, chunk):
    B, T, _ = qb.shape
    sub = min(GLA_SUB, chunk)

    def row(n):
        return pl.BlockSpec((None, tc, n), lambda b, t: (b, t, 0))

    st_spec = pl.BlockSpec((None, H_B, DK_B, DV_B), lambda b, t: (b, 0, 0, 0))
    return pl.pallas_call(
        functools.partial(_gla_kernel, chunk=chunk, sub=sub),
        grid=(B, T // tc),
        in_specs=[row(GLA_K), row(GLA_K), row(GLA_V), row(GLA_K), row(GLA_V),
                  _const_spec(g.shape), st_spec],
        out_specs=[row(GLA_V), st_spec],
        out_shape=[jax.ShapeDtypeStruct((B, T, GLA_V), BF16),
                   jax.ShapeDtypeStruct((B, H_B, DK_B, DV_B), F32)],
        scratch_shapes=[pltpu.VMEM((H_B, DV_B, DK_B), F32)],
        compiler_params=_params("parallel", "arbitrary"),
        name="gla",
    )(qb, kb, vb, la, rb, g, s0_t)


def _merge_kernel(oa_ref, hb_ref, ga_ref, gb_ref, x_ref, wa_ref, wb_ref, wo_ref, g_ref, b_ref, x1_ref):
    ya = _dot(oa_ref[...], wa_ref[...])
    yb = _dot(hb_ref[...], wb_ref[...])
    merged = _sigmoid(ga_ref[...].astype(F32)) * ya + _sigmoid(gb_ref[...].astype(F32)) * yb
    z = ALPHA * x_ref[...] + _dot(merged.astype(BF16), wo_ref[...])
    x1_ref[...] = _layer_norm(z, g_ref[...], b_ref[...])


def _merge(oa, hb, ga, gb, x, wa, wb, wo, g, b, tm):
    B, T, D = x.shape

    def row(n):
        return pl.BlockSpec((None, tm, n), lambda bb, t: (bb, t, 0))

    return pl.pallas_call(
        _merge_kernel,
        grid=(B, T // tm),
        in_specs=[row(FOX_W), row(GLA_V), row(D), row(D), row(D),
                  _const_spec(wa.shape), _const_spec(wb.shape), _const_spec(wo.shape),
                  _const_spec(g.shape), _const_spec(b.shape)],
        out_specs=row(D),
        out_shape=jax.ShapeDtypeStruct((B, T, D), F32),
        compiler_params=_params("parallel", "parallel"),
        name="merge",
    )(oa, hb, ga, gb, x, wa, wb, wo, g, b)


FFN_GROUP = 256


def _gelu_tanh(x):
    return 0.5 * x * (1.0 + jnp.tanh(math.sqrt(2.0 / math.pi) * (x + 0.044715 * (x * x * x))))


def _ffn_pack(a):
    return jnp.concatenate([a[..., k * D_FF + j * FFN_GROUP:k * D_FF + (j + 1) * FFN_GROUP]
                            for j in range(D_FF // FFN_GROUP) for k in range(2)], axis=-1)


def _ffn_unpack(a):
    lead = a.shape[:-1]
    return a.reshape(*lead, D_FF // FFN_GROUP, 2, FFN_GROUP).swapaxes(-3, -2).reshape(*lead, 2 * D_FF)


def _ffn_kernel(x1_ref, p_ref, cprev_ref, wup_ref, cw_ref, cb_ref, wdn_ref, g2_ref, b2_ref,
                wpl_ref, wplg_ref, g3_ref, b3_ref, y_ref, cnew_ref, carry_ref):
    @pl.when(pl.program_id(1) == 0)
    def _():
        carry_ref[...] = cprev_ref[...]

    x1 = x1_ref[...]
    tm = x1.shape[0]
    up = _dot(x1.astype(BF16), wup_ref[...])
    prev = carry_ref[...]
    row = lax.broadcasted_iota(jnp.int32, (tm, 1), 0)
    up_m1 = jnp.where(row == 0, prev[1:2, :], pltpu.roll(up, 1, 0))
    up_m2 = jnp.where(row == 0, prev[0:1, :], jnp.where(row == 1, prev[1:2, :], pltpu.roll(up, 2, 0)))
    conv = cb_ref[...] + cw_ref[0:1, :] * up_m2 + cw_ref[1:2, :] * up_m1 + cw_ref[2:3, :] * up
    last2 = up[tm - 2:tm, :]
    carry_ref[...] = last2
    cnew_ref[...] = last2
    hgl = jnp.concatenate(
        [conv[:, 2 * j * FFN_GROUP:(2 * j + 1) * FFN_GROUP]
         * _gelu_tanh(conv[:, (2 * j + 1) * FFN_GROUP:(2 * j + 2) * FFN_GROUP])
         for j in range(D_FF // FFN_GROUP)], axis=1).astype(BF16)
    z2 = ALPHA * x1 + _dot(hgl, wdn_ref[...])
    x2 = _layer_norm(z2, g2_ref[...], b2_ref[...])
    e = _dot(p_ref[...].astype(BF16), wpl_ref[...]) * _sigmoid(_dot(x2.astype(BF16), wplg_ref[...]))
    y_ref[...] = _layer_norm(ALPHA * x2 + e, g3_ref[...], b3_ref[...])


def _ffn(x1, p, cprev, wup, cw, cb, wdn, g2, b2, wpl, wplg, g3, b3, tm):
    B, T, D = x1.shape

    def row(n):
        return pl.BlockSpec((None, tm, n), lambda bb, t: (bb, t, 0))

    cspec = pl.BlockSpec((None, CONV_W - 1, 2 * D_FF), lambda bb, t: (bb, 0, 0))
    consts = (wup, cw, cb, wdn, g2, b2, wpl, wplg, g3, b3)
    return pl.pallas_call(
        _ffn_kernel,
        grid=(B, T // tm),
        in_specs=[row(D), row(PLE_DIM), cspec] + [_const_spec(c.shape) for c in consts],
        out_specs=[row(D), cspec],
        out_shape=[jax.ShapeDtypeStruct((B, T, D), F32),
                   jax.ShapeDtypeStruct((B, CONV_W - 1, 2 * D_FF), F32)],
        scratch_shapes=[pltpu.VMEM((CONV_W - 1, 2 * D_FF), F32)],
        compiler_params=_params("parallel", "arbitrary"),
        name="ffn",
    )(x1, p, cprev, *consts)


def _pick(T, want):
    t = min(T, want)
    while T % t:
        t //= 2
    return t


def _pack_w_in(w_in):
    pts, acc = {}, 0
    for name, n in (("qa", FOX_W), ("ka", FOX_W), ("va", FOX_W), ("fa", H_A), ("qb", GLA_K), ("kb", GLA_K),
                    ("vb", GLA_V), ("rb", GLA_V), ("a1", GLA_RANK), ("ga", D_MODEL), ("gb", D_MODEL)):
        pts[name] = (acc, n)
        acc += n
    runs = []
    for name in _SEG:
        lo, n = pts[name]
        pad = _SEG[name][1] - n
        if runs and runs[-1][1] == lo and runs[-1][2] == 0:
            runs[-1][1:] = [lo + n, pad]
        else:
            runs.append([lo, lo + n, pad])
    cols = [jnp.pad(w_in[:, lo:hi], ((0, 0), (0, pad))) if pad else w_in[:, lo:hi] for lo, hi, pad in runs]
    return jnp.concatenate(cols, axis=1).astype(BF16)


def _layer(x, p, fox_past, s0, conv_prev, wts):
    (w_in, b_fgate, w_a2, b_a2, g_gla, w_a_out, w_b_out, w_o, ln1_g, ln1_b, w_up, conv_w, conv_b,
     w_down, ln2_g, ln2_b, w_pl, w_plg, ln3_g, ln3_b) = wts
    B, T, D = x.shape
    row2 = lambda a: a.reshape(1, -1).astype(F32)

    wcat = _pack_w_in(w_in)
    bf = jnp.pad(row2(b_fgate), ((0, 0), (0, LANES - H_A)))
    wa2 = jnp.pad(w_a2.astype(F32), ((0, LANES - GLA_RANK), (0, 0)))
    wa2h = wa2.astype(BF16)
    wa2l = (wa2 - wa2h.astype(F32)).astype(BF16)

    tq, tk = _pick(T, 512), _pick(T, 512)
    fold = fox_past is not None and T < LANES
    rows = (lambda a: a.reshape(1, B * T, a.shape[-1])) if fold else (lambda a: a)
    unrows = (lambda a: a.reshape(B, T, a.shape[-1])) if fold else (lambda a: a)
    proj_out = _proj(rows(x), wcat, bf, wa2h, wa2l, row2(b_a2), _pick(B * T if fold else T, 512),
                     vt_block=tk if fox_past is None else None)
    proj_out = list(proj_out)
    proj_out[5] = jnp.transpose(proj_out[5], (0, 2, 1))
    if fold:
        proj_out = [unrows(o) for o in proj_out]
    (qa, ka_f, ka_b, va_f, va_b, logf, logf_pad, qb, kb, vb, rb, la, ga, gb) = proj_out

    if fox_past is None:
        oa = _fox_prompt(qa, ka_b, logf_pad, va_b, tq, tk)
    else:
        past_k, past_v, past_logf = fox_past
        P = past_k.shape[1]
        lk = -(-(P + T) // FOX_KEY_PAD) * FOX_KEY_PAD
        padk = lambda a: jnp.pad(a, ((0, 0), (0, lk - P - T), (0, 0)))
        k_all = padk(jnp.concatenate([past_k.reshape(B, P, FOX_W).astype(BF16), ka_b], axis=1))
        v_all = padk(jnp.concatenate([past_v.reshape(B, P, FOX_W).astype(BF16), va_b], axis=1))
        lf_all = padk(jnp.concatenate([past_logf.astype(F32), logf], axis=1))
        nc = _neg_cumsum(jnp.transpose(lf_all, (0, 2, 1)), _pick(lk, 512))
        oa = _fox_sample(qa, k_all, v_all, nc.reshape(B, H_A // 2, 2, lk), P)

    chunk = min(GLA_CHUNK, T)
    tc = _pick(T, 4 * chunk)
    hb, s_fin = _gla(qb, kb, vb, la, rb, row2(g_gla), s0.astype(F32), tc, chunk)

    x1 = unrows(_merge(rows(oa), rows(hb), rows(ga), rows(gb), rows(x), w_a_out.astype(BF16),
                       w_b_out.astype(BF16), w_o.astype(BF16), row2(ln1_g), row2(ln1_b),
                       _pick(B * T if fold else T, 1024)))
    y, conv_new = _ffn(x1, p, _ffn_pack(conv_prev.astype(F32)), _ffn_pack(w_up.astype(BF16)),
                       _ffn_pack(conv_w.astype(F32)), _ffn_pack(row2(conv_b)),
                       w_down.astype(BF16), row2(ln2_g), row2(ln2_b), w_pl.astype(BF16),
                       w_plg.astype(BF16), row2(ln3_g), row2(ln3_b), _pick(T, 512))
    conv_new = _ffn_unpack(conv_new)
    k_out = ka_f.reshape(B, T, H_A, DH_A)
    v_out = va_f.reshape(B, T, H_A, DH_A)
    return y, k_out, v_out, logf, s_fin, conv_new


def kernel(x_prompt, x_sample, cache_fox_k, cache_fox_v, cache_fox_logf, state_gla, cache_ffn_conv, p_prompt, p_sample, w_in, b_fgate, w_a2, b_a2, g_gla, w_a_out, w_b_out, w_o, ln1_g, ln1_b, w_up, conv_w, conv_b, w_down, ln2_g, ln2_b, w_pl, w_plg, ln3_g, ln3_b):
    hp, hs = x_prompt, x_sample
    outs_p = [[] for _ in range(5)]
    outs_s = [[] for _ in range(5)]
    for i in range(DEPTH):
        wts = (w_in[i], b_fgate[i], w_a2[i], b_a2[i], g_gla[i], w_a_out[i], w_b_out[i], w_o[i],
               ln1_g[i], ln1_b[i], w_up[i], conv_w[i], conv_b[i], w_down[i], ln2_g[i], ln2_b[i],
               w_pl[i], w_plg[i], ln3_g[i], ln3_b[i])
        Bp = hp.shape[0]
        s0_p = jnp.zeros((Bp, H_B, DK_B, DV_B), F32)
        c0_p = jnp.zeros((Bp, CONV_W - 1, 2 * D_FF), F32)
        hp, *rest_p = _layer(hp, p_prompt[i], None, s0_p, c0_p, wts)
        hs, *rest_s = _layer(hs, p_sample[i], (cache_fox_k[i], cache_fox_v[i], cache_fox_logf[i]),
                             state_gla[i], cache_ffn_conv[i], wts)
        for dst, src in ((outs_p, rest_p), (outs_s, rest_s)):
            for lst, val in zip(dst, src):
                lst.append(val)
    return (hp, hs, *[jnp.stack(l) for l in outs_p], *[jnp.stack(l) for l in outs_s])
```

```python
import functools
import math

import jax
import jax.numpy as jnp
from jax import lax
from jax.experimental import pallas as pl
from jax.experimental.pallas import tpu as pltpu

F32 = jnp.float32
BF16 = jnp.bfloat16

D_MODEL = 1024
H_A, DH_A = 8, 64
FOX_W = H_A * DH_A
H_B, DK_B, DV_B = 4, 128, 256
GLA_K = H_B * DK_B
GLA_V = H_B * DV_B
GLA_RANK = 16
GLA_TAU = 16
D_FF = 2816
CONV_W = 3
PLE_DIM = 256
LN_EPS = 1e-5
DEPTH = 1
ALPHA = (2 * DEPTH) ** 0.25
GLA_CHUNK = 64
GLA_SUB = 16

LANES = 128
LOG2E = math.log2(math.e)
V7X_VMEM_LIMIT_BYTES = 56 * 1024 * 1024

_SEG = {}
_off = 0
for _name, _n in (("qa", FOX_W), ("ka", FOX_W), ("va", FOX_W), ("qb", GLA_K), ("kb", GLA_K),
                  ("vb", GLA_V), ("rb", GLA_V), ("ga", D_MODEL), ("gb", D_MODEL),
                  ("fa", LANES), ("a1", LANES)):
    _SEG[_name] = (_off, _n)
    _off += _n


def _params(*sem):
    return pltpu.CompilerParams(dimension_semantics=sem, vmem_limit_bytes=V7X_VMEM_LIMIT_BYTES)


def _const_spec(shape):
    nd = len(shape)
    return pl.BlockSpec(shape, lambda *_: (0,) * nd, pipeline_mode=pl.Buffered(1))


def _log_sigmoid(z):
    return jnp.minimum(z, 0.0) - jnp.log(1.0 + jnp.exp(-jnp.abs(z)))


def _sigmoid(z):
    return 1.0 / (1.0 + jnp.exp(-z))


def _split3(a):
    hi = a.astype(BF16)
    r = a - hi.astype(F32)
    mid = r.astype(BF16)
    lo = (r - mid.astype(F32)).astype(BF16)
    return hi, mid, lo


def _layer_norm(z, g, b):
    mu = jnp.mean(z, axis=-1, keepdims=True)
    zc = z - mu
    var = jnp.mean(zc * zc, axis=-1, keepdims=True)
    return zc * lax.rsqrt(var + LN_EPS) * g + b


def _dot(a, b):
    return jnp.dot(a, b, preferred_element_type=F32)


def _dot_nt(a, b):
    return lax.dot_general(a, b, (((1,), (1,)), ((), ())), preferred_element_type=F32)


def _dot_tn(a, b):
    return lax.dot_general(a, b, (((0,), (0,)), ((), ())), preferred_element_type=F32)


def _proj_kernel(x_ref, w_ref, bf_ref, wa2h_ref, wa2l_ref, ba2_ref,
                 qa_ref, kaf_ref, kab_ref, vaf_ref, vab_ref, logf_ref, logfp_ref,
                 qb_ref, kb_ref, vb_ref, rb_ref, la_ref, ga_ref, gb_ref, carry_ref, *, values_transposed):
    xb = x_ref[...].astype(BF16)

    def seg(name):
        lo, n = _SEG[name]
        return _dot(xb, w_ref[:, lo:lo + n])

    logf = _log_sigmoid(seg("fa") + bf_ref[...])
    qa_ref[...] = (seg("qa") * (DH_A ** -0.5 * LOG2E)).astype(BF16)
    ka = seg("ka")
    kaf_ref[...] = ka
    kab_ref[...] = ka.astype(BF16)
    va = seg("va")
    vaf_ref[...] = va
    if values_transposed:
        for p in range(H_A // 2):
            vab_ref[p] = va[:, p * LANES:(p + 1) * LANES].T.astype(BF16)
    else:
        vab_ref[...] = va.astype(BF16)
    qb_ref[...] = seg("qb") * (DK_B ** -0.5)
    kb_ref[...] = seg("kb")
    vb_ref[...] = seg("vb").astype(BF16)
    rb_ref[...] = seg("rb")
    ga_ref[...] = seg("ga").astype(BF16)
    gb_ref[...] = seg("gb").astype(BF16)
    a1 = seg("a1")
    a1h = a1.astype(BF16)
    a1l = (a1 - a1h.astype(F32)).astype(BF16)
    z = _dot(a1h, wa2h_ref[...]) + _dot(a1l, wa2h_ref[...]) + _dot(a1h, wa2l_ref[...]) + ba2_ref[...]
    la_ref[...] = _log_sigmoid(z) * (1.0 / GLA_TAU)
    logf_ref[...] = logf.T[:H_A, :]
    if values_transposed:
        logfp_ref[...] = _fox_bias_lanes(logf, carry_ref, pl.program_id(1) == 0).astype(logfp_ref.dtype)
    else:
        logfp_ref[...] = logf


def _proj(x, wcat, bf, wa2h, wa2l, ba2, tm, vt_block=None):
    B, T, D = x.shape
    grid = (B, T // tm)

    def row(n, dt):
        return pl.BlockSpec((None, tm, n), lambda b, t: (b, t, 0)), jax.ShapeDtypeStruct((B, T, n), dt)

    outs = [row(FOX_W, BF16), row(FOX_W, F32), row(FOX_W, BF16), row(FOX_W, F32), row(FOX_W, BF16),
            row(H_A, F32), row(LANES, F32), row(GLA_K, F32), row(GLA_K, F32), row(GLA_V, BF16), row(GLA_V, F32),
            row(GLA_K, F32), row(D_MODEL, BF16), row(D_MODEL, BF16)]
    outs[5] = (pl.BlockSpec((None, H_A, tm), lambda b, t: (b, 0, t)), jax.ShapeDtypeStruct((B, H_A, T), F32))
    if vt_block is not None:
        per = vt_block // tm
        outs[4] = (pl.BlockSpec((None, H_A // 2, None, LANES, tm), lambda b, t: (b, 0, t // per, 0, t % per)),
                   jax.ShapeDtypeStruct((B, H_A // 2, T // vt_block, LANES, vt_block), BF16))
        outs[6] = row(LANES, BF16)
    return pl.pallas_call(
        functools.partial(_proj_kernel, values_transposed=vt_block is not None),
        grid=grid,
        in_specs=[pl.BlockSpec((None, tm, D), lambda b, t: (b, t, 0)),
                  _const_spec(wcat.shape), _const_spec(bf.shape), _const_spec(wa2h.shape),
                  _const_spec(wa2l.shape), _const_spec(ba2.shape)],
        out_specs=[o[0] for o in outs],
        out_shape=[o[1] for o in outs],
        scratch_shapes=[pltpu.VMEM((1, LANES), F32)],
        compiler_params=_params("parallel", "arbitrary"),
        name="proj",
    )(x, wcat, bf, wa2h, wa2l, ba2)


def _cumsum_kernel(lf_ref, nc_ref, carry_ref):
    @pl.when(pl.program_id(1) == 0)
    def _():
        carry_ref[...] = jnp.zeros_like(carry_ref)

    x = lf_ref[...]
    tl = x.shape[1]
    r = lax.broadcasted_iota(jnp.int32, (tl, tl), 0)
    c = lax.broadcasted_iota(jnp.int32, (tl, tl), 1)
    tri = jnp.where(r <= c, 1.0, 0.0).astype(BF16)
    hi, mid, lo = _split3(x)
    cs = _dot(hi, tri) + _dot(mid, tri) + _dot(lo, tri) + carry_ref[...]
    nc_ref[...] = -cs
    carry_ref[...] = carry_ref[...] + jnp.sum(x, axis=1, keepdims=True)


def _neg_cumsum(lf_t, tl):
    B, H, L = lf_t.shape
    return pl.pallas_call(
        _cumsum_kernel,
        grid=(B, L // tl),
        in_specs=[pl.BlockSpec((None, H, tl), lambda b, t: (b, 0, t))],
        out_specs=pl.BlockSpec((None, H, tl), lambda b, t: (b, 0, t)),
        out_shape=jax.ShapeDtypeStruct((B, H, L), F32),
        scratch_shapes=[pltpu.VMEM((H, 1), F32)],
        compiler_params=_params("parallel", "arbitrary"),
        name="cumsum",
    )(lf_t)


FOX_KEY_PAD = 512
FOX_BIAS_PARTS = 3
FOX_UNROLL = 4
FOX_SUM_ROWS = 16


def _fox_bias_lanes(x, carry_ref, first):
    @pl.when(first)
    def _():
        carry_ref[...] = jnp.zeros_like(carry_ref)

    tl = x.shape[0]
    row = lax.broadcasted_iota(jnp.int32, (tl, 1), 0)
    cs = x
    shift = 1
    while shift < tl:
        cs = cs + jnp.where(row >= shift, pltpu.roll(cs, shift, 0), 0.0)
        shift *= 2
    cs = cs + carry_ref[...]
    carry_ref[...] = cs[tl - 1:tl, :]
    parts =[p.astype(F32) for p in _split3(cs * (-LOG2E))]
    lane = lax.broadcasted_iota(jnp.int32, (1, LANES), 1)
    out = jnp.zeros_like(x)
    for j in reversed(range(FOX_BIAS_PARTS)):
        moved = parts[j] if j == 0 else pltpu.roll(parts[j], j * H_A, 1)
        out = jnp.where(jnp.logical_and(lane >= j * H_A, lane < (j + 1) * H_A), moved, out)
    return out


def _fox_prompt_kernel(q_ref, k_ref, kb_ref, vt_ref, o_ref, m_ref, acc_ref,
                       sa_ref, sb_ref, bma_ref, bmb_ref, *, tq, tk):
    i = pl.program_id(2)
    lane = lax.broadcasted_iota(jnp.int32, (1, LANES), 1)
    q = q_ref[...]
    zero = jnp.zeros_like(q)
    qa = []
    for h in range(2):
        own = (lane < DH_A) if h == 0 else (lane >= DH_A)
        head = 2 * pl.program_id(1) + h
        ones = jnp.logical_and(jnp.bitwise_and(lane, H_A - 1) == head, lane < FOX_BIAS_PARTS * H_A)
        ones = jnp.broadcast_to(jnp.where(ones, 1.0, 0.0).astype(q.dtype), q.shape)
        qa.append(jnp.concatenate([jnp.where(own, q, zero), ones], axis=1))
    m_ref[...] = jnp.full_like(m_ref, -jnp.inf)
    acc_ref[...] = jnp.zeros_like(acc_ref)
    sum_rows = jnp.ones((FOX_SUM_ROWS, tk), BF16)
    per = tq // tk

    def scores(j, s_ref, bm_ref):
        start = pl.multiple_of(j * tk, tk)
        ka = jnp.concatenate([k_ref[pl.ds(start, tk), :], kb_ref[pl.ds(start, tk), :]], axis=1)
        for h in range(2):
            st = _dot_nt(ka, qa[h])
            s_ref[h] = st
            bm_ref[h] = jnp.max(st, axis=0, keepdims=True)

    def soft_pv(j, s_ref, bm_ref, diag):
        for h in range(2):
            st = s_ref[h]
            if diag is not None:
                r = lax.broadcasted_iota(jnp.int32, (tk, tq), 0)
                c = lax.broadcasted_iota(jnp.int32, (tk, tq), 1)
                st = jnp.where(r + diag * tk <= c, st, -jnp.inf)
                bm = jnp.max(st, axis=0, keepdims=True)
            else:
                bm = bm_ref[h]
            m_old = m_ref[h]
            m_new = jnp.maximum(m_old, bm)
            p = jnp.exp2(st - m_new)
            a = jnp.exp2(m_old - m_new)
            vt = jnp.concatenate([vt_ref[j, h * DH_A:(h + 1) * DH_A, :], sum_rows], axis=0)
            acc_ref[h] = a * acc_ref[h] + _dot(vt, p.astype(BF16))
            m_ref[h] = m_new

    slots = ((sa_ref, bma_ref), (sb_ref, bmb_ref))

    def run(j0, diags, prefetch_after):
        for u, diag in enumerate(diags):
            if u + 1 < len(diags) or prefetch_after:
                scores(j0 + u + 1, *slots[(u + 1) % 2])
            soft_pv(j0 + u, *slots[u % 2], diag)

    scores(0, *slots[0])

    def body(t, carry):
        run(FOX_UNROLL * t, [None] * FOX_UNROLL, True)
        return carry

    full = i * per
    log_unroll = FOX_UNROLL.bit_length() - 1
    lax.fori_loop(0, lax.shift_right_logical(full, log_unroll), body, 0)
    rest = jnp.bitwise_and(full, FOX_UNROLL - 1)
    for n in range(0, FOX_UNROLL, per):
        @pl.when(rest == n)
        def _(n=n):
            run(full - n, [None] * n + list(range(per)), False)

    ot = jnp.concatenate([acc_ref[h, :DH_A, :] / acc_ref[h, DH_A:DH_A + 1, :] for h in range(2)],
                         axis=0)
    o_ref[...] = ot.T.astype(o_ref.dtype)


def _fox_prompt(qa, ka, kbias, vt, tq, tk):
    B, T, _ = qa.shape
    nk = T // tk
    return pl.pallas_call(
        functools.partial(_fox_prompt_kernel, tq=tq, tk=tk),
        grid=(B, H_A // 2, T // tq),
        in_specs=[pl.BlockSpec((None, tq, LANES), lambda b, p, i: (b, i, p)),
                  pl.BlockSpec((None, T, LANES), lambda b, p, i: (b, 0, p)),
                  pl.BlockSpec((None, T, LANES), lambda b, p, i: (b, 0, 0)),
                  pl.BlockSpec((None, None, nk, LANES, tk), lambda b, p, i: (b, p, 0, 0, 0))],
        out_specs=pl.BlockSpec((None, tq, LANES), lambda b, p, i: (b, i, p)),
        out_shape=jax.ShapeDtypeStruct((B, T, FOX_W), BF16),
        scratch_shapes=[pltpu.VMEM((2, 1, tq), F32),
                        pltpu.VMEM((2, DH_A + FOX_SUM_ROWS, tq), F32),
                        pltpu.VMEM((2, tk, tq), F32), pltpu.VMEM((2, tk, tq), F32),
                        pltpu.VMEM((2, 1, tq), F32), pltpu.VMEM((2, 1, tq), F32)],
        compiler_params=_params("parallel", "parallel", "arbitrary"),
        name="fox_prompt",
    )(qa, ka, kbias, vt)


def _fox_sample_kernel(q_ref, k_ref, v_ref, nc_ref, o_ref, *, past):
    first = lax.broadcasted_iota(jnp.int32, (1, LANES), 1) < DH_A
    q = q_ref[...]
    zero = jnp.zeros_like(q)
    tq = q.shape[0]
    kb = k_ref[...]
    vb = v_ref[...]
    lk = kb.shape[0]
    r = lax.broadcasted_iota(jnp.int32, (tq, lk), 0)
    c = lax.broadcasted_iota(jnp.int32, (tq, lk), 1)
    visible = c <= r + past
    outs = []
    for h in range(2):
        qh = jnp.where(first, q, zero) if h == 0 else jnp.where(first, zero, q)
        s = _dot_nt(qh, kb) + nc_ref[h:h + 1, :] * LOG2E
        s = jnp.where(visible, s, -jnp.inf)
        m = jnp.max(s, axis=1, keepdims=True)
        p = jnp.exp2(s - m)
        l = jnp.sum(p, axis=1, keepdims=True)
        outs.append(_dot(p.astype(BF16), vb) / l)
    o_ref[...] = jnp.where(first, outs[0], outs[1]).astype(o_ref.dtype)


def _fox_sample(qa, k_all, v_all, nc, past):
    B, Tq, _ = qa.shape
    Lk = k_all.shape[1]
    return pl.pallas_call(
        functools.partial(_fox_sample_kernel, past=past),
        grid=(B, H_A // 2),
        in_specs=[pl.BlockSpec((None, Tq, LANES), lambda b, p: (b, 0, p)),
                  pl.BlockSpec((None, Lk, LANES), lambda b, p: (b, 0, p)),
                  pl.BlockSpec((None, Lk, LANES), lambda b, p: (b, 0, p)),
                  pl.BlockSpec((None, None, 2, Lk), lambda b, p: (b, p, 0, 0))],
        out_specs=pl.BlockSpec((None, Tq, LANES), lambda b, p: (b, 0, p)),
        out_shape=jax.ShapeDtypeStruct((B, Tq, FOX_W), BF16),
        compiler_params=_params("parallel", "parallel"),
        name="fox_sample",
    )(qa, k_all, v_all, nc)


def _gla_kernel(q_ref, k_ref, v_ref, la_ref, rb_ref, g_ref, s0_ref, hb_ref, sfin_ref, st_ref,
                *, chunk, sub):
    t = pl.program_id(1)

    @pl.when(t == 0)
    def _():
        for h in range(H_B):
            st_ref[h] = s0_ref[h].T

    tc = q_ref.shape[0]
    nsub = chunk // sub
    nchunk = tc // chunk
    r = lax.broadcasted_iota(jnp.int32, (tc, tc), 0)
    c = lax.broadcasted_iota(jnp.int32, (tc, tc), 1)
    log_chunk = chunk.bit_length() - 1
    log_sub = sub.bit_length() - 1
    same_chunk = jnp.right_shift(r, log_chunk) == jnp.right_shift(c, log_chunk)
    near = jnp.logical_and(c <= r, same_chunk)
    far = jnp.right_shift(c, log_chunk) < jnp.right_shift(r, log_chunk)
    tri = jnp.where(c <= r, 1.0, 0.0).astype(BF16)
    hi, mid, lo = _split3(la_ref[...])
    b_all = (_dot(tri, hi) + _dot(tri, mid) + _dot(tri, lo)) * LOG2E

    row = lax.broadcasted_iota(jnp.int32, (tc, 1), 0)
    row_sub = jnp.bitwise_and(jnp.right_shift(row, log_sub), nsub - 1)
    row_chunk = jnp.right_shift(row, log_chunk)

    def rows_of(refs, n):
        return jnp.concatenate([jnp.broadcast_to(rf, (n, DK_B)) for rf in refs], axis=0)

    def zeros(n):
        return jnp.zeros((n, DK_B), F32)

    for h in range(H_B):
        kc = slice(h * DK_B, (h + 1) * DK_B)
        vc = slice(h * DV_B, (h + 1) * DV_B)
        b = b_all[:, kc]
        q = q_ref[:, kc]
        k = k_ref[:, kc]
        v = v_ref[:, vc]
        b_last = b[tc - 1:tc, :]
        st = st_ref[h]
        sub_refs = [b[i * sub:i * sub + 1, :] for i in range(tc // sub)]
        chunk_refs = [b[ci * chunk:ci * chunk + 1, :] for ci in range(nchunk)]

        qt = q * jnp.exp2(b - rows_of(sub_refs, sub))
        qx = [jnp.where(row_sub == i, qt, 0.0) for i in range(nsub)]
        kx = []
        for i in range(nsub):
            pieces = []
            for ci in range(nchunk):
                lo_row, n = ci * chunk, (i + 1) * sub
                pieces.append(k[lo_row:lo_row + n] * jnp.exp2(sub_refs[ci * nsub + i] - b[lo_row:lo_row + n]))
                if n < chunk:
                    pieces.append(zeros(chunk - n))
            kx.append(jnp.concatenate(pieces, axis=0) if len(pieces) > 1 else pieces[0])
        a = jnp.where(near, _dot_nt(jnp.concatenate(qx, axis=1).astype(BF16),
                                    jnp.concatenate(kx, axis=1).astype(BF16)), 0.0)
        if nchunk > 1:
            qc = q * jnp.exp2(b - rows_of(chunk_refs, chunk))
            qx = [jnp.where(row_chunk == ci, qc, 0.0) for ci in range(1, nchunk)]
            kx = []
            for ci in range(1, nchunk):
                n = ci * chunk
                kx.append(jnp.concatenate([k[:n] * jnp.exp2(chunk_refs[ci] - b[:n]), zeros(tc - n)], axis=0))
            a = jnp.where(far, _dot_nt(jnp.concatenate(qx, axis=1).astype(BF16),
                                       jnp.concatenate(kx, axis=1).astype(BF16)), a)
        o = _dot(a.astype(BF16), v) + _dot_nt((q * jnp.exp2(b)).astype(BF16), st.astype(BF16))
        kd = (k * jnp.exp2(b_last - b)).astype(BF16)
        st_ref[h] = st * jnp.exp2(b_last) + _dot_tn(v, kd)
        ms = jnp.mean(o * o, axis=-1, keepdims=True)
        on = o * lax.rsqrt(ms + LN_EPS) * g_ref[:, vc]
        rbv = rb_ref[:, vc]
        hb_ref[:, vc] = (on * (rbv * _sigmoid(rbv))).astype(hb_ref.dtype)

    @pl.when(t == pl.num_programs(1) - 1)
    def _():
        for h in range(H_B):
            sfin_ref[h] = st_ref[h].T


def _gla(qb, kb, vb, la, rb, g, s0, tc, chunk):
    B, T, _ = qb.shape
    sub = min(GLA_SUB, chunk)

    def row(n):
        return pl.BlockSpec((None, tc, n), lambda b, t: (b, t, 0))

    st_spec = pl.BlockSpec((None, H_B, DK_B, DV_B), lambda b, t: (b, 0, 0, 0))
    return pl.pallas_call(
        functools.partial(_gla_kernel, chunk=chunk, sub=sub),
        grid=(B, T // tc),
        in_specs=[row(GLA_K), row(GLA_K), row(GLA_V), row(GLA_K), row(GLA_V),
                  _const_spec(g.shape), st_spec],
        out_specs=[row(GLA_V), st_spec],
        out_shape=[jax.ShapeDtypeStruct((B, T, GLA_V), BF16),
                   jax.ShapeDtypeStruct((B, H_B, DK_B, DV_B), F32)],
        scratch_shapes=[pltpu.VMEM((H_B, DV_B, DK_B), F32)],
        compiler_params=_params("parallel", "arbitrary"),
        name="gla",
    )(qb, kb, vb, la, rb, g, s0)


def _merge_kernel(oa_ref, hb_ref, ga_ref, gb_ref, x_ref, wa_ref, wb_ref, wo_ref, g_ref, b_ref, x1_ref):
    ya = _dot(oa_ref[...], wa_ref[...])
    yb = _dot(hb_ref[...], wb_ref[...])
    merged = _sigmoid(ga_ref[...].astype(F32)) * ya + _sigmoid(gb_ref[...].astype(F32)) * yb
    z = ALPHA * x_ref[...] + _dot(merged.astype(BF16), wo_ref[...])
    x1_ref[...] = _layer_norm(z, g_ref[...], b_ref[...])


def _merge(oa, hb, ga, gb, x, wa, wb, wo, g, b, tm):
    B, T, D = x.shape

    def row(n):
        return pl.BlockSpec((None, tm, n), lambda bb, t: (bb, t, 0))

    return pl.pallas_call(
        _merge_kernel,
        grid=(B, T // tm),
        in_specs=[row(FOX_W), row(GLA_V), row(D), row(D), row(D),
                  _const_spec(wa.shape), _const_spec(wb.shape), _const_spec(wo.shape),
                  _const_spec(g.shape), _const_spec(b.shape)],
        out_specs=row(D),
        out_shape=jax.ShapeDtypeStruct((B, T, D), F32),
        compiler_params=_params("parallel", "parallel"),
        name="merge",
    )(oa, hb, ga, gb, x, wa, wb, wo, g, b)


FFN_GROUP = 256


def _gelu_tanh(x):
    return 0.5 * x * (1.0 + jnp.tanh(math.sqrt(2.0 / math.pi) * (x + 0.044715 * (x * x * x))))


def _ffn_pack(a):
    return jnp.concatenate([a[..., k * D_FF + j * FFN_GROUP:k * D_FF + (j + 1) * FFN_GROUP]
                            for j in range(D_FF // FFN_GROUP) for k in range(2)], axis=-1)


def _ffn_unpack(a):
    lead = a.shape[:-1]
    return a.reshape(*lead, D_FF // FFN_GROUP, 2, FFN_GROUP).swapaxes(-3, -2).reshape(*lead, 2 * D_FF)


def _ffn_kernel(x1_ref, p_ref, cprev_ref, wup_ref, cw_ref, cb_ref, wdn_ref, g2_ref, b2_ref,
                wpl_ref, wplg_ref, g3_ref, b3_ref, y_ref, cnew_ref, carry_ref):
    @pl.when(pl.program_id(1) == 0)
    def _():
        carry_ref[...] = cprev_ref[...]

    x1 = x1_ref[...]
    tm = x1.shape[0]
    up = _dot(x1.astype(BF16), wup_ref[...])
    prev = carry_ref[...]
    row = lax.broadcasted_iota(jnp.int32, (tm, 1), 0)
    up_m1 = jnp.where(row == 0, prev[1:2, :], pltpu.roll(up, 1, 0))
    up_m2 = jnp.where(row == 0, prev[0:1, :], jnp.where(row == 1, prev[1:2, :], pltpu.roll(up, 2, 0)))
    conv = cb_ref[...] + cw_ref[0:1, :] * up_m2 + cw_ref[1:2, :] * up_m1 + cw_ref[2:3, :] * up
    last2 = up[tm - 2:tm, :]
    carry_ref[...] = last2
    cnew_ref[...] = last2
    hgl = jnp.concatenate(
        [conv[:, 2 * j * FFN_GROUP:(2 * j + 1) * FFN_GROUP]
         * _gelu_tanh(conv[:, (2 * j + 1) * FFN_GROUP:(2 * j + 2) * FFN_GROUP])
         for j in range(D_FF // FFN_GROUP)], axis=1).astype(BF16)
    z2 = ALPHA * x1 + _dot(hgl, wdn_ref[...])
    x2 = _layer_norm(z2, g2_ref[...], b2_ref[...])
    e = _dot(p_ref[...].astype(BF16), wpl_ref[...]) * _sigmoid(_dot(x2.astype(BF16), wplg_ref[...]))
    y_ref[...] = _layer_norm(ALPHA * x2 + e, g3_ref[...], b3_ref[...])


def _ffn(x1, p, cprev, wup, cw, cb, wdn, g2, b2, wpl, wplg, g3, b3, tm):
    B, T, D = x1.shape

    def row(n):
        return pl.BlockSpec((None, tm, n), lambda bb, t: (bb, t, 0))

    cspec = pl.BlockSpec((None, CONV_W - 1, 2 * D_FF), lambda bb, t: (bb, 0, 0))
    consts = (wup, cw, cb, wdn, g2, b2, wpl, wplg, g3, b3)
    return pl.pallas_call(
        _ffn_kernel,
        grid=(B, T // tm),
        in_specs=[row(D), row(PLE_DIM), cspec] + [_const_spec(c.shape) for c in consts],
        out_specs=[row(D), cspec],
        out_shape=[jax.ShapeDtypeStruct((B, T, D), F32),
                   jax.ShapeDtypeStruct((B, CONV_W - 1, 2 * D_FF), F32)],
        scratch_shapes=[pltpu.VMEM((CONV_W - 1, 2 * D_FF), F32)],
        compiler_params=_params("parallel", "arbitrary"),
        name="ffn",
    )(x1, p, cprev, *consts)


def _pick(T, want):
    t = min(T, want)
    while T % t:
        t //= 2
    return t


def _tiles(T, rows):
    chunk = min(GLA_CHUNK, T)
    return dict(proj=_pick(rows, 512), merge=_pick(rows, 1024), ffn=_pick(T, 512),
                fox_q=_pick(T, 512), fox_k=_pick(T, 512), gla=_pick(T, 4 * chunk), gla_chunk=chunk)


def _pack_w_in(w_in):
    pts, acc = {}, 0
    for name, n in (("qa", FOX_W), ("ka", FOX_W), ("va", FOX_W), ("fa", H_A), ("qb", GLA_K), ("kb", GLA_K),
                    ("vb", GLA_V), ("rb", GLA_V), ("a1", GLA_RANK), ("ga", D_MODEL), ("gb", D_MODEL)):
        pts[name] = (acc, n)
        acc += n
    runs = []
    for name in _SEG:
        lo, n = pts[name]
        pad = _SEG[name][1] - n
        if runs and runs[-1][1] == lo and runs[-1][2] == 0:
            runs[-1][1:] = [lo + n, pad]
        else:
            runs.append([lo, lo + n, pad])
    cols = [jnp.pad(w_in[:, lo:hi], ((0, 0), (0, pad))) if pad else w_in[:, lo:hi] for lo, hi, pad in runs]
    return jnp.concatenate(cols, axis=1).astype(BF16)


def _layer(x, p, fox_past, s0, conv_prev, wts):
    (w_in, b_fgate, w_a2, b_a2, g_gla, w_a_out, w_b_out, w_o, ln1_g, ln1_b, w_up, conv_w, conv_b,
     w_down, ln2_g, ln2_b, w_pl, w_plg, ln3_g, ln3_b) = wts
    B, T, D = x.shape
    row2 = lambda a: a.reshape(1, -1).astype(F32)

    wcat = _pack_w_in(w_in)
    bf = jnp.pad(row2(b_fgate), ((0, 0), (0, LANES - H_A)))
    wa2 = jnp.pad(w_a2.astype(F32), ((0, LANES - GLA_RANK), (0, 0)))
    wa2h = wa2.astype(BF16)
    wa2l = (wa2 - wa2h.astype(F32)).astype(BF16)

    fold = fox_past is not None and T < LANES
    tile = _tiles(T, B * T if fold else T)
    rows = (lambda a: a.reshape(1, B * T, a.shape[-1])) if fold else (lambda a: a)
    unrows = (lambda a: a.reshape(B, T, a.shape[-1])) if fold else (lambda a: a)
    proj_out = _proj(rows(x), wcat, bf, wa2h, wa2l, row2(b_a2), tile["proj"],
                     vt_block=tile["fox_k"] if fox_past is None else None)
    proj_out = list(proj_out)
    proj_out[5] = jnp.transpose(proj_out[5], (0, 2, 1))
    if fold:
        proj_out = [unrows(o) for o in proj_out]
    (qa, ka_f, ka_b, va_f, va_b, logf, logf_pad, qb, kb, vb, rb, la, ga, gb) = proj_out

    if fox_past is None:
        oa = _fox_prompt(qa, ka_b, logf_pad, va_b, tile["fox_q"], tile["fox_k"])
    else:
        past_k, past_v, past_logf = fox_past
        P = past_k.shape[1]
        lk = -(-(P + T) // FOX_KEY_PAD) * FOX_KEY_PAD
        padk = lambda a: jnp.pad(a, ((0, 0), (0, lk - P - T), (0, 0)))
        k_all = padk(jnp.concatenate([past_k.reshape(B, P, FOX_W).astype(BF16), ka_b], axis=1))
        v_all = padk(jnp.concatenate([past_v.reshape(B, P, FOX_W).astype(BF16), va_b], axis=1))
        lf_all = padk(jnp.concatenate([past_logf.astype(F32), logf], axis=1))
        nc = _neg_cumsum(jnp.transpose(lf_all, (0, 2, 1)), _pick(lk, FOX_KEY_PAD))
        oa = _fox_sample(qa, k_all, v_all, nc.reshape(B, H_A // 2, 2, lk), P)

    hb, s_fin = _gla(qb, kb, vb, la, rb, row2(g_gla), s0.astype(F32), tile["gla"], tile["gla_chunk"])

    x1 = unrows(_merge(rows(oa), rows(hb), rows(ga), rows(gb), rows(x), w_a_out.astype(BF16),
                       w_b_out.astype(BF16), w_o.astype(BF16), row2(ln1_g), row2(ln1_b),
                       tile["merge"]))
    y, conv_new = _ffn(x1, p, _ffn_pack(conv_prev.astype(F32)), _ffn_pack(w_up.astype(BF16)),
                       _ffn_pack(conv_w.astype(F32)), _ffn_pack(row2(conv_b)),
                       w_down.astype(BF16), row2(ln2_g), row2(ln2_b), w_pl.astype(BF16),
                       w_plg.astype(BF16), row2(ln3_g), row2(ln3_b), tile["ffn"])
    conv_new = _ffn_unpack(conv_new)
    k_out = ka_f.reshape(B, T, H_A, DH_A)
    v_out = va_f.reshape(B, T, H_A, DH_A)
    return y, k_out, v_out, logf, s_fin, conv_new


def kernel(x_prompt, x_sample, cache_fox_k, cache_fox_v, cache_fox_logf, state_gla, cache_ffn_conv, p_prompt, p_sample, w_in, b_fgate, w_a2, b_a2, g_gla, w_a_out, w_b_out, w_o, ln1_g, ln1_b, w_up, conv_w, conv_b, w_down, ln2_g, ln2_b, w_pl, w_plg, ln3_g, ln3_b):
    hp, hs = x_prompt, x_sample
    outs_p = [[] for _ in range(5)]
    outs_s = [[] for _ in range(5)]
    for i in range(DEPTH):
        wts = (w_in[i], b_fgate[i], w_a2[i], b_a2[i], g_gla[i], w_a_out[i], w_b_out[i], w_o[i],
               ln1_g[i], ln1_b[i], w_up[i], conv_w[i], conv_b[i], w_down[i], ln2_g[i], ln2_b[i],
               w_pl[i], w_plg[i], ln3_g[i], ln3_b[i])
        Bp = hp.shape[0]
        s0_p = jnp.zeros((Bp, H_B, DK_B, DV_B), F32)
        c0_p = jnp.zeros((Bp, CONV_W - 1, 2 * D_FF), F32)
        hp, *rest_p = _layer(hp, p_prompt[i], None, s0_p, c0_p, wts)
        hs, *rest_s = _layer(hs, p_sample[i], (cache_fox_k[i], cache_fox_v[i], cache_fox_logf[i]),
                             state_gla[i], cache_ffn_conv[i], wts)
        for dst, src in ((outs_p, rest_p), (outs_s, rest_s)):
            for lst, val in zip(dst, src):
                lst.append(val)
    return (hp, hs, *[jnp.stack(l) for l in outs_p], *[jnp.stack(l) for l in outs_s])
```

```python
import functools
import math

import jax
import jax.numpy as jnp
from jax import lax
from jax.experimental import pallas as pl
from jax.experimental.pallas import tpu as pltpu

F32 = jnp.float32
BF16 = jnp.bfloat16

D_MODEL = 1024
H_A, DH_A = 8, 64
FOX_W = H_A * DH_A
H_B, DK_B, DV_B = 4, 128, 256
GLA_K = H_B * DK_B
GLA_V = H_B * DV_B
GLA_RANK = 16
GLA_TAU = 16
D_FF = 2816
CONV_W = 3
PLE_DIM = 256
LN_EPS = 1e-5
DEPTH = 1
ALPHA = (2 * DEPTH) ** 0.25
GLA_CHUNK = 64
GLA_SUB = 16

LANES = 128
LOG2E = math.log2(math.e)
V7X_VMEM_LIMIT_BYTES = 56 * 1024 * 1024

_SEG = {}
_off = 0
for _name, _n in (("qa", FOX_W), ("ka", FOX_W), ("va", FOX_W), ("qb", GLA_K), ("kb", GLA_K),
                  ("vb", GLA_V), ("rb", GLA_V), ("ga", D_MODEL), ("gb", D_MODEL),
                  ("fa", LANES), ("a1", LANES)):
    _SEG[_name] = (_off, _n)
    _off += _n


def _params(*sem):
    return pltpu.CompilerParams(dimension_semantics=sem, vmem_limit_bytes=V7X_VMEM_LIMIT_BYTES)


def _const_spec(shape):
    nd = len(shape)
    return pl.BlockSpec(shape, lambda *_: (0,) * nd, pipeline_mode=pl.Buffered(1))


def _log_sigmoid(z):
    return jnp.minimum(z, 0.0) - jnp.log(1.0 + jnp.exp(-jnp.abs(z)))


def _sigmoid(z):
    return 1.0 / (1.0 + jnp.exp(-z))


def _split3(a):
    hi = a.astype(BF16)
    r = a - hi.astype(F32)
    mid = r.astype(BF16)
    lo = (r - mid.astype(F32)).astype(BF16)
    return hi, mid, lo


def _layer_norm(z, g, b):
    mu = jnp.mean(z, axis=-1, keepdims=True)
    zc = z - mu
    var = jnp.mean(zc * zc, axis=-1, keepdims=True)
    return zc * lax.rsqrt(var + LN_EPS) * g + b


def _dot(a, b):
    return jnp.dot(a, b, preferred_element_type=F32)


def _dot_nt(a, b):
    return lax.dot_general(a, b, (((1,), (1,)), ((), ())), preferred_element_type=F32)


def _dot_tn(a, b):
    return lax.dot_general(a, b, (((0,), (0,)), ((), ())), preferred_element_type=F32)


def _proj_kernel(x_ref, w_ref, bf_ref, wa2h_ref, wa2l_ref, ba2_ref,
                 qa_ref, kaf_ref, kab_ref, vaf_ref, vab_ref, logf_ref, logfp_ref,
                 qb_ref, kb_ref, vb_ref, rb_ref, la_ref, ga_ref, gb_ref, carry_ref, *, values_transposed):
    xb = x_ref[...].astype(BF16)

    def seg(name):
        lo, n = _SEG[name]
        return _dot(xb, w_ref[:, lo:lo + n])

    logf = _log_sigmoid(seg("fa") + bf_ref[...])
    qa_ref[...] = (seg("qa") * (DH_A ** -0.5 * LOG2E)).astype(BF16)
    ka = seg("ka")
    kaf_ref[...] = ka
    kab_ref[...] = ka.astype(BF16)
    va = seg("va")
    vaf_ref[...] = va
    if values_transposed:
        for p in range(H_A // 2):
            vab_ref[p] = va[:, p * LANES:(p + 1) * LANES].T.astype(BF16)
    else:
        vab_ref[...] = va.astype(BF16)
    qb_ref[...] = seg("qb") * (DK_B ** -0.5)
    kb_ref[...] = seg("kb")
    vb_ref[...] = seg("vb").astype(BF16)
    rb_ref[...] = seg("rb")
    ga_ref[...] = seg("ga").astype(BF16)
    gb_ref[...] = seg("gb").astype(BF16)
    a1 = seg("a1")
    a1h = a1.astype(BF16)
    a1l = (a1 - a1h.astype(F32)).astype(BF16)
    z = _dot(a1h, wa2h_ref[...]) + _dot(a1l, wa2h_ref[...]) + _dot(a1h, wa2l_ref[...]) + ba2_ref[...]
    la_ref[...] = _log_sigmoid(z) * (1.0 / GLA_TAU)
    logf_ref[...] = logf.T[:H_A, :]
    if values_transposed:
        logfp_ref[...] = _fox_bias_lanes(logf, carry_ref, pl.program_id(1) == 0).astype(logfp_ref.dtype)
    else:
        logfp_ref[...] = logf


def _proj(x, wcat, bf, wa2h, wa2l, ba2, tm, vt_block=None):
    B, T, D = x.shape
    grid = (B, T // tm)

    def row(n, dt):
        return pl.BlockSpec((None, tm, n), lambda b, t: (b, t, 0)), jax.ShapeDtypeStruct((B, T, n), dt)

    outs = [row(FOX_W, BF16), row(FOX_W, F32), row(FOX_W, BF16), row(FOX_W, F32), row(FOX_W, BF16),
            row(H_A, F32), row(LANES, F32), row(GLA_K, F32), row(GLA_K, F32), row(GLA_V, BF16), row(GLA_V, F32),
            row(GLA_K, F32), row(D_MODEL, BF16), row(D_MODEL, BF16)]
    outs[5] = (pl.BlockSpec((None, H_A, tm), lambda b, t: (b, 0, t)), jax.ShapeDtypeStruct((B, H_A, T), F32))
    if vt_block is not None:
        per = vt_block // tm
        outs[4] = (pl.BlockSpec((None, H_A // 2, None, LANES, tm), lambda b, t: (b, 0, t // per, 0, t % per)),
                   jax.ShapeDtypeStruct((B, H_A // 2, T // vt_block, LANES, vt_block), BF16))
        outs[6] = row(LANES, BF16)
    return pl.pallas_call(
        functools.partial(_proj_kernel, values_transposed=vt_block is not None),
        grid=grid,
        in_specs=[pl.BlockSpec((None, tm, D), lambda b, t: (b, t, 0)),
                  _const_spec(wcat.shape), _const_spec(bf.shape), _const_spec(wa2h.shape),
                  _const_spec(wa2l.shape), _const_spec(ba2.shape)],
        out_specs=[o[0] for o in outs],
        out_shape=[o[1] for o in outs],
        scratch_shapes=[pltpu.VMEM((1, LANES), F32)],
        compiler_params=_params("parallel", "arbitrary"),
        name="proj",
    )(x, wcat, bf, wa2h, wa2l, ba2)


def _cumsum_kernel(lf_ref, nc_ref, carry_ref):
    @pl.when(pl.program_id(1) == 0)
    def _():
        carry_ref[...] = jnp.zeros_like(carry_ref)

    x = lf_ref[...]
    tl = x.shape[1]
    r = lax.broadcasted_iota(jnp.int32, (tl, tl), 0)
    c = lax.broadcasted_iota(jnp.int32, (tl, tl), 1)
    tri = jnp.where(r <= c, 1.0, 0.0).astype(BF16)
    hi, mid, lo = _split3(x)
    cs = _dot(hi, tri) + _dot(mid, tri) + _dot(lo, tri) + carry_ref[...]
    nc_ref[...] = -cs
    carry_ref[...] = carry_ref[...] + jnp.sum(x, axis=1, keepdims=True)


def _neg_cumsum(lf_t, tl):
    B, H, L = lf_t.shape
    return pl.pallas_call(
        _cumsum_kernel,
        grid=(B, L // tl),
        in_specs=[pl.BlockSpec((None, H, tl), lambda b, t: (b, 0, t))],
        out_specs=pl.BlockSpec((None, H, tl), lambda b, t: (b, 0, t)),
        out_shape=jax.ShapeDtypeStruct((B, H, L), F32),
        scratch_shapes=[pltpu.VMEM((H, 1), F32)],
        compiler_params=_params("parallel", "arbitrary"),
        name="cumsum",
    )(lf_t)


FOX_KEY_PAD = 512
FOX_BIAS_PARTS = 3
FOX_UNROLL = 4
FOX_SUM_ROWS = 16


def _fox_bias_lanes(x, carry_ref, first):
    @pl.when(first)
    def _():
        carry_ref[...] = jnp.zeros_like(carry_ref)

    tl = x.shape[0]
    row = lax.broadcasted_iota(jnp.int32, (tl, 1), 0)
    cs = x
    shift = 1
    while shift < tl:
        cs = cs + jnp.where(row >= shift, pltpu.roll(cs, shift, 0), 0.0)
        shift *= 2
    cs = cs + carry_ref[...]
    carry_ref[...] = cs[tl - 1:tl, :]
    parts =[p.astype(F32) for p in _split3(cs * (-LOG2E))]
    lane = lax.broadcasted_iota(jnp.int32, (1, LANES), 1)
    out = jnp.zeros_like(x)
    for j in reversed(range(FOX_BIAS_PARTS)):
        moved = parts[j] if j == 0 else pltpu.roll(parts[j], j * H_A, 1)
        out = jnp.where(jnp.logical_and(lane >= j * H_A, lane < (j + 1) * H_A), moved, out)
    return out


def _fox_prompt_kernel(q_ref, k_ref, kb_ref, vt_ref, o_ref, m_ref, acc_ref,
                       sa_ref, sb_ref, bma_ref, bmb_ref, *, tq, tk):
    i = pl.program_id(2)
    lane = lax.broadcasted_iota(jnp.int32, (1, LANES), 1)
    q = q_ref[...]
    zero = jnp.zeros_like(q)
    qa = []
    for h in range(2):
        own = (lane < DH_A) if h == 0 else (lane >= DH_A)
        head = 2 * pl.program_id(1) + h
        ones = jnp.logical_and(jnp.bitwise_and(lane, H_A - 1) == head, lane < FOX_BIAS_PARTS * H_A)
        ones = jnp.broadcast_to(jnp.where(ones, 1.0, 0.0).astype(q.dtype), q.shape)
        qa.append(jnp.concatenate([jnp.where(own, q, zero), ones], axis=1))
    m_ref[...] = jnp.full_like(m_ref, -jnp.inf)
    acc_ref[...] = jnp.zeros_like(acc_ref)
    sum_rows = jnp.ones((FOX_SUM_ROWS, tk), BF16)
    per = tq // tk

    def scores(j, s_ref, bm_ref):
        start = pl.multiple_of(j * tk, tk)
        ka = jnp.concatenate([k_ref[pl.ds(start, tk), :], kb_ref[pl.ds(start, tk), :]], axis=1)
        for h in range(2):
            st = _dot_nt(ka, qa[h])
            s_ref[h] = st
            bm_ref[h] = jnp.max(st, axis=0, keepdims=True)

    def soft_pv(j, s_ref, bm_ref, diag):
        for h in range(2):
            st = s_ref[h]
            if diag is not None:
                r = lax.broadcasted_iota(jnp.int32, (tk, tq), 0)
                c = lax.broadcasted_iota(jnp.int32, (tk, tq), 1)
                st = jnp.where(r + diag * tk <= c, st, -jnp.inf)
                bm = jnp.max(st, axis=0, keepdims=True)
            else:
                bm = bm_ref[h]
            m_old = m_ref[h]
            m_new = jnp.maximum(m_old, bm)
            p = jnp.exp2(st - m_new)
            a = jnp.exp2(m_old - m_new)
            vt = jnp.concatenate([vt_ref[j, h * DH_A:(h + 1) * DH_A, :], sum_rows], axis=0)
            acc_ref[h] = a * acc_ref[h] + _dot(vt, p.astype(BF16))
            m_ref[h] = m_new

    slots = ((sa_ref, bma_ref), (sb_ref, bmb_ref))

    def run(j0, diags, prefetch_after):
        for u, diag in enumerate(diags):
            if u + 1 < len(diags) or prefetch_after:
                scores(j0 + u + 1, *slots[(u + 1) % 2])
            soft_pv(j0 + u, *slots[u % 2], diag)

    scores(0, *slots[0])

    def body(t, carry):
        run(FOX_UNROLL * t, [None] * FOX_UNROLL, True)
        return carry

    full = i * per
    log_unroll = FOX_UNROLL.bit_length() - 1
    lax.fori_loop(0, lax.shift_right_logical(full, log_unroll), body, 0)
    rest = jnp.bitwise_and(full, FOX_UNROLL - 1)
    for n in range(0, FOX_UNROLL, per):
        @pl.when(rest == n)
        def _(n=n):
            run(full - n, [None] * n + list(range(per)), False)

    ot = jnp.concatenate([acc_ref[h, :DH_A, :] / acc_ref[h, DH_A:DH_A + 1, :] for h in range(2)],
                         axis=0)
    o_ref[...] = ot.T.astype(o_ref.dtype)


def _fox_prompt(qa, ka, kbias, vt, tq, tk):
    B, T, _ = qa.shape
    nk = T // tk
    return pl.pallas_call(
        functools.partial(_fox_prompt_kernel, tq=tq, tk=tk),
        grid=(B, H_A // 2, T // tq),
        in_specs=[pl.BlockSpec((None, tq, LANES), lambda b, p, i: (b, i, p)),
                  pl.BlockSpec((None, T, LANES), lambda b, p, i: (b, 0, p)),
                  pl.BlockSpec((None, T, LANES), lambda b, p, i: (b, 0, 0)),
                  pl.BlockSpec((None, None, nk, LANES, tk), lambda b, p, i: (b, p, 0, 0, 0))],
        out_specs=pl.BlockSpec((None, tq, LANES), lambda b, p, i: (b, i, p)),
        out_shape=jax.ShapeDtypeStruct((B, T, FOX_W), BF16),
        scratch_shapes=[pltpu.VMEM((2, 1, tq), F32),
                        pltpu.VMEM((2, DH_A + FOX_SUM_ROWS, tq), F32),
                        pltpu.VMEM((2, tk, tq), F32), pltpu.VMEM((2, tk, tq), F32),
                        pltpu.VMEM((2, 1, tq), F32), pltpu.VMEM((2, 1, tq), F32)],
        compiler_params=_params("parallel", "parallel", "arbitrary"),
        name="fox_prompt",
    )(qa, ka, kbias, vt)


def _fox_sample_kernel(q_ref, kt_ref, vt_ref, kn_ref, vn_ref, ncp_ref, ncn_ref, o_ref):
    first = lax.broadcasted_iota(jnp.int32, (1, LANES), 1) < DH_A
    q = q_ref[...]
    zero = jnp.zeros_like(q)
    tq = q.shape[0]
    kt = jnp.concatenate([kt_ref[0], kt_ref[1]], axis=0).astype(BF16)
    vt = jnp.concatenate([vt_ref[0], vt_ref[1]], axis=0).astype(BF16)
    kn = kn_ref[...]
    vn = vn_ref[...]
    r = lax.broadcasted_iota(jnp.int32, (tq, tq), 0)
    c = lax.broadcasted_iota(jnp.int32, (tq, tq), 1)
    outs = []
    for h in range(2):
        qh = jnp.where(first, q, zero) if h == 0 else jnp.where(first, zero, q)
        sp = _dot(qh, kt) + ncp_ref[h:h + 1, :] * LOG2E
        sn = jnp.where(c <= r, _dot_nt(qh, kn) + ncn_ref[h:h + 1, :] * LOG2E, -jnp.inf)
        m = jnp.maximum(jnp.max(sp, axis=1, keepdims=True), jnp.max(sn, axis=1, keepdims=True))
        pp = jnp.exp2(sp - m)
        pn = jnp.exp2(sn - m)
        l = jnp.sum(pp, axis=1, keepdims=True) + jnp.sum(pn, axis=1, keepdims=True)
        outs.append((_dot_nt(pp.astype(BF16), vt) + _dot(pn.astype(BF16), vn)) / l)
    o_ref[...] = jnp.where(first, outs[0], outs[1]).astype(o_ref.dtype)


def _fox_sample(qa, kt, vt, kn, vn, ncp, ncn):
    B, Tq, _ = qa.shape
    P = kt.shape[-1]
    new = pl.BlockSpec((None, Tq, LANES), lambda b, p: (b, 0, p))
    old = pl.BlockSpec((None, 2, DH_A, P), lambda b, p: (b, p, 0, 0))
    return pl.pallas_call(
        _fox_sample_kernel,
        grid=(B, H_A // 2),
        in_specs=[new, old, old, new, new,
                  pl.BlockSpec((None, None, 2, P), lambda b, p: (b, p, 0, 0)),
                  pl.BlockSpec((None, None, 2, Tq), lambda b, p: (b, p, 0, 0))],
        out_specs=new,
        out_shape=jax.ShapeDtypeStruct((B, Tq, FOX_W), BF16),
        compiler_params=_params("parallel", "parallel"),
        name="fox_sample",
    )(qa, kt, vt, kn, vn, ncp, ncn)


def _gla_kernel(q_ref, k_ref, v_ref, la_ref, rb_ref, g_ref, s0_ref, hb_ref, sfin_ref, st_ref,
                *, chunk, sub):
    t = pl.program_id(1)

    @pl.when(t == 0)
    def _():
        for h in range(H_B):
            st_ref[h] = s0_ref[h].T

    tc = q_ref.shape[0]
    nsub = chunk // sub
    nchunk = tc // chunk
    r = lax.broadcasted_iota(jnp.int32, (tc, tc), 0)
    c = lax.broadcasted_iota(jnp.int32, (tc, tc), 1)
    log_chunk = chunk.bit_length() - 1
    log_sub = sub.bit_length() - 1
    same_chunk = jnp.right_shift(r, log_chunk) == jnp.right_shift(c, log_chunk)
    near = jnp.logical_and(c <= r, same_chunk)
    far = jnp.right_shift(c, log_chunk) < jnp.right_shift(r, log_chunk)
    tri = jnp.where(c <= r, 1.0, 0.0).astype(BF16)
    hi, mid, lo = _split3(la_ref[...])
    b_all = (_dot(tri, hi) + _dot(tri, mid) + _dot(tri, lo)) * LOG2E

    row = lax.broadcasted_iota(jnp.int32, (tc, 1), 0)
    row_sub = jnp.bitwise_and(jnp.right_shift(row, log_sub), nsub - 1)
    row_chunk = jnp.right_shift(row, log_chunk)

    def rows_of(refs, n):
        return jnp.concatenate([jnp.broadcast_to(rf, (n, DK_B)) for rf in refs], axis=0)

    def zeros(n):
        return jnp.zeros((n, DK_B), F32)

    for h in range(H_B):
        kc = slice(h * DK_B, (h + 1) * DK_B)
        vc = slice(h * DV_B, (h + 1) * DV_B)
        b = b_all[:, kc]
        q = q_ref[:, kc]
        k = k_ref[:, kc]
        v = v_ref[:, vc]
        b_last = b[tc - 1:tc, :]
        st = st_ref[h]
        sub_refs = [b[i * sub:i * sub + 1, :] for i in range(tc // sub)]
        chunk_refs = [b[ci * chunk:ci * chunk + 1, :] for ci in range(nchunk)]

        qt = q * jnp.exp2(b - rows_of(sub_refs, sub))
        qx = [jnp.where(row_sub == i, qt, 0.0) for i in range(nsub)]
        kx = []
        for i in range(nsub):
            pieces = []
            for ci in range(nchunk):
                lo_row, n = ci * chunk, (i + 1) * sub
                pieces.append(k[lo_row:lo_row + n] * jnp.exp2(sub_refs[ci * nsub + i] - b[lo_row:lo_row + n]))
                if n < chunk:
                    pieces.append(zeros(chunk - n))
            kx.append(jnp.concatenate(pieces, axis=0) if len(pieces) > 1 else pieces[0])
        a = jnp.where(near, _dot_nt(jnp.concatenate(qx, axis=1).astype(BF16),
                                    jnp.concatenate(kx, axis=1).astype(BF16)), 0.0)
        if nchunk > 1:
            qc = q * jnp.exp2(b - rows_of(chunk_refs, chunk))
            qx = [jnp.where(row_chunk == ci, qc, 0.0) for ci in range(1, nchunk)]
            kx = []
            for ci in range(1, nchunk):
                n = ci * chunk
                kx.append(jnp.concatenate([k[:n] * jnp.exp2(chunk_refs[ci] - b[:n]), zeros(tc - n)], axis=0))
            a = jnp.where(far, _dot_nt(jnp.concatenate(qx, axis=1).astype(BF16),
                                       jnp.concatenate(kx, axis=1).astype(BF16)), a)
        o = _dot(a.astype(BF16), v) + _dot_nt((q * jnp.exp2(b)).astype(BF16), st.astype(BF16))
        kd = (k * jnp.exp2(b_last - b)).astype(BF16)
        st_ref[h] = st * jnp.exp2(b_last) + _dot_tn(v, kd)
        ms = jnp.mean(o * o, axis=-1, keepdims=True)
        on = o * lax.rsqrt(ms + LN_EPS) * g_ref[:, vc]
        rbv = rb_ref[:, vc]
        hb_ref[:, vc] = (on * (rbv * _sigmoid(rbv))).astype(hb_ref.dtype)

    @pl.when(t == pl.num_programs(1) - 1)
    def _():
        for h in range(H_B):
            sfin_ref[h] = st_ref[h].T


def _gla(qb, kb, vb, la, rb, g, s0, tc, chunk):
    B, T, _ = qb.shape
    sub = min(GLA_SUB, chunk)

    def row(n):
        return pl.BlockSpec((None, tc, n), lambda b, t: (b, t, 0))

    st_spec = pl.BlockSpec((None, H_B, DK_B, DV_B), lambda b, t: (b, 0, 0, 0))
    return pl.pallas_call(
        functools.partial(_gla_kernel, chunk=chunk, sub=sub),
        grid=(B, T // tc),
        in_specs=[row(GLA_K), row(GLA_K), row(GLA_V), row(GLA_K), row(GLA_V),
                  _const_spec(g.shape), st_spec],
        out_specs=[row(GLA_V), st_spec],
        out_shape=[jax.ShapeDtypeStruct((B, T, GLA_V), BF16),
                   jax.ShapeDtypeStruct((B, H_B, DK_B, DV_B), F32)],
        scratch_shapes=[pltpu.VMEM((H_B, DV_B, DK_B), F32)],
        compiler_params=_params("parallel", "arbitrary"),
        name="gla",
    )(qb, kb, vb, la, rb, g, s0)


def _merge_kernel(oa_ref, hb_ref, ga_ref, gb_ref, x_ref, wa_ref, wb_ref, wo_ref, g_ref, b_ref, x1_ref):
    ya = _dot(oa_ref[...], wa_ref[...])
    yb = _dot(hb_ref[...], wb_ref[...])
    merged = _sigmoid(ga_ref[...].astype(F32)) * ya + _sigmoid(gb_ref[...].astype(F32)) * yb
    z = ALPHA * x_ref[...] + _dot(merged.astype(BF16), wo_ref[...])
    x1_ref[...] = _layer_norm(z, g_ref[...], b_ref[...])


def _merge(oa, hb, ga, gb, x, wa, wb, wo, g, b, tm):
    B, T, D = x.shape

    def row(n):
        return pl.BlockSpec((None, tm, n), lambda bb, t: (bb, t, 0))

    return pl.pallas_call(
        _merge_kernel,
        grid=(B, T // tm),
        in_specs=[row(FOX_W), row(GLA_V), row(D), row(D), row(D),
                  _const_spec(wa.shape), _const_spec(wb.shape), _const_spec(wo.shape),
                  _const_spec(g.shape), _const_spec(b.shape)],
        out_specs=row(D),
        out_shape=jax.ShapeDtypeStruct((B, T, D), F32),
        compiler_params=_params("parallel", "parallel"),
        name="merge",
    )(oa, hb, ga, gb, x, wa, wb, wo, g, b)


FFN_GROUP = 256


def _gelu_tanh(x):
    return 0.5 * x * (1.0 + jnp.tanh(math.sqrt(2.0 / math.pi) * (x + 0.044715 * (x * x * x))))


def _ffn_pack(a):
    return jnp.concatenate([a[..., k * D_FF + j * FFN_GROUP:k * D_FF + (j + 1) * FFN_GROUP]
                            for j in range(D_FF // FFN_GROUP) for k in range(2)], axis=-1)


def _ffn_unpack(a):
    lead = a.shape[:-1]
    return a.reshape(*lead, D_FF // FFN_GROUP, 2, FFN_GROUP).swapaxes(-3, -2).reshape(*lead, 2 * D_FF)


def _ffn_kernel(x1_ref, p_ref, cprev_ref, wup_ref, cw_ref, cb_ref, wdn_ref, g2_ref, b2_ref,
                wpl_ref, wplg_ref, g3_ref, b3_ref, y_ref, cnew_ref, carry_ref):
    @pl.when(pl.program_id(1) == 0)
    def _():
        carry_ref[...] = cprev_ref[...]

    x1 = x1_ref[...]
    tm = x1.shape[0]
    up = _dot(x1.astype(BF16), wup_ref[...])
    prev = carry_ref[...]
    row = lax.broadcasted_iota(jnp.int32, (tm, 1), 0)
    up_m1 = jnp.where(row == 0, prev[1:2, :], pltpu.roll(up, 1, 0))
    up_m2 = jnp.where(row == 0, prev[0:1, :], jnp.where(row == 1, prev[1:2, :], pltpu.roll(up, 2, 0)))
    conv = cb_ref[...] + cw_ref[0:1, :] * up_m2 + cw_ref[1:2, :] * up_m1 + cw_ref[2:3, :] * up
    last2 = up[tm - 2:tm, :]
    carry_ref[...] = last2
    cnew_ref[...] = last2
    hgl = jnp.concatenate(
        [conv[:, 2 * j * FFN_GROUP:(2 * j + 1) * FFN_GROUP]
         * _gelu_tanh(conv[:, (2 * j + 1) * FFN_GROUP:(2 * j + 2) * FFN_GROUP])
         for j in range(D_FF // FFN_GROUP)], axis=1).astype(BF16)
    z2 = ALPHA * x1 + _dot(hgl, wdn_ref[...])
    x2 = _layer_norm(z2, g2_ref[...], b2_ref[...])
    e = _dot(p_ref[...].astype(BF16), wpl_ref[...]) * _sigmoid(_dot(x2.astype(BF16), wplg_ref[...]))
    y_ref[...] = _layer_norm(ALPHA * x2 + e, g3_ref[...], b3_ref[...])


def _ffn(x1, p, cprev, wup, cw, cb, wdn, g2, b2, wpl, wplg, g3, b3, tm):
    B, T, D = x1.shape

    def row(n):
        return pl.BlockSpec((None, tm, n), lambda bb, t: (bb, t, 0))

    cspec = pl.BlockSpec((None, CONV_W - 1, 2 * D_FF), lambda bb, t: (bb, 0, 0))
    consts = (wup, cw, cb, wdn, g2, b2, wpl, wplg, g3, b3)
    return pl.pallas_call(
        _ffn_kernel,
        grid=(B, T // tm),
        in_specs=[row(D), row(PLE_DIM), cspec] + [_const_spec(c.shape) for c in consts],
        out_specs=[row(D), cspec],
        out_shape=[jax.ShapeDtypeStruct((B, T, D), F32),
                   jax.ShapeDtypeStruct((B, CONV_W - 1, 2 * D_FF), F32)],
        scratch_shapes=[pltpu.VMEM((CONV_W - 1, 2 * D_FF), F32)],
        compiler_params=_params("parallel", "arbitrary"),
        name="ffn",
    )(x1, p, cprev, *consts)


def _pick(T, want):
    t = min(T, want)
    while T % t:
        t //= 2
    return t


def _tiles(T, rows):
    chunk = min(GLA_CHUNK, T)
    return dict(proj=_pick(rows, 512), merge=_pick(rows, 1024), ffn=_pick(T, 512),
                fox_q=_pick(T, 512), fox_k=_pick(T, 512), gla=_pick(T, 4 * chunk), gla_chunk=chunk)


def _pack_w_in(w_in):
    pts, acc = {}, 0
    for name, n in (("qa", FOX_W), ("ka", FOX_W), ("va", FOX_W), ("fa", H_A), ("qb", GLA_K), ("kb", GLA_K),
                    ("vb", GLA_V), ("rb", GLA_V), ("a1", GLA_RANK), ("ga", D_MODEL), ("gb", D_MODEL)):
        pts[name] = (acc, n)
        acc += n
    runs = []
    for name in _SEG:
        lo, n = pts[name]
        pad = _SEG[name][1] - n
        if runs and runs[-1][1] == lo and runs[-1][2] == 0:
            runs[-1][1:] = [lo + n, pad]
        else:
            runs.append([lo, lo + n, pad])
    cols = [jnp.pad(w_in[:, lo:hi], ((0, 0), (0, pad))) if pad else w_in[:, lo:hi] for lo, hi, pad in runs]
    return jnp.concatenate(cols, axis=1).astype(BF16)


def _layer(x, p, fox_past, s0, conv_prev, wts):
    (w_in, b_fgate, w_a2, b_a2, g_gla, w_a_out, w_b_out, w_o, ln1_g, ln1_b, w_up, conv_w, conv_b,
     w_down, ln2_g, ln2_b, w_pl, w_plg, ln3_g, ln3_b) = wts
    B, T, D = x.shape
    row2 = lambda a: a.reshape(1, -1).astype(F32)

    wcat = _pack_w_in(w_in)
    bf = jnp.pad(row2(b_fgate), ((0, 0), (0, LANES - H_A)))
    wa2 = jnp.pad(w_a2.astype(F32), ((0, LANES - GLA_RANK), (0, 0)))
    wa2h = wa2.astype(BF16)
    wa2l = (wa2 - wa2h.astype(F32)).astype(BF16)

    fold = fox_past is not None and T < LANES
    tile = _tiles(T, B * T if fold else T)
    rows = (lambda a: a.reshape(1, B * T, a.shape[-1])) if fold else (lambda a: a)
    unrows = (lambda a: a.reshape(B, T, a.shape[-1])) if fold else (lambda a: a)
    proj_out = _proj(rows(x), wcat, bf, wa2h, wa2l, row2(b_a2), tile["proj"],
                     vt_block=tile["fox_k"] if fox_past is None else None)
    proj_out = list(proj_out)
    proj_out[5] = jnp.transpose(proj_out[5], (0, 2, 1))
    if fold:
        proj_out = [unrows(o) for o in proj_out]
    (qa, ka_f, ka_b, va_f, va_b, logf, logf_pad, qb, kb, vb, rb, la, ga, gb) = proj_out

    if fox_past is None:
        oa = _fox_prompt(qa, ka_b, logf_pad, va_b, tile["fox_q"], tile["fox_k"])
    else:
        past_k, past_v, past_logf = fox_past
        P = past_k.shape[1]
        lk = -(-(P + T) // FOX_KEY_PAD) * FOX_KEY_PAD
        lf_all = jnp.concatenate([jnp.transpose(past_logf.astype(F32), (0, 2, 1)),
                                  jnp.transpose(logf, (0, 2, 1))], axis=2)
        lf_all = jnp.pad(lf_all, ((0, 0), (0, 0), (0, lk - P - T)))
        nc = _neg_cumsum(lf_all, _pick(lk, FOX_KEY_PAD))
        oa = _fox_sample(qa, jnp.transpose(past_k.astype(F32), (0, 2, 3, 1)),
                         jnp.transpose(past_v.astype(F32), (0, 2, 3, 1)), ka_b, va_b,
                         nc[:, :, :P].reshape(B, H_A // 2, 2, P),
                         nc[:, :, P:P + T].reshape(B, H_A // 2, 2, T))

    hb, s_fin = _gla(qb, kb, vb, la, rb, row2(g_gla), s0.astype(F32), tile["gla"], tile["gla_chunk"])

    x1 = unrows(_merge(rows(oa), rows(hb), rows(ga), rows(gb), rows(x), w_a_out.astype(BF16),
                       w_b_out.astype(BF16), w_o.astype(BF16), row2(ln1_g), row2(ln1_b),
                       tile["merge"]))
    y, conv_new = _ffn(x1, p, _ffn_pack(conv_prev.astype(F32)), _ffn_pack(w_up.astype(BF16)),
                       _ffn_pack(conv_w.astype(F32)), _ffn_pack(row2(conv_b)),
                       w_down.astype(BF16), row2(ln2_g), row2(ln2_b), w_pl.astype(BF16),
                       w_plg.astype(BF16), row2(ln3_g), row2(ln3_b), tile["ffn"])
    conv_new = _ffn_unpack(conv_new)
    k_out = ka_f.reshape(B, T, H_A, DH_A)
    v_out = va_f.reshape(B, T, H_A, DH_A)
    return y, k_out, v_out, logf, s_fin, conv_new


def kernel(x_prompt, x_sample, cache_fox_k, cache_fox_v, cache_fox_logf, state_gla, cache_ffn_conv, p_prompt, p_sample, w_in, b_fgate, w_a2, b_a2, g_gla, w_a_out, w_b_out, w_o, ln1_g, ln1_b, w_up, conv_w, conv_b, w_down, ln2_g, ln2_b, w_pl, w_plg, ln3_g, ln3_b):
    hp, hs = x_prompt, x_sample
    outs_p = [[] for _ in range(5)]
    outs_s = [[] for _ in range(5)]
    for i in range(DEPTH):
        wts = (w_in[i], b_fgate[i], w_a2[i], b_a2[i], g_gla[i], w_a_out[i], w_b_out[i], w_o[i],
               ln1_g[i], ln1_b[i], w_up[i], conv_w[i], conv_b[i], w_down[i], ln2_g[i], ln2_b[i],
               w_pl[i], w_plg[i], ln3_g[i], ln3_b[i])
        Bp = hp.shape[0]
        s0_p = jnp.zeros((Bp, H_B, DK_B, DV_B), F32)
        c0_p = jnp.zeros((Bp, CONV_W - 1, 2 * D_FF), F32)
        hp, *rest_p = _layer(hp, p_prompt[i], None, s0_p, c0_p, wts)
        hs, *rest_s = _layer(hs, p_sample[i], (cache_fox_k[i], cache_fox_v[i], cache_fox_logf[i]),
                             state_gla[i], cache_ffn_conv[i], wts)
        for dst, src in ((outs_p, rest_p), (outs_s, rest_s)):
            for lst, val in zip(dst, src):
                lst.append(val)
    return (hp, hs, *[jnp.stack(l) for l in outs_p], *[jnp.stack(l) for l in outs_s])
```

```python
import functools
import math

import jax
import jax.numpy as jnp
from jax import lax
from jax.experimental import pallas as pl
from jax.experimental.pallas import tpu as pltpu

F32 = jnp.float32
BF16 = jnp.bfloat16

D_MODEL = 1024
H_A, DH_A = 8, 64
FOX_W = H_A * DH_A
H_B, DK_B, DV_B = 4, 128, 256
GLA_K = H_B * DK_B
GLA_V = H_B * DV_B
GLA_RANK = 16
GLA_TAU = 16
D_FF = 2816
CONV_W = 3
PLE_DIM = 256
LN_EPS = 1e-5
DEPTH = 1
ALPHA = (2 * DEPTH) ** 0.25
GLA_CHUNK = 64
GLA_SUB = 16

LANES = 128
LOG2E = math.log2(math.e)
V7X_VMEM_LIMIT_BYTES = 56 * 1024 * 1024

_SEG = {}
_off = 0
for _name, _n in (("qa", FOX_W), ("ka", FOX_W), ("va", FOX_W), ("qb", GLA_K), ("kb", GLA_K),
                  ("vb", GLA_V), ("rb", GLA_V), ("ga", D_MODEL), ("gb", D_MODEL),
                  ("fa", LANES), ("a1", LANES)):
    _SEG[_name] = (_off, _n)
    _off += _n


def _params(*sem):
    return pltpu.CompilerParams(dimension_semantics=sem, vmem_limit_bytes=V7X_VMEM_LIMIT_BYTES)


def _const_spec(shape):
    nd = len(shape)
    return pl.BlockSpec(shape, lambda *_: (0,) * nd, pipeline_mode=pl.Buffered(1))


def _log_sigmoid(z):
    return jnp.minimum(z, 0.0) - jnp.log(1.0 + jnp.exp(-jnp.abs(z)))


def _sigmoid(z):
    return 1.0 / (1.0 + jnp.exp(-z))


def _split3(a):
    hi = a.astype(BF16)
    r = a - hi.astype(F32)
    mid = r.astype(BF16)
    lo = (r - mid.astype(F32)).astype(BF16)
    return hi, mid, lo


def _layer_norm(z, g, b):
    mu = jnp.mean(z, axis=-1, keepdims=True)
    zc = z - mu
    var = jnp.mean(zc * zc, axis=-1, keepdims=True)
    return zc * lax.rsqrt(var + LN_EPS) * g + b


def _dot(a, b):
    return jnp.dot(a, b, preferred_element_type=F32)


def _dot_nt(a, b):
    return lax.dot_general(a, b, (((1,), (1,)), ((), ())), preferred_element_type=F32)


def _dot_tn(a, b):
    return lax.dot_general(a, b, (((0,), (0,)), ((), ())), preferred_element_type=F32)


def _proj_kernel(x_ref, w_ref, bf_ref, wa2h_ref, wa2l_ref, ba2_ref,
                 qa_ref, kaf_ref, kab_ref, vaf_ref, vab_ref, logf_ref, logfp_ref,
                 qb_ref, kb_ref, vb_ref, rb_ref, la_ref, ga_ref, gb_ref, carry_ref, *, values_transposed):
    xb = x_ref[...].astype(BF16)

    def seg(name):
        lo, n = _SEG[name]
        return _dot(xb, w_ref[:, lo:lo + n])

    logf = _log_sigmoid(seg("fa") + bf_ref[...])
    qa_ref[...] = (seg("qa") * (DH_A ** -0.5 * LOG2E)).astype(BF16)
    ka = seg("ka")
    kaf_ref[...] = ka
    kab_ref[...] = ka.astype(BF16)
    va = seg("va")
    vaf_ref[...] = va
    if values_transposed:
        for p in range(H_A // 2):
            vab_ref[p] = va[:, p * LANES:(p + 1) * LANES].T.astype(BF16)
    else:
        vab_ref[...] = va.astype(BF16)
    qb_ref[...] = seg("qb") * (DK_B ** -0.5)
    kb_ref[...] = seg("kb")
    vb_ref[...] = seg("vb").astype(BF16)
    rb_ref[...] = seg("rb")
    ga_ref[...] = seg("ga").astype(BF16)
    gb_ref[...] = seg("gb").astype(BF16)
    a1 = seg("a1")
    a1h = a1.astype(BF16)
    a1l = (a1 - a1h.astype(F32)).astype(BF16)
    z = _dot(a1h, wa2h_ref[...]) + _dot(a1l, wa2h_ref[...]) + _dot(a1h, wa2l_ref[...]) + ba2_ref[...]
    la_ref[...] = _log_sigmoid(z) * (1.0 / GLA_TAU)
    logf_ref[...] = logf.T[:H_A, :]
    if values_transposed:
        logfp_ref[...] = _fox_bias_lanes(logf, carry_ref, pl.program_id(1) == 0).astype(logfp_ref.dtype)
    else:
        logfp_ref[...] = logf


def _proj(x, wcat, bf, wa2h, wa2l, ba2, tm, vt_block=None):
    B, T, D = x.shape
    grid = (B, T // tm)

    def row(n, dt):
        return pl.BlockSpec((None, tm, n), lambda b, t: (b, t, 0)), jax.ShapeDtypeStruct((B, T, n), dt)

    outs = [row(FOX_W, BF16), row(FOX_W, F32), row(FOX_W, BF16), row(FOX_W, F32), row(FOX_W, BF16),
            row(H_A, F32), row(LANES, F32), row(GLA_K, F32), row(GLA_K, F32), row(GLA_V, BF16), row(GLA_V, F32),
            row(GLA_K, F32), row(D_MODEL, BF16), row(D_MODEL, BF16)]
    outs[5] = (pl.BlockSpec((None, H_A, tm), lambda b, t: (b, 0, t)), jax.ShapeDtypeStruct((B, H_A, T), F32))
    if vt_block is not None:
        per = vt_block // tm
        outs[4] = (pl.BlockSpec((None, H_A // 2, None, LANES, tm), lambda b, t: (b, 0, t // per, 0, t % per)),
                   jax.ShapeDtypeStruct((B, H_A // 2, T // vt_block, LANES, vt_block), BF16))
        outs[6] = row(LANES, BF16)
    return pl.pallas_call(
        functools.partial(_proj_kernel, values_transposed=vt_block is not None),
        grid=grid,
        in_specs=[pl.BlockSpec((None, tm, D), lambda b, t: (b, t, 0)),
                  _const_spec(wcat.shape), _const_spec(bf.shape), _const_spec(wa2h.shape),
                  _const_spec(wa2l.shape), _const_spec(ba2.shape)],
        out_specs=[o[0] for o in outs],
        out_shape=[o[1] for o in outs],
        scratch_shapes=[pltpu.VMEM((1, LANES), F32)],
        compiler_params=_params("parallel", "arbitrary"),
        name="proj",
    )(x, wcat, bf, wa2h, wa2l, ba2)


def _cumsum_kernel(lf_ref, nc_ref):
    x = lf_ref[...]
    n = x.shape[1]
    lane = lax.broadcasted_iota(jnp.int32, (1, n), 1)
    shift = 1
    while shift < n:
        x = x + jnp.where(lane >= shift, pltpu.roll(x, shift, 1), 0.0)
        shift *= 2
    nc_ref[...] = -x


def _neg_cumsum(lf_t):
    B, H, L = lf_t.shape
    flat = lf_t.reshape(B * H, L)
    return pl.pallas_call(
        _cumsum_kernel,
        out_shape=jax.ShapeDtypeStruct(flat.shape, F32),
        compiler_params=pltpu.CompilerParams(vmem_limit_bytes=V7X_VMEM_LIMIT_BYTES),
        name="cumsum",
    )(flat).reshape(B, H, L)


FOX_KEY_PAD = 512
FOX_BIAS_PARTS = 3
FOX_UNROLL = 4
FOX_SUM_ROWS = 16


def _fox_bias_lanes(x, carry_ref, first):
    @pl.when(first)
    def _():
        carry_ref[...] = jnp.zeros_like(carry_ref)

    tl = x.shape[0]
    row = lax.broadcasted_iota(jnp.int32, (tl, 1), 0)
    cs = x
    shift = 1
    while shift < tl:
        cs = cs + jnp.where(row >= shift, pltpu.roll(cs, shift, 0), 0.0)
        shift *= 2
    cs = cs + carry_ref[...]
    carry_ref[...] = cs[tl - 1:tl, :]
    parts =[p.astype(F32) for p in _split3(cs * (-LOG2E))]
    lane = lax.broadcasted_iota(jnp.int32, (1, LANES), 1)
    out = jnp.zeros_like(x)
    for j in reversed(range(FOX_BIAS_PARTS)):
        moved = parts[j] if j == 0 else pltpu.roll(parts[j], j * H_A, 1)
        out = jnp.where(jnp.logical_and(lane >= j * H_A, lane < (j + 1) * H_A), moved, out)
    return out


def _fox_prompt_kernel(q_ref, k_ref, kb_ref, vt_ref, o_ref, m_ref, acc_ref,
                       sa_ref, sb_ref, bma_ref, bmb_ref, *, tq, tk):
    i = pl.program_id(2)
    lane = lax.broadcasted_iota(jnp.int32, (1, LANES), 1)
    q = q_ref[...]
    zero = jnp.zeros_like(q)
    qa = []
    for h in range(2):
        own = (lane < DH_A) if h == 0 else (lane >= DH_A)
        head = 2 * pl.program_id(1) + h
        ones = jnp.logical_and(jnp.bitwise_and(lane, H_A - 1) == head, lane < FOX_BIAS_PARTS * H_A)
        ones = jnp.broadcast_to(jnp.where(ones, 1.0, 0.0).astype(q.dtype), q.shape)
        qa.append(jnp.concatenate([jnp.where(own, q, zero), ones], axis=1))
    m_ref[...] = jnp.full_like(m_ref, -jnp.inf)
    acc_ref[...] = jnp.zeros_like(acc_ref)
    sum_rows = jnp.ones((FOX_SUM_ROWS, tk), BF16)
    per = tq // tk

    def scores(j, s_ref, bm_ref):
        start = pl.multiple_of(j * tk, tk)
        ka = jnp.concatenate([k_ref[pl.ds(start, tk), :], kb_ref[pl.ds(start, tk), :]], axis=1)
        for h in range(2):
            st = _dot_nt(ka, qa[h])
            s_ref[h] = st
            bm_ref[h] = jnp.max(st, axis=0, keepdims=True)

    def soft_pv(j, s_ref, bm_ref, diag):
        for h in range(2):
            st = s_ref[h]
            if diag is not None:
                r = lax.broadcasted_iota(jnp.int32, (tk, tq), 0)
                c = lax.broadcasted_iota(jnp.int32, (tk, tq), 1)
                st = jnp.where(r + diag * tk <= c, st, -jnp.inf)
                bm = jnp.max(st, axis=0, keepdims=True)
            else:
                bm = bm_ref[h]
            m_old = m_ref[h]
            m_new = jnp.maximum(m_old, bm)
            p = jnp.exp2(st - m_new)
            a = jnp.exp2(m_old - m_new)
            vt = jnp.concatenate([vt_ref[j, h * DH_A:(h + 1) * DH_A, :], sum_rows], axis=0)
            acc_ref[h] = a * acc_ref[h] + _dot(vt, p.astype(BF16))
            m_ref[h] = m_new

    slots = ((sa_ref, bma_ref), (sb_ref, bmb_ref))

    def run(j0, diags, prefetch_after):
        for u, diag in enumerate(diags):
            if u + 1 < len(diags) or prefetch_after:
                scores(j0 + u + 1, *slots[(u + 1) % 2])
            soft_pv(j0 + u, *slots[u % 2], diag)

    scores(0, *slots[0])

    def body(t, carry):
        run(FOX_UNROLL * t, [None] * FOX_UNROLL, True)
        return carry

    full = i * per
    log_unroll = FOX_UNROLL.bit_length() - 1
    lax.fori_loop(0, lax.shift_right_logical(full, log_unroll), body, 0)
    rest = jnp.bitwise_and(full, FOX_UNROLL - 1)
    for n in range(0, FOX_UNROLL, per):
        @pl.when(rest == n)
        def _(n=n):
            run(full - n, [None] * n + list(range(per)), False)

    ot = jnp.concatenate([acc_ref[h, :DH_A, :] / acc_ref[h, DH_A:DH_A + 1, :] for h in range(2)],
                         axis=0)
    o_ref[...] = ot.T.astype(o_ref.dtype)


def _fox_prompt(qa, ka, kbias, vt, tq, tk):
    B, T, _ = qa.shape
    nk = T // tk
    return pl.pallas_call(
        functools.partial(_fox_prompt_kernel, tq=tq, tk=tk),
        grid=(B, H_A // 2, T // tq),
        in_specs=[pl.BlockSpec((None, tq, LANES), lambda b, p, i: (b, i, p)),
                  pl.BlockSpec((None, T, LANES), lambda b, p, i: (b, 0, p)),
                  pl.BlockSpec((None, T, LANES), lambda b, p, i: (b, 0, 0)),
                  pl.BlockSpec((None, None, nk, LANES, tk), lambda b, p, i: (b, p, 0, 0, 0))],
        out_specs=pl.BlockSpec((None, tq, LANES), lambda b, p, i: (b, i, p)),
        out_shape=jax.ShapeDtypeStruct((B, T, FOX_W), BF16),
        scratch_shapes=[pltpu.VMEM((2, 1, tq), F32),
                        pltpu.VMEM((2, DH_A + FOX_SUM_ROWS, tq), F32),
                        pltpu.VMEM((2, tk, tq), F32), pltpu.VMEM((2, tk, tq), F32),
                        pltpu.VMEM((2, 1, tq), F32), pltpu.VMEM((2, 1, tq), F32)],
        compiler_params=_params("parallel", "parallel", "arbitrary"),
        name="fox_prompt",
    )(qa, ka, kbias, vt)


def _fox_sample_kernel(q_ref, kt_ref, vt_ref, kn_ref, vn_ref, ncp_ref, ncn_ref, o_ref):
    first = lax.broadcasted_iota(jnp.int32, (1, LANES), 1) < DH_A
    q = q_ref[...]
    zero = jnp.zeros_like(q)
    tq = q.shape[0]
    kt = jnp.concatenate([kt_ref[0], kt_ref[1]], axis=0).astype(BF16)
    vt = jnp.concatenate([vt_ref[0], vt_ref[1]], axis=0).astype(BF16)
    kn = kn_ref[...]
    vn = vn_ref[...]
    r = lax.broadcasted_iota(jnp.int32, (tq, tq), 0)
    c = lax.broadcasted_iota(jnp.int32, (tq, tq), 1)
    outs = []
    for h in range(2):
        qh = jnp.where(first, q, zero) if h == 0 else jnp.where(first, zero, q)
        sp = _dot(qh, kt) + ncp_ref[h:h + 1, :] * LOG2E
        sn = jnp.where(c <= r, _dot_nt(qh, kn) + ncn_ref[h:h + 1, :] * LOG2E, -jnp.inf)
        m = jnp.maximum(jnp.max(sp, axis=1, keepdims=True), jnp.max(sn, axis=1, keepdims=True))
        pp = jnp.exp2(sp - m)
        pn = jnp.exp2(sn - m)
        l = jnp.sum(pp, axis=1, keepdims=True) + jnp.sum(pn, axis=1, keepdims=True)
        outs.append((_dot_nt(pp.astype(BF16), vt) + _dot(pn.astype(BF16), vn)) / l)
    o_ref[...] = jnp.where(first, outs[0], outs[1]).astype(o_ref.dtype)


def _fox_sample(qa, kt, vt, kn, vn, ncp, ncn):
    B, Tq, _ = qa.shape
    P = kt.shape[-1]
    new = pl.BlockSpec((None, Tq, LANES), lambda b, p: (b, 0, p))
    old = pl.BlockSpec((None, 2, DH_A, P), lambda b, p: (b, p, 0, 0))
    return pl.pallas_call(
        _fox_sample_kernel,
        grid=(B, H_A // 2),
        in_specs=[new, old, old, new, new,
                  pl.BlockSpec((None, None, 2, P), lambda b, p: (b, p, 0, 0)),
                  pl.BlockSpec((None, None, 2, Tq), lambda b, p: (b, p, 0, 0))],
        out_specs=new,
        out_shape=jax.ShapeDtypeStruct((B, Tq, FOX_W), BF16),
        compiler_params=_params("parallel", "parallel"),
        name="fox_sample",
    )(qa, kt, vt, kn, vn, ncp, ncn)


def _gla_kernel(q_ref, k_ref, v_ref, la_ref, rb_ref, g_ref, s0_ref, hb_ref, sfin_ref, st_ref,
                *, chunk, sub):
    t = pl.program_id(1)

    @pl.when(t == 0)
    def _():
        for h in range(H_B):
            st_ref[h] = s0_ref[h].T

    tc = q_ref.shape[0]
    nsub = chunk // sub
    nchunk = tc // chunk
    r = lax.broadcasted_iota(jnp.int32, (tc, tc), 0)
    c = lax.broadcasted_iota(jnp.int32, (tc, tc), 1)
    log_chunk = chunk.bit_length() - 1
    log_sub = sub.bit_length() - 1
    same_chunk = jnp.right_shift(r, log_chunk) == jnp.right_shift(c, log_chunk)
    near = jnp.logical_and(c <= r, same_chunk)
    far = jnp.right_shift(c, log_chunk) < jnp.right_shift(r, log_chunk)
    tri = jnp.where(c <= r, 1.0, 0.0).astype(BF16)
    hi, mid, lo = _split3(la_ref[...])
    b_all = (_dot(tri, hi) + _dot(tri, mid) + _dot(tri, lo)) * LOG2E

    row = lax.broadcasted_iota(jnp.int32, (tc, 1), 0)
    row_sub = jnp.bitwise_and(jnp.right_shift(row, log_sub), nsub - 1)
    row_chunk = jnp.right_shift(row, log_chunk)

    def rows_of(refs, n):
        return jnp.concatenate([jnp.broadcast_to(rf, (n, DK_B)) for rf in refs], axis=0)

    def zeros(n):
        return jnp.zeros((n, DK_B), F32)

    for h in range(H_B):
        kc = slice(h * DK_B, (h + 1) * DK_B)
        vc = slice(h * DV_B, (h + 1) * DV_B)
        b = b_all[:, kc]
        q = q_ref[:, kc]
        k = k_ref[:, kc]
        v = v_ref[:, vc]
        b_last = b[tc - 1:tc, :]
        st = st_ref[h]
        sub_refs = [b[i * sub:i * sub + 1, :] for i in range(tc // sub)]
        chunk_refs = [b[ci * chunk:ci * chunk + 1, :] for ci in range(nchunk)]

        qt = q * jnp.exp2(b - rows_of(sub_refs, sub))
        qx = [jnp.where(row_sub == i, qt, 0.0) for i in range(nsub)]
        kx = []
        for i in range(nsub):
            pieces = []
            for ci in range(nchunk):
                lo_row, n = ci * chunk, (i + 1) * sub
                pieces.append(k[lo_row:lo_row + n] * jnp.exp2(sub_refs[ci * nsub + i] - b[lo_row:lo_row + n]))
                if n < chunk:
                    pieces.append(zeros(chunk - n))
            kx.append(jnp.concatenate(pieces, axis=0) if len(pieces) > 1 else pieces[0])
        a = jnp.where(near, _dot_nt(jnp.concatenate(qx, axis=1).astype(BF16),
                                    jnp.concatenate(kx, axis=1).astype(BF16)), 0.0)
        if nchunk > 1:
            qc = q * jnp.exp2(b - rows_of(chunk_refs, chunk))
            qx = [jnp.where(row_chunk == ci, qc, 0.0) for ci in range(1, nchunk)]
            kx = []
            for ci in range(1, nchunk):
                n = ci * chunk
                kx.append(jnp.concatenate([k[:n] * jnp.exp2(chunk_refs[ci] - b[:n]), zeros(tc - n)], axis=0))
            a = jnp.where(far, _dot_nt(jnp.concatenate(qx, axis=1).astype(BF16),
                                       jnp.concatenate(kx, axis=1).astype(BF16)), a)
        o = _dot(a.astype(BF16), v) + _dot_nt((q * jnp.exp2(b)).astype(BF16), st.astype(BF16))
        kd = (k * jnp.exp2(b_last - b)).astype(BF16)
        st_ref[h] = st * jnp.exp2(b_last) + _dot_tn(v, kd)
        ms = jnp.mean(o * o, axis=-1, keepdims=True)
        on = o * lax.rsqrt(ms + LN_EPS) * g_ref[:, vc]
        rbv = rb_ref[:, vc]
        hb_ref[:, vc] = (on * (rbv * _sigmoid(rbv))).astype(hb_ref.dtype)

    @pl.when(t == pl.num_programs(1) - 1)
    def _():
        for h in range(H_B):
            sfin_ref[h] = st_ref[h].T


def _gla(qb, kb, vb, la, rb, g, s0, tc, chunk):
    B, T, _ = qb.shape
    sub = min(GLA_SUB, chunk)

    def row(n):
        return pl.BlockSpec((None, tc, n), lambda b, t: (b, t, 0))

    st_spec = pl.BlockSpec((None, H_B, DK_B, DV_B), lambda b, t: (b, 0, 0, 0))
    return pl.pallas_call(
        functools.partial(_gla_kernel, chunk=chunk, sub=sub),
        grid=(B, T // tc),
        in_specs=[row(GLA_K), row(GLA_K), row(GLA_V), row(GLA_K), row(GLA_V),
                  _const_spec(g.shape), st_spec],
        out_specs=[row(GLA_V), st_spec],
        out_shape=[jax.ShapeDtypeStruct((B, T, GLA_V), BF16),
                   jax.ShapeDtypeStruct((B, H_B, DK_B, DV_B), F32)],
        scratch_shapes=[pltpu.VMEM((H_B, DV_B, DK_B), F32)],
        compiler_params=_params("parallel", "arbitrary"),
        name="gla",
    )(qb, kb, vb, la, rb, g, s0)


def _merge_kernel(oa_ref, hb_ref, ga_ref, gb_ref, x_ref, wa_ref, wb_ref, wo_ref, g_ref, b_ref, x1_ref):
    ya = _dot(oa_ref[...], wa_ref[...])
    yb = _dot(hb_ref[...], wb_ref[...])
    merged = _sigmoid(ga_ref[...].astype(F32)) * ya + _sigmoid(gb_ref[...].astype(F32)) * yb
    z = ALPHA * x_ref[...] + _dot(merged.astype(BF16), wo_ref[...])
    x1_ref[...] = _layer_norm(z, g_ref[...], b_ref[...])


def _merge(oa, hb, ga, gb, x, wa, wb, wo, g, b, tm):
    B, T, D = x.shape

    def row(n):
        return pl.BlockSpec((None, tm, n), lambda bb, t: (bb, t, 0))

    return pl.pallas_call(
        _merge_kernel,
        grid=(B, T // tm),
        in_specs=[row(FOX_W), row(GLA_V), row(D), row(D), row(D),
                  _const_spec(wa.shape), _const_spec(wb.shape), _const_spec(wo.shape),
                  _const_spec(g.shape), _const_spec(b.shape)],
        out_specs=row(D),
        out_shape=jax.ShapeDtypeStruct((B, T, D), F32),
        compiler_params=_params("parallel", "parallel"),
        name="merge",
    )(oa, hb, ga, gb, x, wa, wb, wo, g, b)


FFN_GROUP = 256


def _gelu_tanh(x):
    return 0.5 * x * (1.0 + jnp.tanh(math.sqrt(2.0 / math.pi) * (x + 0.044715 * (x * x * x))))


def _ffn_pack(a):
    return jnp.concatenate([a[..., k * D_FF + j * FFN_GROUP:k * D_FF + (j + 1) * FFN_GROUP]
                            for j in range(D_FF // FFN_GROUP) for k in range(2)], axis=-1)


def _ffn_unpack(a):
    lead = a.shape[:-1]
    return a.reshape(*lead, D_FF // FFN_GROUP, 2, FFN_GROUP).swapaxes(-3, -2).reshape(*lead, 2 * D_FF)


def _ffn_kernel(x1_ref, p_ref, cprev_ref, wup_ref, cw_ref, cb_ref, wdn_ref, g2_ref, b2_ref,
                wpl_ref, wplg_ref, g3_ref, b3_ref, y_ref, cnew_ref, carry_ref):
    @pl.when(pl.program_id(1) == 0)
    def _():
        carry_ref[...] = cprev_ref[...]

    x1 = x1_ref[...]
    tm = x1.shape[0]
    up = _dot(x1.astype(BF16), wup_ref[...])
    prev = carry_ref[...]
    row = lax.broadcasted_iota(jnp.int32, (tm, 1), 0)
    up_m1 = jnp.where(row == 0, prev[1:2, :], pltpu.roll(up, 1, 0))
    up_m2 = jnp.where(row == 0, prev[0:1, :], jnp.where(row == 1, prev[1:2, :], pltpu.roll(up, 2, 0)))
    conv = cb_ref[...] + cw_ref[0:1, :] * up_m2 + cw_ref[1:2, :] * up_m1 + cw_ref[2:3, :] * up
    last2 = up[tm - 2:tm, :]
    carry_ref[...] = last2
    cnew_ref[...] = last2
    hgl = jnp.concatenate(
        [conv[:, 2 * j * FFN_GROUP:(2 * j + 1) * FFN_GROUP]
         * _gelu_tanh(conv[:, (2 * j + 1) * FFN_GROUP:(2 * j + 2) * FFN_GROUP])
         for j in range(D_FF // FFN_GROUP)], axis=1).astype(BF16)
    z2 = ALPHA * x1 + _dot(hgl, wdn_ref[...])
    x2 = _layer_norm(z2, g2_ref[...], b2_ref[...])
    e = _dot(p_ref[...].astype(BF16), wpl_ref[...]) * _sigmoid(_dot(x2.astype(BF16), wplg_ref[...]))
    y_ref[...] = _layer_norm(ALPHA * x2 + e, g3_ref[...], b3_ref[...])


def _ffn(x1, p, cprev, wup, cw, cb, wdn, g2, b2, wpl, wplg, g3, b3, tm):
    B, T, D = x1.shape

    def row(n):
        return pl.BlockSpec((None, tm, n), lambda bb, t: (bb, t, 0))

    cspec = pl.BlockSpec((None, CONV_W - 1, 2 * D_FF), lambda bb, t: (bb, 0, 0))
    consts = (wup, cw, cb, wdn, g2, b2, wpl, wplg, g3, b3)
    return pl.pallas_call(
        _ffn_kernel,
        grid=(B, T // tm),
        in_specs=[row(D), row(PLE_DIM), cspec] + [_const_spec(c.shape) for c in consts],
        out_specs=[row(D), cspec],
        out_shape=[jax.ShapeDtypeStruct((B, T, D), F32),
                   jax.ShapeDtypeStruct((B, CONV_W - 1, 2 * D_FF), F32)],
        scratch_shapes=[pltpu.VMEM((CONV_W - 1, 2 * D_FF), F32)],
        compiler_params=_params("parallel", "arbitrary"),
        name="ffn",
    )(x1, p, cprev, *consts)


def _pick(T, want):
    t = min(T, want)
    while T % t:
        t //= 2
    return t


def _tiles(T, rows):
    chunk = min(GLA_CHUNK, T)
    return dict(proj=_pick(rows, 512), merge=_pick(rows, 1024), ffn=_pick(T, 512),
                fox_q=_pick(T, 512), fox_k=_pick(T, 512), gla=_pick(T, 4 * chunk), gla_chunk=chunk)


def _pack_w_in(w_in):
    pts, acc = {}, 0
    for name, n in (("qa", FOX_W), ("ka", FOX_W), ("va", FOX_W), ("fa", H_A), ("qb", GLA_K), ("kb", GLA_K),
                    ("vb", GLA_V), ("rb", GLA_V), ("a1", GLA_RANK), ("ga", D_MODEL), ("gb", D_MODEL)):
        pts[name] = (acc, n)
        acc += n
    runs = []
    for name in _SEG:
        lo, n = pts[name]
        pad = _SEG[name][1] - n
        if runs and runs[-1][1] == lo and runs[-1][2] == 0:
            runs[-1][1:] = [lo + n, pad]
        else:
            runs.append([lo, lo + n, pad])
    cols = [jnp.pad(w_in[:, lo:hi], ((0, 0), (0, pad))) if pad else w_in[:, lo:hi] for lo, hi, pad in runs]
    return jnp.concatenate(cols, axis=1).astype(BF16)


def _layer(x, p, fox_past, s0, conv_prev, wts):
    (w_in, b_fgate, w_a2, b_a2, g_gla, w_a_out, w_b_out, w_o, ln1_g, ln1_b, w_up, conv_w, conv_b,
     w_down, ln2_g, ln2_b, w_pl, w_plg, ln3_g, ln3_b) = wts
    B, T, D = x.shape
    row2 = lambda a: a.reshape(1, -1).astype(F32)

    wcat = _pack_w_in(w_in)
    bf = jnp.pad(row2(b_fgate), ((0, 0), (0, LANES - H_A)))
    wa2 = jnp.pad(w_a2.astype(F32), ((0, LANES - GLA_RANK), (0, 0)))
    wa2h = wa2.astype(BF16)
    wa2l = (wa2 - wa2h.astype(F32)).astype(BF16)

    fold = fox_past is not None and T < LANES
    tile = _tiles(T, B * T if fold else T)
    rows = (lambda a: a.reshape(1, B * T, a.shape[-1])) if fold else (lambda a: a)
    unrows = (lambda a: a.reshape(B, T, a.shape[-1])) if fold else (lambda a: a)
    proj_out = _proj(rows(x), wcat, bf, wa2h, wa2l, row2(b_a2), tile["proj"],
                     vt_block=tile["fox_k"] if fox_past is None else None)
    proj_out = list(proj_out)
    proj_out[5] = jnp.transpose(proj_out[5], (0, 2, 1))
    if fold:
        proj_out = [unrows(o) for o in proj_out]
    (qa, ka_f, ka_b, va_f, va_b, logf, logf_pad, qb, kb, vb, rb, la, ga, gb) = proj_out

    if fox_past is None:
        oa = _fox_prompt(qa, ka_b, logf_pad, va_b, tile["fox_q"], tile["fox_k"])
    else:
        past_k, past_v, past_logf = fox_past
        P = past_k.shape[1]
        lk = -(-(P + T) // FOX_KEY_PAD) * FOX_KEY_PAD
        lf_all = jnp.concatenate([jnp.transpose(past_logf.astype(F32), (0, 2, 1)),
                                  jnp.transpose(logf, (0, 2, 1))], axis=2)
        lf_all = jnp.pad(lf_all, ((0, 0), (0, 0), (0, lk - P - T)))
        nc = _neg_cumsum(lf_all)
        oa = _fox_sample(qa, jnp.transpose(past_k.astype(F32), (0, 2, 3, 1)),
                         jnp.transpose(past_v.astype(F32), (0, 2, 3, 1)), ka_b, va_b,
                         nc[:, :, :P].reshape(B, H_A // 2, 2, P),
                         nc[:, :, P:P + T].reshape(B, H_A // 2, 2, T))

    hb, s_fin = _gla(qb, kb, vb, la, rb, row2(g_gla), s0.astype(F32), tile["gla"], tile["gla_chunk"])

    x1 = unrows(_merge(rows(oa), rows(hb), rows(ga), rows(gb), rows(x), w_a_out.astype(BF16),
                       w_b_out.astype(BF16), w_o.astype(BF16), row2(ln1_g), row2(ln1_b),
                       tile["merge"]))
    y, conv_new = _ffn(x1, p, _ffn_pack(conv_prev.astype(F32)), _ffn_pack(w_up.astype(BF16)),
                       _ffn_pack(conv_w.astype(F32)), _ffn_pack(row2(conv_b)),
                       w_down.astype(BF16), row2(ln2_g), row2(ln2_b), w_pl.astype(BF16),
                       w_plg.astype(BF16), row2(ln3_g), row2(ln3_b), tile["ffn"])
    conv_new = _ffn_unpack(conv_new)
    k_out = ka_f.reshape(B, T, H_A, DH_A)
    v_out = va_f.reshape(B, T, H_A, DH_A)
    return y, k_out, v_out, logf, s_fin, conv_new


def kernel(x_prompt, x_sample, cache_fox_k, cache_fox_v, cache_fox_logf, state_gla, cache_ffn_conv, p_prompt, p_sample, w_in, b_fgate, w_a2, b_a2, g_gla, w_a_out, w_b_out, w_o, ln1_g, ln1_b, w_up, conv_w, conv_b, w_down, ln2_g, ln2_b, w_pl, w_plg, ln3_g, ln3_b):
    hp, hs = x_prompt, x_sample
    outs_p = [[] for _ in range(5)]
    outs_s = [[] for _ in range(5)]
    for i in range(DEPTH):
        wts = (w_in[i], b_fgate[i], w_a2[i], b_a2[i], g_gla[i], w_a_out[i], w_b_out[i], w_o[i],
               ln1_g[i], ln1_b[i], w_up[i], conv_w[i], conv_b[i], w_down[i], ln2_g[i], ln2_b[i],
               w_pl[i], w_plg[i], ln3_g[i], ln3_b[i])
        Bp = hp.shape[0]
        s0_p = jnp.zeros((Bp, H_B, DK_B, DV_B), F32)
        c0_p = jnp.zeros((Bp, CONV_W - 1, 2 * D_FF), F32)
        hp, *rest_p = _layer(hp, p_prompt[i], None, s0_p, c0_p, wts)
        hs, *rest_s = _layer(hs, p_sample[i], (cache_fox_k[i], cache_fox_v[i], cache_fox_logf[i]),
                             state_gla[i], cache_ffn_conv[i], wts)
        for dst, src in ((outs_p, rest_p), (outs_s, rest_s)):
            for lst, val in zip(dst, src):
                lst.append(val)
    return (hp, hs, *[jnp.stack(l) for l in outs_p], *[jnp.stack(l) for l in outs_s])
```
